```python
import math
import numpy as np
import jax
import jax.numpy as jnp
from jax import lax

D_MODEL = 1024
BATCH = 8
SEQ = 4096
DEPTH = 2

BRANCH_W = 512
HEAD_DIM = 64
NSA_HEADS = 8
NSA_KV_GROUPS = 2
NSA_HPG = NSA_HEADS // NSA_KV_GROUPS
CMP_BLOCK = 32
CMP_STRIDE = 16
CMP_HIDDEN = 256
SLC_BLOCK = 64
SLC_TOP = 16
WINDOW = 512
Q_BLOCK = 128
S5_GROUPS = 32
S5_GROUP_CH = 16
S5_STATE = 64
RWKV_HEADS = 8
DECAY_LORA = 64
AAA_LORA = 64
N_BRANCH = 3
ROPE_THETA = 10000.0
NORM_EPS = 1e-6
RWKV_LN_EPS = 64e-5
NEG_BIG = -1e30
KV_W = NSA_KV_GROUPS * HEAD_DIM
RWKV_MIX_W = 3 * BRANCH_W + DECAY_LORA + AAA_LORA
IN_SPLITS = (BRANCH_W, 6 * KV_W, 3 * NSA_HEADS, BRANCH_W,
             BRANCH_W, BRANCH_W,
             RWKV_MIX_W, BRANCH_W,
             N_BRANCH * D_MODEL)
IN_W = 5 * BRANCH_W + 6 * KV_W + 3 * NSA_HEADS + RWKV_MIX_W + N_BRANCH * D_MODEL

kernel_name = "hybrid_nsa_s5_rwkv7_adaln"


def _split(t, sizes):
    idx = np.cumsum(np.array(sizes))[:-1].tolist()
    return jnp.split(t, idx, axis=-1)


def _rmsnorm(x, w):
    xf = x.astype(jnp.float32)
    y = xf * lax.rsqrt(jnp.mean(xf * xf, axis=-1, keepdims=True) + NORM_EPS)
    return (y * w.astype(jnp.float32)).astype(x.dtype)


def _rope_tables(seq):
    half = HEAD_DIM // 2
    inv = jnp.exp(-math.log(ROPE_THETA) * jnp.arange(half, dtype=jnp.float32) / half)
    ang = jnp.arange(seq, dtype=jnp.float32)[:, None] * inv[None, :]
    return jnp.cos(ang), jnp.sin(ang)


def _rope(t, cos, sin):
    t1, t2 = jnp.split(t.astype(jnp.float32), 2, axis=-1)
    cc = cos[None, :, None, :]
    ss = sin[None, :, None, :]
    return jnp.concatenate([t1 * cc - t2 * ss, t1 * ss + t2 * cc], axis=-1).astype(t.dtype)


def _compress(t, pos, w1, w2):
    bn, s, g, dh = t.shape
    n_cmp = (s - CMP_BLOCK) // CMP_STRIDE + 1
    idx = CMP_STRIDE * np.arange(n_cmp)[:, None] + np.arange(CMP_BLOCK)[None, :]
    blk = t[:, idx] + pos[None, None, :, None, :]
    blk = jnp.moveaxis(blk, 3, 2).reshape(bn, n_cmp, g, CMP_BLOCK * dh)
    return jax.nn.silu(blk @ w1) @ w2


def _nsa(q, kc, vc, ks, vs, kw, vw, gates, pos_k, pos_v, w1k, w2k, w1v, w2v):
    bn, s = q.shape[:2]
    G, R, Dh = NSA_KV_GROUPS, NSA_HPG, HEAD_DIM
    scale = Dh ** -0.5
    k_cmp = _compress(kc, pos_k, w1k, w2k)
    v_cmp = _compress(vc, pos_v, w1v, w2v)
    n_cmp = k_cmp.shape[1]
    cmp_end = CMP_STRIDE * np.arange(n_cmp) + CMP_BLOCK - 1
    n_slc = s // SLC_BLOCK
    top = min(SLC_TOP, n_slc)
    cs = CMP_STRIDE * np.arange(n_cmp)[:, None]
    ss = SLC_BLOCK * np.arange(n_slc)[None, :]
    overlap = np.clip(np.minimum(cs + CMP_BLOCK, ss + SLC_BLOCK) - np.maximum(cs, ss), 0, None)
    cmp_to_slc = jnp.asarray(overlap / CMP_BLOCK, dtype=jnp.float32)
    ks_blk = jnp.moveaxis(ks.reshape(bn, n_slc, SLC_BLOCK, G, Dh), 3, 1)
    vs_blk = jnp.moveaxis(vs.reshape(bn, n_slc, SLC_BLOCK, G, Dh), 3, 1)
    kw_pad = jnp.pad(kw, ((0, 0), (WINDOW, 0), (0, 0), (0, 0)))
    vw_pad = jnp.pad(vw, ((0, 0), (WINDOW, 0), (0, 0), (0, 0)))
    n_qb = s // Q_BLOCK
    q_b = jnp.moveaxis(q.reshape(bn, n_qb, Q_BLOCK, G, R, Dh), 1, 0)
    g_b = jnp.moveaxis(gates.reshape(bn, n_qb, Q_BLOCK, G, R, 3), 1, 0)
    starts = jnp.arange(n_qb, dtype=jnp.int32) * Q_BLOCK
    b_ix = jnp.arange(bn)[:, None, None, None]
    g_ix = jnp.arange(G)[None, None, :, None]
    j_slc = jnp.arange(n_slc)

    def block(args):
        qb, gb, s0 = args
        t = s0 + jnp.arange(Q_BLOCK)
        s_c = jnp.einsum('bqgrd,bngd->bqgrn', qb, k_cmp).astype(jnp.float32) * scale
        valid_c = (cmp_end[None, :] <= t[:, None])[None, :, None, None, :]
        p_c = jax.nn.softmax(jnp.where(valid_c, s_c, NEG_BIG), axis=-1) * valid_c
        o_c = jnp.einsum('bqgrn,bngd->bqgrd', p_c.astype(v_cmp.dtype), v_cmp)
        imp = jnp.einsum('bqgrn,nm->bqgm', p_c, cmp_to_slc)
        blk_t = t // SLC_BLOCK
        causal_blk = (j_slc[None, :] <= blk_t[:, None])[None, :, None, :]
        forced = ((j_slc[None, :] == 0) | (j_slc[None, :] == blk_t[:, None]))[None, :, None, :]
        imp = jnp.where(forced, jnp.inf, jnp.where(causal_blk, imp, -jnp.inf))
        _, sel = lax.top_k(imp, top)
        k_sel = ks_blk[b_ix, g_ix, sel].reshape(bn, Q_BLOCK, G, top * SLC_BLOCK, Dh)
        v_sel = vs_blk[b_ix, g_ix, sel].reshape(bn, Q_BLOCK, G, top * SLC_BLOCK, Dh)
        kpos = (sel[..., None] * SLC_BLOCK + jnp.arange(SLC_BLOCK)).reshape(bn, Q_BLOCK, G, top * SLC_BLOCK)
        mask_s = (kpos <= t[None, :, None, None])[:, :, :, None, :]
        s_s = jnp.einsum('bqgrd,bqgnd->bqgrn', qb, k_sel).astype(jnp.float32) * scale
        p_s = jax.nn.softmax(jnp.where(mask_s, s_s, -jnp.inf), axis=-1)
        o_s = jnp.einsum('bqgrn,bqgnd->bqgrd', p_s.astype(v_sel.dtype), v_sel)
        k_w = lax.dynamic_slice_in_dim(kw_pad, s0, WINDOW + Q_BLOCK, axis=1)
        v_w = lax.dynamic_slice_in_dim(vw_pad, s0, WINDOW + Q_BLOCK, axis=1)
        wpos = s0 - WINDOW + jnp.arange(WINDOW + Q_BLOCK)
        mask_w = ((wpos[None, :] <= t[:, None]) & (wpos[None, :] > t[:, None] - WINDOW)
                  & (wpos[None, :] >= 0))[None, :, None, None, :]
        s_w = jnp.einsum('bqgrd,bngd->bqgrn', qb, k_w).astype(jnp.float32) * scale
        p_w = jax.nn.softmax(jnp.where(mask_w, s_w, -jnp.inf), axis=-1)
        o_w = jnp.einsum('bqgrn,bngd->bqgrd', p_w.astype(v_w.dtype), v_w)
        gc = jax.nn.sigmoid(gb)
        return gc[..., 0:1] * o_c + gc[..., 1:2] * o_s + gc[..., 2:3] * o_w

    out = lax.map(block, (q_b, g_b, starts))
    return jnp.moveaxis(out, 0, 1).reshape(bn, s, NSA_HEADS * Dh)


def _s5(u, a_re, a_im, b_re, b_im, c_re, c_im, d, log_dt, glu_w, glu_b):
    f32 = jnp.float32
    bn, s = u.shape[:2]
    uf = u.astype(f32).reshape(bn, s, S5_GROUPS, S5_GROUP_CH)
    A = lax.complex(a_re.astype(f32), a_im.astype(f32))
    dt = jnp.exp(log_dt.astype(f32))[:, None]
    A_bar = jnp.exp(A * dt)
    Bm = lax.complex(b_re.astype(f32), b_im.astype(f32))
    B_bar = ((A_bar - 1.0) / A)[..., None] * Bm
    Bu = jnp.einsum('bsgc,gpc->bsgp', uf.astype(jnp.complex64), B_bar)
    A_seq = jnp.broadcast_to(A_bar, Bu.shape)

    def combine(e1, e2):
        a1, b1 = e1
        a2, b2 = e2
        return a2 * a1, a2 * b1 + b2

    _, state = lax.associative_scan(combine, (A_seq, Bu), axis=1)
    Cm = lax.complex(c_re.astype(f32), c_im.astype(f32))
    y = jnp.einsum('bsgp,gcp->bsgc', state, Cm).real + d.astype(f32) * uf
    y = jax.nn.gelu(y.reshape(bn, s, BRANCH_W))
    y = y * jax.nn.sigmoid(y @ glu_w.astype(f32) + glu_b.astype(f32))
    return y.astype(u.dtype)


def _token_shift(t, mu):
    prev = jnp.pad(t, ((0, 0), (1, 0), (0, 0)))[:, :-1]
    return t + (prev - t) * mu


def _rwkv7(mix, w0, w2, a0, a2, k_k, k_a, r_k, ln_w, ln_b):
    f32 = jnp.float32
    out_dtype = mix.dtype
    bn, s = mix.shape[:2]
    H, N = RWKV_HEADS, HEAD_DIM
    r, k, v, wl, al = _split(mix.astype(f32), (BRANCH_W, BRANCH_W, BRANCH_W, DECAY_LORA, AAA_LORA))
    w = -jax.nn.softplus(-(w0.astype(f32) + jnp.tanh(wl) @ w2.astype(f32))) - 0.5
    decay = jnp.exp(-jnp.exp(w))
    a = jax.nn.sigmoid(a0.astype(f32) + al @ a2.astype(f32))
    kk = (k * k_k.astype(f32)).reshape(bn, s, H, N)
    kk = kk / jnp.maximum(jnp.sqrt(jnp.sum(kk * kk, axis=-1, keepdims=True)), 1e-12)
    k = k * (1.0 + (a - 1.0) * k_a.astype(f32))
    heads = lambda t: t.reshape(bn, s, H, N)
    r_h, k_h, v_h, w_h, a_h = heads(r), heads(k), heads(v), heads(decay), heads(a)
    aa = -kk
    bb = kk * a_h

    def step(state, inp):
        r_t, w_t, k_t, v_t, a_t, b_t = inp
        sa = jnp.einsum('bhvk,bhk->bhv', state, a_t)
        state = (state * w_t[:, :, None, :] + sa[..., None] * b_t[:, :, None, :]
                 + v_t[..., None] * k_t[:, :, None, :])
        return state, jnp.einsum('bhvk,bhk->bhv', state, r_t)

    xs = tuple(jnp.moveaxis(t, 1, 0) for t in (r_h, w_h, k_h, v_h, aa, bb))
    _, y = lax.scan(step, jnp.zeros((bn, H, N, N), f32), xs)
    y = jnp.moveaxis(y, 0, 1)
    mean = jnp.mean(y, axis=-1, keepdims=True)
    var = jnp.mean(jnp.square(y - mean), axis=-1, keepdims=True)
    y = ((y - mean) * lax.rsqrt(var + RWKV_LN_EPS)).reshape(bn, s, BRANCH_W)
    y = y * ln_w.astype(f32) + ln_b.astype(f32)
    bonus = jnp.sum(r_h * k_h * r_k.astype(f32).reshape(H, N), axis=-1, keepdims=True) * v_h
    return (y + bonus.reshape(bn, s, BRANCH_W)).astype(out_dtype)


def setup_inputs(seed: int = 0) -> dict:
    key = jax.random.key(seed)
    keys = iter(jax.random.split(key, 40))
    f32 = jnp.float32
    L, D, W = DEPTH, D_MODEL, BRANCH_W
    G, P, C = S5_GROUPS, S5_STATE, S5_GROUP_CH
    nrm = lambda shape, std: std * jax.random.normal(next(keys), shape, f32)
    unif = lambda shape, lo, hi: jax.random.uniform(next(keys), shape, f32, lo, hi)
    return {
        "x": nrm((BATCH, SEQ, D), 1.0),
        "c": nrm((BATCH, D), 1.0),
        "norm_w": 1.0 + nrm((L, D), 0.02),
        "mod_w": nrm((L, D, 3 * D), 0.5 * D ** -0.5),
        "mod_b": nrm((L, 3 * D), 0.01),
        "w_in": nrm((L, D, IN_W), D ** -0.5),
        "cmp_pos_k": nrm((L, CMP_BLOCK, HEAD_DIM), 0.1),
        "cmp_pos_v": nrm((L, CMP_BLOCK, HEAD_DIM), 0.1),
        "cmp_w1_k": nrm((L, CMP_BLOCK * HEAD_DIM, CMP_HIDDEN), (CMP_BLOCK * HEAD_DIM) ** -0.5),
        "cmp_w2_k": nrm((L, CMP_HIDDEN, HEAD_DIM), CMP_HIDDEN ** -0.5),
        "cmp_w1_v": nrm((L, CMP_BLOCK * HEAD_DIM, CMP_HIDDEN), (CMP_BLOCK * HEAD_DIM) ** -0.5),
        "cmp_w2_v": nrm((L, CMP_HIDDEN, HEAD_DIM), CMP_HIDDEN ** -0.5),
        "s5_a_re": -0.5 + nrm((L, G, P), 0.01),
        "s5_a_im": math.pi * jnp.broadcast_to(jnp.arange(P, dtype=f32), (L, G, P)) + nrm((L, G, P), 0.01),
        "s5_b_re": nrm((L, G, P, C), (2.0 * C) ** -0.5),
        "s5_b_im": nrm((L, G, P, C), (2.0 * C) ** -0.5),
        "s5_c_re": nrm((L, G, C, P), (2.0 * P) ** -0.5),
        "s5_c_im": nrm((L, G, C, P), (2.0 * P) ** -0.5),
        "s5_d": nrm((L, G, C), 0.5),
        "s5_log_dt": unif((L, G), math.log(1e-3), math.log(1e-1)),
        "s5_glu_w": nrm((L, W, W), W ** -0.5),
        "s5_glu_b": nrm((L, W), 0.01),
        "rwkv_mu": unif((L, RWKV_MIX_W), 0.0, 1.0),
        "rwkv_w0": unif((L, W), -6.0, -1.0),
        "rwkv_w2": nrm((L, DECAY_LORA, W), 0.5 * DECAY_LORA ** -0.5),
        "rwkv_a0": nrm((L, W), 0.1),
        "rwkv_a2": nrm((L, AAA_LORA, W), 0.5 * AAA_LORA ** -0.5),
        "rwkv_k_k": 0.85 + nrm((L, W), 0.02),
        "rwkv_k_a": 1.0 + nrm((L, W), 0.02),
        "rwkv_r_k": nrm((L, W), 0.1),
        "rwkv_ln_w": 1.0 + nrm((L, W), 0.02),
        "rwkv_ln_b": nrm((L, W), 0.01),
        "w_up": nrm((L, N_BRANCH, W, D), W ** -0.5),
        "w_out": nrm((L, D, D), D ** -0.5),
        "final_norm_w": 1.0 + nrm((D,), 0.02),
    }


def reference(x, c, norm_w, mod_w, mod_b, w_in, cmp_pos_k, cmp_pos_v, cmp_w1_k, cmp_w2_k,
              cmp_w1_v, cmp_w2_v, s5_a_re, s5_a_im, s5_b_re, s5_b_im, s5_c_re, s5_c_im, s5_d,
              s5_log_dt, s5_glu_w, s5_glu_b, rwkv_mu, rwkv_w0, rwkv_w2, rwkv_a0, rwkv_a2,
              rwkv_k_k, rwkv_k_a, rwkv_r_k, rwkv_ln_w, rwkv_ln_b, w_up, w_out, final_norm_w):
    bn, s, d = x.shape
    G, Dh = NSA_KV_GROUPS, HEAD_DIM
    cos, sin = _rope_tables(s)
    cond = jax.nn.silu(c)
    for l in range(DEPTH):
        shift, scale, gate = jnp.split(cond @ mod_w[l] + mod_b[l], 3, axis=-1)
        h = _rmsnorm(x, norm_w[l]) * (1.0 + scale[:, None, :]) + shift[:, None, :]
        proj = h @ w_in[l]
        q, kv, nsa_g, nsa_gate, s5_u, s5_gate, rwkv_mix, rwkv_gate, merge_g = _split(proj, IN_SPLITS)
        q = _rope(q.reshape(bn, s, NSA_HEADS, Dh), cos, sin)
        kc, vc, ks, vs, kw, vw = [t.reshape(bn, s, G, Dh) for t in jnp.split(kv, 6, axis=-1)]
        kc, ks, kw = _rope(kc, cos, sin), _rope(ks, cos, sin), _rope(kw, cos, sin)
        o_nsa = _nsa(q, kc, vc, ks, vs, kw, vw, nsa_g, cmp_pos_k[l], cmp_pos_v[l],
                     cmp_w1_k[l], cmp_w2_k[l], cmp_w1_v[l], cmp_w2_v[l])
        o_s5 = _s5(s5_u, s5_a_re[l], s5_a_im[l], s5_b_re[l], s5_b_im[l], s5_c_re[l], s5_c_im[l],
                   s5_d[l], s5_log_dt[l], s5_glu_w[l], s5_glu_b[l])
        o_rwkv = _rwkv7(_token_shift(rwkv_mix, rwkv_mu[l]), rwkv_w0[l], rwkv_w2[l], rwkv_a0[l],
                        rwkv_a2[l], rwkv_k_k[l], rwkv_k_a[l], rwkv_r_k[l], rwkv_ln_w[l], rwkv_ln_b[l])
        branches = jnp.stack([o_nsa * jax.nn.silu(nsa_gate), o_s5 * jax.nn.silu(s5_gate),
                              o_rwkv * jax.nn.silu(rwkv_gate)], axis=2)
        up = jnp.einsum('bsnw,nwd->bsnd', branches, w_up[l])
        merged = jnp.sum(jax.nn.sigmoid(merge_g.reshape(bn, s, N_BRANCH, d)) * up, axis=2)
        x = x + gate[:, None, :] * (merged @ w_out[l])
    return _rmsnorm(x, final_norm_w)
```

```python
import functools
import math

import numpy as np
import jax
import jax.numpy as jnp
from jax import lax
from jax.experimental import pallas as pl
from jax.experimental.pallas import tpu as pltpu

F32 = jnp.float32
BF16 = jnp.bfloat16
HIGHEST = lax.Precision.HIGHEST

HEAD_DIM = 64
NSA_HEADS = 8
NSA_GROUPS = 2
NSA_HPG = NSA_HEADS // NSA_GROUPS
CMP_BLOCK = 32
CMP_STRIDE = 16
SLC_BLOCK = 64
SLC_TOP = 16
WINDOW = 512
Q_BLOCK = 128
S5_GROUPS = 32
S5_GROUP_CH = 16
S5_STATE = 64
RWKV_HEADS = 8
BRANCH_W = 512
LORA_W = 64
RWKV_MIX_W = 3 * BRANCH_W + 2 * LORA_W
ROPE_THETA = 10000.0
NORM_EPS = 1e-6
RWKV_LN_EPS = 64e-5
NEG_BIG = -1e30
MASK_BIG = 2.0 ** 100
M_INIT = -3.0e38

VMEM_LIMIT = 56 * 1024 * 1024
LANES = 128
SUBLANES = 8

C_Q = 0
C_KC, C_KS, C_KW, C_VC, C_VS, C_VW = 512, 640, 768, 896, 1024, 1152
C_G = 1280
C_NG = 1408
C_SU = 1920
C_SG = 2432
C_MIX = 2944
C_RG = C_MIX + RWKV_MIX_W
C_MG = C_RG + BRANCH_W
IN_PACKED = C_MG + 3 * 1024


def _silu(z):
    return z * jax.nn.sigmoid(z)


def _bdot(a, b):
    return jnp.dot(a.astype(BF16), b.astype(BF16), preferred_element_type=F32)


def _dot_nt(a, b, precision=None):
    return lax.dot_general(a, b, (((1,), (1,)), ((), ())), precision=precision,
                           preferred_element_type=F32)


def _hdot(a, b):
    return jnp.dot(a, b, precision=HIGHEST, preferred_element_type=F32)


def _mod_kernel(c_ref, w_ref, b_ref, o_ref):
    cond = _silu(c_ref[...])
    o_ref[0] = _bdot(cond, w_ref[0]) + b_ref[0]


def _modulation(c, mod_w, mod_b):
    depth, d, d3 = mod_w.shape
    bn = c.shape[0]
    nj = d3 // d
    return pl.pallas_call(
        _mod_kernel,
        grid=(depth, nj),
        in_specs=[pl.BlockSpec((bn, d), lambda l, j: (0, 0)),
                  pl.BlockSpec((1, d, d), lambda l, j: (l, 0, j)),
                  pl.BlockSpec((1, 1, d), lambda l, j: (l, 0, j))],
        out_specs=pl.BlockSpec((1, bn, d), lambda l, j: (l, 0, j)),
        out_shape=jax.ShapeDtypeStruct((depth, bn, d3), F32),
        name="adaln_mod",
    )(c, mod_w, mod_b.reshape(depth, 1, d3))


_INPROJ_OUTS = (
    ("q", C_Q, 512, BF16), ("kc", C_KC, 128, F32), ("ks", C_KS, 128, BF16), ("kw", C_KW, 128, BF16),
    ("vc", C_VC, 128, F32), ("vs", C_VS, 128, BF16), ("vw", C_VW, 128, BF16), ("g", C_G, 128, F32),
    ("ng", C_NG, 512, F32), ("su", C_SU, 512, F32), ("sg", C_SG, 512, F32),
    ("mix", C_MIX, RWKV_MIX_W, F32), ("rg", C_RG, 512, F32), ("mg", C_MG, 3072, F32))
_ROPED = ("q", "kc", "ks", "kw")


def _inproj_kernel(x_ref, nw_ref, sc_ref, sh_ref, cos_ref, sin_ref, w_ref, *out_refs):
    x = x_ref[0]
    tm = x.shape[0]
    ms = jnp.mean(x * x, axis=-1, keepdims=True)
    h = x * lax.rsqrt(ms + NORM_EPS) * nw_ref[...]
    h = h * (1.0 + sc_ref[0]) + sh_ref[0]
    hb = h.astype(BF16)
    cos = cos_ref[...]
    sin = sin_ref[...]
    lane = lax.broadcasted_iota(jnp.int32, (tm, LANES), 1)
    first_half = (lane % HEAD_DIM) < (HEAD_DIM // 2)

    def rope(t):
        partner = jnp.where(first_half, pltpu.roll(t, 96, 1), pltpu.roll(t, 32, 1))
        return t * cos + partner * sin

    for (name, c0, width, dt), o_ref in zip(_INPROJ_OUTS, out_refs):
        step = min(width, 512)
        for j0 in range(0, width, step):
            w = min(step, width - j0)
            y = jnp.dot(hb, w_ref[:, c0 + j0:c0 + j0 + w], preferred_element_type=F32)
            if name in _ROPED:
                y = jnp.concatenate([rope(y[:, k:k + LANES]) for k in range(0, w, LANES)], axis=1)
            if name == "q":
                y = y * (HEAD_DIM ** -0.5)
            o_ref[0, :, j0:j0 + w] = y.astype(dt)


def _inproj(x, nw, scale, shift, cos_t, sin_t, w_packed, tm=256):
    bn, s, d = x.shape
    out_shape = [jax.ShapeDtypeStruct((bn, s, w), dt) for (_, _, w, dt) in _INPROJ_OUTS]
    out_specs = [pl.BlockSpec((1, tm, w), lambda b, i: (b, i, 0)) for (_, _, w, _) in _INPROJ_OUTS]
    outs = pl.pallas_call(
        _inproj_kernel,
        grid=(bn, s // tm),
        in_specs=[pl.BlockSpec((1, tm, d), lambda b, i: (b, i, 0)),
                  pl.BlockSpec((1, d), lambda b, i: (0, 0)),
                  pl.BlockSpec((1, 1, d), lambda b, i: (b, 0, 0)),
                  pl.BlockSpec((1, 1, d), lambda b, i: (b, 0, 0)),
                  pl.BlockSpec((tm, LANES), lambda b, i: (i, 0)),
                  pl.BlockSpec((tm, LANES), lambda b, i: (i, 0)),
                  pl.BlockSpec((d, IN_PACKED), lambda b, i: (0, 0), pipeline_mode=pl.Buffered(1))],
        out_specs=out_specs,
        out_shape=out_shape,
        compiler_params=pltpu.CompilerParams(dimension_semantics=("parallel", "parallel"),
                                             vmem_limit_bytes=VMEM_LIMIT),
        name="inproj",
    )(x, nw.reshape(1, d), scale.reshape(bn, 1, d), shift.reshape(bn, 1, d), cos_t, sin_t, w_packed)
    return dict(zip([o[0] for o in _INPROJ_OUTS], outs))


def _pack_w_in(w_in):
    d = w_in.shape[0]
    sizes = (512, 768, 24, 512, 512, 512, RWKV_MIX_W, 512, 3072)
    offs = np.concatenate([[0], np.cumsum(sizes)])
    q, kv, g, ng, su, sg, mix, rg, mg = [w_in[:, offs[i]:offs[i + 1]] for i in range(len(sizes))]
    kc, vc, ks, vs, kw, vw = [kv[:, i * 128:(i + 1) * 128] for i in range(6)]
    gpad = jnp.pad(g, ((0, 0), (0, 128 - 24)))
    return jnp.concatenate([q, kc, ks, kw, vc, vs, vw, gpad, ng, su, sg, mix, rg, mg], axis=1).astype(BF16)


def _rope_tables(s):
    half = HEAD_DIM // 2
    inv = jnp.exp(-math.log(ROPE_THETA) * jnp.arange(half, dtype=F32) / half)
    ang = jnp.arange(s, dtype=F32)[:, None] * inv[None, :]
    cos, sin = jnp.cos(ang), jnp.sin(ang)
    cos_t = jnp.tile(cos, (1, LANES // half))
    sin_t = jnp.tile(jnp.concatenate([-sin, sin], axis=1), (1, LANES // HEAD_DIM))
    return cos_t, sin_t


def _compress_kernel(kc_ref, vc_ref, pk_ref, pv_ref, wkt_ref, wkb_ref, wvt_ref, wvb_ref,
                     w2k_ref, w2v_ref, ko_ref, vo_ref):
    def one(x_ref, p_ref, wt_ref, wb_ref, w2_ref, o_ref):
        x = x_ref[0]
        n = x.shape[0]
        top = _bdot(x + p_ref[0:1, :], wt_ref[...])
        bot = _bdot(x + p_ref[1:2, :], wb_ref[...])
        hid = top + pltpu.roll(bot, n - 1, 0)
        act = _silu(hid)
        hw = act.shape[1] // NSA_GROUPS
        for g in range(NSA_GROUPS):
            o_ref[0, g] = _bdot(act[:, g * hw:(g + 1) * hw], w2_ref[...])

    one(kc_ref, pk_ref, wkt_ref, wkb_ref, w2k_ref, ko_ref)
    one(vc_ref, pv_ref, wvt_ref, wvb_ref, w2v_ref, vo_ref)


def _compress_weights(pos, w1, w2):
    hid = w1.shape[1]
    half = CMP_BLOCK // 2
    w1r = w1.reshape(2, half, HEAD_DIM, hid)
    eye = jnp.eye(NSA_GROUPS, dtype=w1.dtype)
    wd = jnp.einsum("tldj,gh->tlgdhj", w1r, eye).reshape(2, half * NSA_GROUPS * HEAD_DIM,
                                                         NSA_GROUPS * hid)
    pr = pos.reshape(2, half, 1, HEAD_DIM)
    pt = jnp.broadcast_to(pr, (2, half, NSA_GROUPS, HEAD_DIM)).reshape(2, -1)
    return pt, wd[0].astype(BF16), wd[1].astype(BF16), w2.astype(BF16)


def _compress(kc, vc, pos_k, pos_v, w1k, w2k, w1v, w2v):
    bn, s, kvw = kc.shape
    n16 = s // CMP_STRIDE
    row_w = CMP_STRIDE * kvw
    pk, wkt, wkb, w2kb = _compress_weights(pos_k, w1k, w2k)
    pv, wvt, wvb, w2vb = _compress_weights(pos_v, w1v, w2v)
    hid2 = wkt.shape[1]
    full = lambda shape: pl.BlockSpec(shape, lambda b: tuple(0 for _ in shape))
    out = jax.ShapeDtypeStruct((bn, NSA_GROUPS, n16, HEAD_DIM), F32)
    return pl.pallas_call(
        _compress_kernel,
        grid=(bn,),
        in_specs=[pl.BlockSpec((1, n16, row_w), lambda b: (b, 0, 0)),
                  pl.BlockSpec((1, n16, row_w), lambda b: (b, 0, 0)),
                  full((2, row_w)), full((2, row_w)),
                  full((row_w, hid2)), full((row_w, hid2)), full((row_w, hid2)), full((row_w, hid2)),
                  full(w2kb.shape), full(w2vb.shape)],
        out_specs=[pl.BlockSpec((1, NSA_GROUPS, n16, HEAD_DIM), lambda b: (b, 0, 0, 0))] * 2,
        out_shape=[out, out],
        compiler_params=pltpu.CompilerParams(dimension_semantics=("parallel",),
                                             vmem_limit_bytes=VMEM_LIMIT),
        name="nsa_compress",
    )(kc.reshape(bn, n16, row_w), vc.reshape(bn, n16, row_w), pk, pv, wkt, wkb, wvt, wvb, w2kb, w2vb)


SEL_TILE = 256
WIN_TILE = 128
MASK_W = 64


def _nsa_kernel(q_ref, g_ref, kcmp_ref, vcmp_ref, ks_ref, vs_ref, kw_ref, vw_ref, c2s_ref,
                o_ref, kaug_ref, *, top):
    qb = pl.program_id(1)
    s0 = qb * Q_BLOCK
    s = ks_ref.shape[1]
    nc = kcmp_ref.shape[2]
    rows = NSA_HPG * Q_BLOCK

    @pl.when(qb == 0)
    def _():
        blk = lax.broadcasted_iota(jnp.int32, (s, MASK_W), 0) // SLC_BLOCK
        col = lax.broadcasted_iota(jnp.int32, (s, MASK_W), 1)
        onehot = jnp.where(blk == col, 1.0, 0.0).astype(BF16)
        for g in range(NSA_GROUPS):
            kaug_ref[g, :, 0:HEAD_DIM] = ks_ref[0, :, g * HEAD_DIM:(g + 1) * HEAD_DIM]
            kaug_ref[g, :, HEAD_DIM:HEAD_DIM + MASK_W] = onehot

    q_all = q_ref[0]
    gates = jax.nn.sigmoid(g_ref[0])
    t_rows = s0 + lax.broadcasted_iota(jnp.int32, (rows, 1), 0) % Q_BLOCK

    for g in range(NSA_GROUPS):
        qg = jnp.concatenate([q_all[:, (g * NSA_HPG + r) * HEAD_DIM:(g * NSA_HPG + r + 1) * HEAD_DIM]
                              for r in range(NSA_HPG)], axis=0)
        kc = kcmp_ref[0, g].astype(BF16)
        vc = vcmp_ref[0, g].astype(BF16)

        sc = _dot_nt(qg, kc)
        n_col = lax.broadcasted_iota(jnp.int32, (rows, nc), 1)
        valid = (CMP_STRIDE * n_col + CMP_BLOCK - 1) <= t_rows
        scm = jnp.where(valid, sc, NEG_BIG)
        e = jnp.exp(scm - jnp.max(scm, axis=1, keepdims=True))
        pc = jnp.where(valid, e / jnp.sum(e, axis=1, keepdims=True), 0.0)
        o_c = _bdot(pc, vc)

        st = _dot_nt(kc, qg)
        n_row = lax.broadcasted_iota(jnp.int32, (nc, rows), 0)
        t_col = s0 + lax.broadcasted_iota(jnp.int32, (nc, rows), 1) % Q_BLOCK
        valid_t = (CMP_STRIDE * n_row + CMP_BLOCK - 1) <= t_col
        stm = jnp.where(valid_t, st, NEG_BIG)
        et = jnp.exp(stm - jnp.max(stm, axis=0, keepdims=True))
        pt = jnp.where(valid_t, et / jnp.sum(et, axis=0, keepdims=True), 0.0)
        psum = pt[:, 0:Q_BLOCK]
        for r in range(1, NSA_HPG):
            psum = psum + pt[:, r * Q_BLOCK:(r + 1) * Q_BLOCK]
        imp = _hdot(c2s_ref[...], psum)
        j_idx = lax.broadcasted_iota(jnp.int32, (MASK_W, Q_BLOCK), 0)
        blk_t = (s0 + lax.broadcasted_iota(jnp.int32, (MASK_W, Q_BLOCK), 1)) // SLC_BLOCK
        causal = j_idx <= blk_t
        forced = (j_idx == 0) | (j_idx == blk_t)
        imp = jnp.where(forced, jnp.inf, jnp.where(causal, imp, -jnp.inf))
        cnt = jnp.zeros((MASK_W, Q_BLOCK), jnp.int32)
        for i in range(s // SLC_BLOCK):
            row = imp[i:i + 1, :]
            tie = jnp.where(j_idx > i, 1, 0)
            cnt = cnt + jnp.where(row > imp, 1, jnp.where(row == imp, tie, 0))
        sel_t = jnp.where(causal, jnp.where(cnt < top, 0.0, -MASK_BIG), -MASK_BIG)
        mterm = sel_t.T.astype(BF16)
        q_aug = jnp.concatenate([qg, jnp.concatenate([mterm] * NSA_HPG, axis=0)], axis=1)

        def sel_body(i, carry):
            m, l, acc = carry
            k0 = pl.multiple_of(i * SEL_TILE, SEL_TILE)
            kt = kaug_ref[g, pl.ds(k0, SEL_TILE), :]
            sc_ = _dot_nt(q_aug, kt)
            kpos = k0 + lax.broadcasted_iota(jnp.int32, (rows, SEL_TILE), 1)
            sc_ = jnp.where(kpos <= t_rows, sc_, -MASK_BIG)
            m_new = jnp.maximum(m, jnp.max(sc_, axis=1, keepdims=True))
            alpha = jnp.exp(m - m_new)
            p = jnp.exp(sc_ - m_new)
            l = alpha * l + jnp.sum(p, axis=1, keepdims=True)
            acc = alpha * acc + jnp.dot(p.astype(BF16), vs_ref[0, pl.ds(k0, SEL_TILE), :],
                                        preferred_element_type=F32)
            return m_new, l, acc

        init = (jnp.full((rows, 1), M_INIT, F32), jnp.zeros((rows, 1), F32),
                jnp.zeros((rows, NSA_GROUPS * HEAD_DIM), F32))
        n_sel = (s0 + Q_BLOCK + SEL_TILE - 1) // SEL_TILE
        _, l_s, acc_s = lax.fori_loop(0, n_sel, sel_body, init)
        o_s = acc_s[:, g * HEAD_DIM:(g + 1) * HEAD_DIM] / l_s

        zeros = jnp.zeros((rows, HEAD_DIM), BF16)
        q_w = jnp.concatenate([qg, zeros] if g == 0 else [zeros, qg], axis=1)

        def win_body(i, carry):
            m, l, acc = carry
            k0 = pl.multiple_of(i * WIN_TILE, WIN_TILE)
            sc_ = _dot_nt(q_w, kw_ref[0, pl.ds(k0, WIN_TILE), :])
            kpos = k0 + lax.broadcasted_iota(jnp.int32, (rows, WIN_TILE), 1)
            ok = (kpos <= t_rows) & (kpos > t_rows - WINDOW)
            sc_ = jnp.where(ok, sc_, -MASK_BIG)
            m_new = jnp.maximum(m, jnp.max(sc_, axis=1, keepdims=True))
            alpha = jnp.exp(m - m_new)
            p = jnp.where(ok, jnp.exp(sc_ - m_new), 0.0)
            l = alpha * l + jnp.sum(p, axis=1, keepdims=True)
            acc = alpha * acc + jnp.dot(p.astype(BF16), vw_ref[0, pl.ds(k0, WIN_TILE), :],
                                        preferred_element_type=F32)
            return m_new, l, acc

        w_lo = jnp.maximum(qb - WINDOW // WIN_TILE, 0)
        _, l_w, acc_w = lax.fori_loop(w_lo, qb + 1, win_body, init)
        o_w = acc_w[:, g * HEAD_DIM:(g + 1) * HEAD_DIM] / l_w

        for r in range(NSA_HPG):
            hh = g * NSA_HPG + r
            rs = slice(r * Q_BLOCK, (r + 1) * Q_BLOCK)
            o = (gates[:, 3 * hh:3 * hh + 1] * o_c[rs] + gates[:, 3 * hh + 1:3 * hh + 2] * o_s[rs]
                 + gates[:, 3 * hh + 2:3 * hh + 3] * o_w[rs])
            o_ref[0, :, hh * HEAD_DIM:(hh + 1) * HEAD_DIM] = o


def _cmp_to_slc_t(s):
    n_cmp = s // CMP_STRIDE
    n_slc = s // SLC_BLOCK
    cs = CMP_STRIDE * np.arange(n_cmp)[:, None]
    ss = SLC_BLOCK * np.arange(n_slc)[None, :]
    overlap = np.clip(np.minimum(cs + CMP_BLOCK, ss + SLC_BLOCK) - np.maximum(cs, ss), 0, None)
    m = np.zeros((MASK_W, n_cmp), np.float32)
    m[:n_slc] = (overlap / CMP_BLOCK).T
    return jnp.asarray(m)


def _nsa(p, k_cmp, v_cmp):
    q = p["q"]
    bn, s, _ = q.shape
    assert s // SLC_BLOCK <= MASK_W and s % SEL_TILE == 0
    top = min(SLC_TOP, s // SLC_BLOCK)
    nc = k_cmp.shape[2]
    kvw = NSA_GROUPS * HEAD_DIM
    per_b = lambda w: pl.BlockSpec((1, s, w), lambda b, i: (b, 0, 0))
    return pl.pallas_call(
        functools.partial(_nsa_kernel, top=top),
        grid=(bn, s // Q_BLOCK),
        in_specs=[pl.BlockSpec((1, Q_BLOCK, NSA_HEADS * HEAD_DIM), lambda b, i: (b, i, 0)),
                  pl.BlockSpec((1, Q_BLOCK, LANES), lambda b, i: (b, i, 0)),
                  pl.BlockSpec((1, NSA_GROUPS, nc, HEAD_DIM), lambda b, i: (b, 0, 0, 0)),
                  pl.BlockSpec((1, NSA_GROUPS, nc, HEAD_DIM), lambda b, i: (b, 0, 0, 0)),
                  per_b(kvw), per_b(kvw), per_b(kvw), per_b(kvw),
                  pl.BlockSpec((MASK_W, nc), lambda b, i: (0, 0))],
        out_specs=pl.BlockSpec((1, Q_BLOCK, NSA_HEADS * HEAD_DIM), lambda b, i: (b, i, 0)),
        out_shape=jax.ShapeDtypeStruct((bn, s, NSA_HEADS * HEAD_DIM), F32),
        scratch_shapes=[pltpu.VMEM((NSA_GROUPS, s, HEAD_DIM + MASK_W), BF16)],
        compiler_params=pltpu.CompilerParams(dimension_semantics=("parallel", "arbitrary"),
                                             vmem_limit_bytes=VMEM_LIMIT),
        name="nsa_attention",
    )(q, p["g"], k_cmp, v_cmp, p["ks"], p["vs"], p["kw"], p["vw"], _cmp_to_slc_t(s))


S5_STRIP = 512


def _gelu_tanh(y):
    return 0.5 * y * (1.0 + jnp.tanh(math.sqrt(2.0 / math.pi) * (y + 0.044715 * (y * y * y))))


def _s5_kernel(u_ref, bre_ref, bim_ref, are_ref, aim_ref, cre_ref, cim_ref, d_ref, gw_ref, gb_ref,
               o_ref, xre_ref, xim_ref, sre_ref, sim_ref):
    tc, bn, width = u_ref.shape
    nstate = are_ref.shape[1]

    @pl.when(pl.program_id(0) == 0)
    def _():
        sre_ref[...] = jnp.zeros_like(sre_ref)
        sim_ref[...] = jnp.zeros_like(sim_ref)

    u = u_ref[...].reshape(tc * bn, width)
    ub = u.astype(BF16)
    xre_ref[...] = jnp.dot(ub, bre_ref[...], preferred_element_type=F32)
    xim_ref[...] = jnp.dot(ub, bim_ref[...], preferred_element_type=F32)

    for c0 in range(0, nstate, S5_STRIP):
        cols = pl.ds(c0, S5_STRIP)
        a_r = jnp.broadcast_to(are_ref[:, c0:c0 + S5_STRIP], (bn, S5_STRIP))
        a_i = jnp.broadcast_to(aim_ref[:, c0:c0 + S5_STRIP], (bn, S5_STRIP))

        def step(t, carry):
            x_r, x_i = carry
            r0 = pl.multiple_of(t * bn, bn)
            n_r = a_r * x_r - a_i * x_i + xre_ref[pl.ds(r0, bn), cols]
            n_i = a_r * x_i + a_i * x_r + xim_ref[pl.ds(r0, bn), cols]
            xre_ref[pl.ds(r0, bn), cols] = n_r
            xim_ref[pl.ds(r0, bn), cols] = n_i
            return n_r, n_i

        x_r, x_i = lax.fori_loop(0, tc, step, (sre_ref[:, cols], sim_ref[:, cols]), unroll=8)
        sre_ref[:, cols] = x_r
        sim_ref[:, cols] = x_i

    y = (jnp.dot(xre_ref[...].astype(BF16), cre_ref[...], preferred_element_type=F32)
         + jnp.dot(xim_ref[...].astype(BF16), cim_ref[...], preferred_element_type=F32)
         + d_ref[...] * u)
    y = _gelu_tanh(y)
    z = _bdot(y, gw_ref[...]) + gb_ref[...]
    o_ref[...] = (y * jax.nn.sigmoid(z)).reshape(tc, bn, width)


def _s5_params(a_re, a_im, b_re, b_im, c_re, c_im, log_dt):
    g, p = a_re.shape
    a = lax.complex(a_re.astype(F32), a_im.astype(F32))
    dt = jnp.exp(log_dt.astype(F32))[:, None]
    a_bar = jnp.exp(a * dt)
    b_bar = ((a_bar - 1.0) / a)[..., None] * lax.complex(b_re.astype(F32), b_im.astype(F32))
    eye = jnp.eye(g, dtype=F32)
    c = b_re.shape[-1]
    pack_b = lambda m: jnp.einsum("gpc,gh->gchp", m, eye).reshape(g * c, g * p).astype(BF16)
    pack_c = lambda m: jnp.einsum("gcp,gh->gphc", m, eye).reshape(g * p, g * c).astype(BF16)
    return (pack_b(jnp.real(b_bar)), pack_b(jnp.imag(b_bar)),
            jnp.real(a_bar).reshape(1, g * p), jnp.imag(a_bar).reshape(1, g * p),
            pack_c(c_re.astype(F32)), pack_c(-c_im.astype(F32)))


def _s5(u_t, params, d, glu_w, glu_b, tc=64):
    s, bn, width = u_t.shape
    assert bn == SUBLANES
    bre, bim, are, aim, cre, cim = params
    nstate = are.shape[1]
    full = lambda a: pl.BlockSpec(a.shape, lambda i: tuple(0 for _ in a.shape))
    d2, gb2, gwb = d.reshape(1, width), glu_b.reshape(1, width), glu_w.astype(BF16)
    return pl.pallas_call(
        _s5_kernel,
        grid=(s // tc,),
        in_specs=[pl.BlockSpec((tc, bn, width), lambda i: (i, 0, 0)),
                  full(bre), full(bim), full(are), full(aim), full(cre), full(cim),
                  full(d2), full(gwb), full(gb2)],
        out_specs=pl.BlockSpec((tc, bn, width), lambda i: (i, 0, 0)),
        out_shape=jax.ShapeDtypeStruct((s, bn, width), F32),
        scratch_shapes=[pltpu.VMEM((tc * bn, nstate), F32), pltpu.VMEM((tc * bn, nstate), F32),
                        pltpu.VMEM((bn, nstate), F32), pltpu.VMEM((bn, nstate), F32)],
        compiler_params=pltpu.CompilerParams(dimension_semantics=("arbitrary",),
                                             vmem_limit_bytes=VMEM_LIMIT),
        name="s5_scan",
    )(u_t, bre, bim, are, aim, cre, cim, d2, gwb, gb2)


RWKV_CHUNK = 64


def _rwkv_kernel(mix_ref, mu_ref, w0_ref, w2_ref, a0_ref, a2_ref, kk_ref, ka_ref, rk_ref,
                 lnw_ref, lnb_ref, seg_ref, o_ref, state_ref, carry_ref, y_ref):
    n = HEAD_DIM
    chunk = mix_ref.shape[1]
    w = BRANCH_W

    @pl.when(pl.program_id(1) == 0)
    def _():
        state_ref[...] = jnp.zeros_like(state_ref)
        carry_ref[...] = jnp.zeros_like(carry_ref)

    mix = mix_ref[0]
    row_m = lax.broadcasted_iota(jnp.int32, mix.shape, 0)
    prev = jnp.where(row_m == 0, carry_ref[0:1, :], pltpu.roll(mix, 1, 0))
    carry_ref[0:1, :] = mix[chunk - 1:chunk, :]
    xs = mix + (prev - mix) * mu_ref[...]
    r, k, v = xs[:, 0:w], xs[:, w:2 * w], xs[:, 2 * w:3 * w]
    wl, al = xs[:, 3 * w:3 * w + LORA_W], xs[:, 3 * w + LORA_W:3 * w + 2 * LORA_W]

    lw = w0_ref[...] + _bdot(jnp.tanh(wl), w2_ref[...])
    z = -lw
    softplus = jnp.maximum(z, 0.0) + jnp.log(1.0 + jnp.exp(-jnp.abs(z)))
    ld = -jnp.exp(-softplus - 0.5)
    a = jax.nn.sigmoid(a0_ref[...] + _bdot(al, a2_ref[...]))
    seg = seg_ref[...]
    kk = k * kk_ref[...]
    kk = kk / jnp.maximum(jnp.sqrt(_hdot(kk * kk, seg)), 1e-12)
    k2 = k * (1.0 + (a - 1.0) * ka_ref[...])
    aa = -kk
    bb = kk * a

    row = lax.broadcasted_iota(jnp.int32, (chunk, w), 0)
    cum = ld
    sh = 1
    while sh < chunk:
        cum = cum + jnp.where(row >= sh, pltpu.roll(cum, sh, 0), 0.0)
        sh *= 2
    tot = cum[chunk - 1:chunk, :]
    at = aa * jnp.exp(cum - ld)
    rt = r * jnp.exp(cum)
    e_neg = jnp.exp(-cum)
    bt, kt = bb * e_neg, k2 * e_neg
    e_rem = jnp.exp(tot - cum)
    bh, kh = bb * e_rem, k2 * e_rem
    p_tot = jnp.exp(tot)

    ti = lax.broadcasted_iota(jnp.int32, (chunk, chunk), 0)
    si = lax.broadcasted_iota(jnp.int32, (chunk, chunk), 1)
    strict, incl = si < ti, si <= ti
    n_double = int(math.log2(chunk)) - 1

    for h in range(RWKV_HEADS):
        sl = slice(h * n, (h + 1) * n)
        s0 = state_ref[h]
        ar = jnp.concatenate([at[:, sl], rt[:, sl]], axis=0)
        bk = jnp.concatenate([bt[:, sl], kt[:, sl]], axis=0)
        amat = _dot_nt(ar, bk, HIGHEST)
        a_ab = jnp.where(strict, amat[:chunk, :chunk], 0.0)
        a_ak = jnp.where(strict, amat[:chunk, chunk:], 0.0)
        a_rb = jnp.where(incl, amat[chunk:, :chunk], 0.0)
        a_rk = jnp.where(incl, amat[chunk:, chunk:], 0.0)
        ah = _dot_nt(ar, s0, HIGHEST)
        vh = v[:, sl]
        u = ah[:chunk] + _hdot(a_ak, vh)
        pw = a_ab
        u = u + _hdot(pw, u)
        for _ in range(n_double):
            pw = _hdot(pw, pw)
            u = u + _hdot(pw, u)
        y_ref[:, sl] = ah[chunk:] + _hdot(a_rb, u) + _hdot(a_rk, vh)
        state_ref[h] = s0 * p_tot[:, sl] + _hdot(u.T, bh[:, sl]) + _hdot(vh.T, kh[:, sl])

    y = y_ref[...]
    inv_n = 1.0 / n
    mean = _hdot(y, seg) * inv_n
    dev = y - mean
    var = _hdot(dev * dev, seg) * inv_n
    yn = dev * lax.rsqrt(var + RWKV_LN_EPS) * lnw_ref[...] + lnb_ref[...]
    bonus = _hdot(r * k2 * rk_ref[...], seg) * v
    o_ref[0] = yn + bonus


def _rwkv(mix, mu, w0, w2, a0, a2, k_k, k_a, r_k, ln_w, ln_b):
    bn, s, mw = mix.shape
    w = BRANCH_W
    chunk = RWKV_CHUNK
    head = np.arange(w) // HEAD_DIM
    seg = jnp.asarray((head[:, None] == head[None, :]).astype(np.float32))
    row = lambda t: t.reshape(1, -1).astype(F32)
    args = (row(mu), row(w0), w2.astype(BF16), row(a0), a2.astype(BF16), row(k_k), row(k_a), row(r_k),
            row(ln_w), row(ln_b), seg)
    full = lambda a: pl.BlockSpec(a.shape, lambda b, i: tuple(0 for _ in a.shape))
    return pl.pallas_call(
        _rwkv_kernel,
        grid=(bn, s // chunk),
        in_specs=[pl.BlockSpec((1, chunk, mw), lambda b, i: (b, i, 0))] + [full(t) for t in args],
        out_specs=pl.BlockSpec((1, chunk, w), lambda b, i: (b, i, 0)),
        out_shape=jax.ShapeDtypeStruct((bn, s, w), F32),
        scratch_shapes=[pltpu.VMEM((RWKV_HEADS, HEAD_DIM, HEAD_DIM), F32),
                        pltpu.VMEM((SUBLANES, mw), F32),
                        pltpu.VMEM((chunk, w), F32)],
        compiler_params=pltpu.CompilerParams(dimension_semantics=("parallel", "arbitrary"),
                                             vmem_limit_bytes=VMEM_LIMIT),
        name="rwkv7",
    )(mix, *args)


def _merge_kernel(on_ref, os_ref, or_ref, gn_ref, gs_ref, gr_ref, mg_ref, x_ref, gate_ref,
                  wup_ref, wout_ref, fnw_ref, o_ref, *, final):
    d = x_ref.shape[2]
    merged = None
    for i, (b_ref, g_ref) in enumerate(((on_ref, gn_ref), (os_ref, gs_ref), (or_ref, gr_ref))):
        branch = b_ref[0] * _silu(g_ref[0])
        up = _bdot(branch, wup_ref[i])
        term = jax.nn.sigmoid(mg_ref[0, :, i * d:(i + 1) * d]) * up
        merged = term if merged is None else merged + term
    out = x_ref[0] + gate_ref[0] * _bdot(merged, wout_ref[...])
    if final:
        ms = jnp.mean(out * out, axis=-1, keepdims=True)
        out = out * lax.rsqrt(ms + NORM_EPS) * fnw_ref[...]
    o_ref[0] = out


def _merge(o_nsa, o_s5, o_rwkv, p, x, gate, w_up, w_out, fnw, final, tm=256):
    bn, s, d = x.shape
    w = BRANCH_W
    rows = lambda width: pl.BlockSpec((1, tm, width), lambda b, i: (b, i, 0))
    wupb, woutb = w_up.astype(BF16), w_out.astype(BF16)
    return pl.pallas_call(
        functools.partial(_merge_kernel, final=final),
        grid=(bn, s // tm),
        in_specs=[rows(w)] * 6 + [rows(3 * d), rows(d),
                                  pl.BlockSpec((1, 1, d), lambda b, i: (b, 0, 0)),
                                  pl.BlockSpec(wupb.shape, lambda b, i: (0, 0, 0)),
                                  pl.BlockSpec(woutb.shape, lambda b, i: (0, 0)),
                                  pl.BlockSpec((1, d), lambda b, i: (0, 0))],
        out_specs=rows(d),
        out_shape=jax.ShapeDtypeStruct((bn, s, d), F32),
        compiler_params=pltpu.CompilerParams(dimension_semantics=("parallel", "parallel"),
                                             vmem_limit_bytes=VMEM_LIMIT),
        name="merge_out",
    )(o_nsa, o_s5, o_rwkv, p["ng"], p["sg"], p["rg"], p["mg"], x, gate.reshape(bn, 1, d),
      wupb, woutb, fnw.reshape(1, d))


def kernel(x, c, norm_w, mod_w, mod_b, w_in, cmp_pos_k, cmp_pos_v, cmp_w1_k, cmp_w2_k, cmp_w1_v, cmp_w2_v, s5_a_re, s5_a_im, s5_b_re, s5_b_im, s5_c_re, s5_c_im, s5_d, s5_log_dt, s5_glu_w, s5_glu_b, rwkv_mu, rwkv_w0, rwkv_w2, rwkv_a0, rwkv_a2, rwkv_k_k, rwkv_k_a, rwkv_r_k, rwkv_ln_w, rwkv_ln_b, w_up, w_out, final_norm_w):
    bn, s, d = x.shape
    depth = norm_w.shape[0]
    cos_t, sin_t = _rope_tables(s)
    mod = _modulation(c, mod_w, mod_b)
    for l in range(depth):
        shift, scale, gate = mod[l, :, 0:d], mod[l, :, d:2 * d], mod[l, :, 2 * d:3 * d]
        p = _inproj(x, norm_w[l], scale, shift, cos_t, sin_t, _pack_w_in(w_in[l]))
        k_cmp, v_cmp = _compress(p["kc"], p["vc"], cmp_pos_k[l], cmp_pos_v[l], cmp_w1_k[l], cmp_w2_k[l],
                                 cmp_w1_v[l], cmp_w2_v[l])
        o_nsa = _nsa(p, k_cmp, v_cmp)
        s5p = _s5_params(s5_a_re[l], s5_a_im[l], s5_b_re[l], s5_b_im[l], s5_c_re[l], s5_c_im[l],
                         s5_log_dt[l])
        o_s5 = _s5(jnp.swapaxes(p["su"], 0, 1), s5p, s5_d[l], s5_glu_w[l], s5_glu_b[l])
        o_s5 = jnp.swapaxes(o_s5, 0, 1)
        o_rwkv = _rwkv(p["mix"], rwkv_mu[l], rwkv_w0[l], rwkv_w2[l], rwkv_a0[l], rwkv_a2[l],
                       rwkv_k_k[l], rwkv_k_a[l], rwkv_r_k[l], rwkv_ln_w[l], rwkv_ln_b[l])
        x = _merge(o_nsa, o_s5, o_rwkv, p, x, gate, w_up[l], w_out[l], final_norm_w,
                   final=(l == depth - 1))
    return x
```

```python
import functools
import math

import numpy as np
import jax
import jax.numpy as jnp
from jax import lax
from jax.experimental import pallas as pl
from jax.experimental.pallas import tpu as pltpu

F32 = jnp.float32
BF16 = jnp.bfloat16
HIGHEST = lax.Precision.HIGHEST

HEAD_DIM = 64
NSA_HEADS = 8
NSA_GROUPS = 2
NSA_HPG = NSA_HEADS // NSA_GROUPS
CMP_BLOCK = 32
CMP_STRIDE = 16
SLC_BLOCK = 64
SLC_TOP = 16
WINDOW = 512
Q_BLOCK = 128
S5_GROUPS = 32
S5_GROUP_CH = 16
S5_STATE = 64
RWKV_HEADS = 8
BRANCH_W = 512
LORA_W = 64
RWKV_MIX_W = 3 * BRANCH_W + 2 * LORA_W
ROPE_THETA = 10000.0
NORM_EPS = 1e-6
RWKV_LN_EPS = 64e-5
NEG_BIG = -1e30
MASK_BIG = 2.0 ** 100
M_INIT = -3.0e38

VMEM_LIMIT = 56 * 1024 * 1024
LANES = 128
SUBLANES = 8

C_Q = 0
C_KC, C_KS, C_KW, C_VC, C_VS, C_VW = 512, 640, 768, 896, 1024, 1152
C_G = 1280
C_NG = 1408
C_SU = 1920
C_SG = 2432
C_MIX = 2944
C_RG = C_MIX + RWKV_MIX_W
C_MG = C_RG + BRANCH_W
IN_PACKED = C_MG + 3 * 1024


def _silu(z):
    return z * jax.nn.sigmoid(z)


def _bdot(a, b):
    return jnp.dot(a.astype(BF16), b.astype(BF16), preferred_element_type=F32)


def _dot_nt(a, b, precision=None):
    return lax.dot_general(a, b, (((1,), (1,)), ((), ())), precision=precision,
                           preferred_element_type=F32)


def _hdot(a, b):
    return jnp.dot(a, b, precision=HIGHEST, preferred_element_type=F32)


def _mod_kernel(c_ref, w_ref, b_ref, o_ref):
    cond = _silu(c_ref[...])
    o_ref[0] = _bdot(cond, w_ref[0]) + b_ref[0]


def _modulation(c, mod_w, mod_b):
    depth, d, d3 = mod_w.shape
    bn = c.shape[0]
    nj = d3 // d
    return pl.pallas_call(
        _mod_kernel,
        grid=(depth, nj),
        in_specs=[pl.BlockSpec((bn, d), lambda l, j: (0, 0)),
                  pl.BlockSpec((1, d, d), lambda l, j: (l, 0, j)),
                  pl.BlockSpec((1, 1, d), lambda l, j: (l, 0, j))],
        out_specs=pl.BlockSpec((1, bn, d), lambda l, j: (l, 0, j)),
        out_shape=jax.ShapeDtypeStruct((depth, bn, d3), F32),
        name="adaln_mod",
    )(c, mod_w, mod_b.reshape(depth, 1, d3))


_INPROJ_OUTS = (
    ("q", C_Q, 512, BF16), ("kc", C_KC, 128, F32), ("ks", C_KS, 128, BF16), ("kw", C_KW, 128, BF16),
    ("vc", C_VC, 128, F32), ("vs", C_VS, 128, BF16), ("vw", C_VW, 128, BF16), ("g", C_G, 128, F32),
    ("ng", C_NG, 512, F32), ("su", C_SU, 512, F32), ("sg", C_SG, 512, F32),
    ("mix", C_MIX, RWKV_MIX_W, F32), ("rg", C_RG, 512, F32), ("mg", C_MG, 3072, F32))
_ROPED = ("q", "kc", "ks", "kw")


def _inproj_kernel(x_ref, nw_ref, sc_ref, sh_ref, cos_ref, sin_ref, w_ref, *out_refs):
    x = x_ref[0]
    tm = x.shape[0]
    ms = jnp.mean(x * x, axis=-1, keepdims=True)
    h = x * lax.rsqrt(ms + NORM_EPS) * nw_ref[...]
    h = h * (1.0 + sc_ref[0]) + sh_ref[0]
    hb = h.astype(BF16)
    cos = cos_ref[...]
    sin = sin_ref[...]
    lane = lax.broadcasted_iota(jnp.int32, (tm, LANES), 1)
    first_half = (lane % HEAD_DIM) < (HEAD_DIM // 2)

    def rope(t):
        partner = jnp.where(first_half, pltpu.roll(t, 96, 1), pltpu.roll(t, 32, 1))
        return t * cos + partner * sin

    for (name, c0, width, dt), o_ref in zip(_INPROJ_OUTS, out_refs):
        step = min(width, 512)
        for j0 in range(0, width, step):
            w = min(step, width - j0)
            y = jnp.dot(hb, w_ref[:, c0 + j0:c0 + j0 + w], preferred_element_type=F32)
            if name in _ROPED:
                y = jnp.concatenate([rope(y[:, k:k + LANES]) for k in range(0, w, LANES)], axis=1)
            if name == "q":
                y = y * (HEAD_DIM ** -0.5)
            o_ref[0, :, j0:j0 + w] = y.astype(dt)


def _inproj(x, nw, scale, shift, cos_t, sin_t, w_packed, tm=256):
    bn, s, d = x.shape
    out_shape = [jax.ShapeDtypeStruct((bn, s, w), dt) for (_, _, w, dt) in _INPROJ_OUTS]
    out_specs = [pl.BlockSpec((1, tm, w), lambda b, i: (b, i, 0)) for (_, _, w, _) in _INPROJ_OUTS]
    outs = pl.pallas_call(
        _inproj_kernel,
        grid=(bn, s // tm),
        in_specs=[pl.BlockSpec((1, tm, d), lambda b, i: (b, i, 0)),
                  pl.BlockSpec((1, d), lambda b, i: (0, 0)),
                  pl.BlockSpec((1, 1, d), lambda b, i: (b, 0, 0)),
                  pl.BlockSpec((1, 1, d), lambda b, i: (b, 0, 0)),
                  pl.BlockSpec((tm, LANES), lambda b, i: (i, 0)),
                  pl.BlockSpec((tm, LANES), lambda b, i: (i, 0)),
                  pl.BlockSpec((d, IN_PACKED), lambda b, i: (0, 0), pipeline_mode=pl.Buffered(1))],
        out_specs=out_specs,
        out_shape=out_shape,
        compiler_params=pltpu.CompilerParams(dimension_semantics=("parallel", "parallel"),
                                             vmem_limit_bytes=VMEM_LIMIT),
        name="inproj",
    )(x, nw.reshape(1, d), scale.reshape(bn, 1, d), shift.reshape(bn, 1, d), cos_t, sin_t, w_packed)
    return dict(zip([o[0] for o in _INPROJ_OUTS], outs))


def _pack_w_in(w_in):
    d = w_in.shape[0]
    sizes = (512, 768, 24, 512, 512, 512, RWKV_MIX_W, 512, 3072)
    offs = np.concatenate([[0], np.cumsum(sizes)])
    q, kv, g, ng, su, sg, mix, rg, mg = [w_in[:, offs[i]:offs[i + 1]] for i in range(len(sizes))]
    kc, vc, ks, vs, kw, vw = [kv[:, i * 128:(i + 1) * 128] for i in range(6)]
    gpad = jnp.pad(g, ((0, 0), (0, 128 - 24)))
    return jnp.concatenate([q, kc, ks, kw, vc, vs, vw, gpad, ng, su, sg, mix, rg, mg], axis=1).astype(BF16)


def _rope_tables(s):
    half = HEAD_DIM // 2
    inv = jnp.exp(-math.log(ROPE_THETA) * jnp.arange(half, dtype=F32) / half)
    ang = jnp.arange(s, dtype=F32)[:, None] * inv[None, :]
    cos, sin = jnp.cos(ang), jnp.sin(ang)
    cos_t = jnp.tile(cos, (1, LANES // half))
    sin_t = jnp.tile(jnp.concatenate([-sin, sin], axis=1), (1, LANES // HEAD_DIM))
    return cos_t, sin_t


def _compress_kernel(kc_ref, vc_ref, pk_ref, pv_ref, wkt_ref, wkb_ref, wvt_ref, wvb_ref,
                     w2k_ref, w2v_ref, ko_ref, vo_ref):
    def one(x_ref, p_ref, wt_ref, wb_ref, w2_ref, o_ref):
        x = x_ref[0]
        n = x.shape[0]
        top = _bdot(x + p_ref[0:1, :], wt_ref[...])
        bot = _bdot(x + p_ref[1:2, :], wb_ref[...])
        hid = top + pltpu.roll(bot, n - 1, 0)
        act = _silu(hid)
        hw = act.shape[1] // NSA_GROUPS
        for g in range(NSA_GROUPS):
            o_ref[0, g] = _bdot(act[:, g * hw:(g + 1) * hw], w2_ref[...])

    one(kc_ref, pk_ref, wkt_ref, wkb_ref, w2k_ref, ko_ref)
    one(vc_ref, pv_ref, wvt_ref, wvb_ref, w2v_ref, vo_ref)


def _compress_weights(pos, w1, w2):
    hid = w1.shape[1]
    half = CMP_BLOCK // 2
    w1r = w1.reshape(2, half, HEAD_DIM, hid)
    eye = jnp.eye(NSA_GROUPS, dtype=w1.dtype)
    wd = jnp.einsum("tldj,gh->tlgdhj", w1r, eye).reshape(2, half * NSA_GROUPS * HEAD_DIM,
                                                         NSA_GROUPS * hid)
    pr = pos.reshape(2, half, 1, HEAD_DIM)
    pt = jnp.broadcast_to(pr, (2, half, NSA_GROUPS, HEAD_DIM)).reshape(2, -1)
    return pt, wd[0].astype(BF16), wd[1].astype(BF16), w2.astype(BF16)


def _compress(kc, vc, pos_k, pos_v, w1k, w2k, w1v, w2v):
    bn, s, kvw = kc.shape
    n16 = s // CMP_STRIDE
    row_w = CMP_STRIDE * kvw
    pk, wkt, wkb, w2kb = _compress_weights(pos_k, w1k, w2k)
    pv, wvt, wvb, w2vb = _compress_weights(pos_v, w1v, w2v)
    hid2 = wkt.shape[1]
    full = lambda shape: pl.BlockSpec(shape, lambda b: tuple(0 for _ in shape))
    out = jax.ShapeDtypeStruct((bn, NSA_GROUPS, n16, HEAD_DIM), F32)
    return pl.pallas_call(
        _compress_kernel,
        grid=(bn,),
        in_specs=[pl.BlockSpec((1, n16, row_w), lambda b: (b, 0, 0)),
                  pl.BlockSpec((1, n16, row_w), lambda b: (b, 0, 0)),
                  full((2, row_w)), full((2, row_w)),
                  full((row_w, hid2)), full((row_w, hid2)), full((row_w, hid2)), full((row_w, hid2)),
                  full(w2kb.shape), full(w2vb.shape)],
        out_specs=[pl.BlockSpec((1, NSA_GROUPS, n16, HEAD_DIM), lambda b: (b, 0, 0, 0))] * 2,
        out_shape=[out, out],
        compiler_params=pltpu.CompilerParams(dimension_semantics=("parallel",),
                                             vmem_limit_bytes=VMEM_LIMIT),
        name="nsa_compress",
    )(kc.reshape(bn, n16, row_w), vc.reshape(bn, n16, row_w), pk, pv, wkt, wkb, wvt, wvb, w2kb, w2vb)


SEL_TILE = 256
WIN_TILE = 128
MASK_W = 64


def _nsa_kernel(q_ref, g_ref, kcmp_ref, vcmp_ref, ks_ref, vs_ref, kw_ref, vw_ref, c2s_ref,
                o_ref, kaug_ref, *, top):
    qb = pl.program_id(1)
    s0 = qb * Q_BLOCK
    s = ks_ref.shape[1]
    nc = kcmp_ref.shape[2]
    rows = NSA_HPG * Q_BLOCK

    @pl.when(qb == 0)
    def _():
        blk = lax.broadcasted_iota(jnp.int32, (s, MASK_W), 0) // SLC_BLOCK
        col = lax.broadcasted_iota(jnp.int32, (s, MASK_W), 1)
        onehot = jnp.where(blk == col, 1.0, 0.0).astype(BF16)
        for g in range(NSA_GROUPS):
            kaug_ref[g, :, 0:HEAD_DIM] = ks_ref[0, :, g * HEAD_DIM:(g + 1) * HEAD_DIM]
            kaug_ref[g, :, HEAD_DIM:HEAD_DIM + MASK_W] = onehot

    q_all = q_ref[0]
    gates = jax.nn.sigmoid(g_ref[0])
    t_rows = s0 + lax.broadcasted_iota(jnp.int32, (rows, 1), 0) % Q_BLOCK

    for g in range(NSA_GROUPS):
        qg = jnp.concatenate([q_all[:, (g * NSA_HPG + r) * HEAD_DIM:(g * NSA_HPG + r + 1) * HEAD_DIM]
                              for r in range(NSA_HPG)], axis=0)
        kc = kcmp_ref[0, g].astype(BF16)
        vc = vcmp_ref[0, g].astype(BF16)

        sc = _dot_nt(qg, kc)
        n_col = lax.broadcasted_iota(jnp.int32, (rows, nc), 1)
        valid = (CMP_STRIDE * n_col + CMP_BLOCK - 1) <= t_rows
        scm = jnp.where(valid, sc, NEG_BIG)
        e = jnp.exp(scm - jnp.max(scm, axis=1, keepdims=True))
        pc = jnp.where(valid, e / jnp.sum(e, axis=1, keepdims=True), 0.0)
        o_c = _bdot(pc, vc)

        st = _dot_nt(kc, qg)
        n_row = lax.broadcasted_iota(jnp.int32, (nc, rows), 0)
        t_col = s0 + lax.broadcasted_iota(jnp.int32, (nc, rows), 1) % Q_BLOCK
        valid_t = (CMP_STRIDE * n_row + CMP_BLOCK - 1) <= t_col
        stm = jnp.where(valid_t, st, NEG_BIG)
        et = jnp.exp(stm - jnp.max(stm, axis=0, keepdims=True))
        pt = jnp.where(valid_t, et / jnp.sum(et, axis=0, keepdims=True), 0.0)
        psum = pt[:, 0:Q_BLOCK]
        for r in range(1, NSA_HPG):
            psum = psum + pt[:, r * Q_BLOCK:(r + 1) * Q_BLOCK]
        imp = _hdot(c2s_ref[...], psum)
        j_idx = lax.broadcasted_iota(jnp.int32, (MASK_W, Q_BLOCK), 0)
        blk_t = (s0 + lax.broadcasted_iota(jnp.int32, (MASK_W, Q_BLOCK), 1)) // SLC_BLOCK
        causal = j_idx <= blk_t
        forced = (j_idx == 0) | (j_idx == blk_t)
        imp = jnp.where(forced, jnp.inf, jnp.where(causal, imp, -jnp.inf))
        cnt = jnp.zeros((MASK_W, Q_BLOCK), jnp.int32)
        for i in range(s // SLC_BLOCK):
            row = imp[i:i + 1, :]
            tie = jnp.where(j_idx > i, 1, 0)
            cnt = cnt + jnp.where(row > imp, 1, jnp.where(row == imp, tie, 0))
        sel_t = jnp.where(causal, jnp.where(cnt < top, 0.0, -MASK_BIG), -MASK_BIG)
        mterm = sel_t.T.astype(BF16)
        q_aug = jnp.concatenate([qg, jnp.concatenate([mterm] * NSA_HPG, axis=0)], axis=1)

        def sel_body(i, carry):
            m, l, acc = carry
            k0 = pl.multiple_of(i * SEL_TILE, SEL_TILE)
            kt = kaug_ref[g, pl.ds(k0, SEL_TILE), :]
            sc_ = _dot_nt(q_aug, kt)
            kpos = k0 + lax.broadcasted_iota(jnp.int32, (rows, SEL_TILE), 1)
            sc_ = jnp.where(kpos <= t_rows, sc_, -MASK_BIG)
            m_new = jnp.maximum(m, jnp.max(sc_, axis=1, keepdims=True))
            alpha = jnp.exp(m - m_new)
            p = jnp.exp(sc_ - m_new)
            l = alpha * l + jnp.sum(p, axis=1, keepdims=True)
            acc = alpha * acc + jnp.dot(p.astype(BF16), vs_ref[0, pl.ds(k0, SEL_TILE), :],
                                        preferred_element_type=F32)
            return m_new, l, acc

        init = (jnp.full((rows, 1), M_INIT, F32), jnp.zeros((rows, 1), F32),
                jnp.zeros((rows, NSA_GROUPS * HEAD_DIM), F32))
        n_sel = (s0 + Q_BLOCK + SEL_TILE - 1) // SEL_TILE
        _, l_s, acc_s = lax.fori_loop(0, n_sel, sel_body, init)
        o_s = acc_s[:, g * HEAD_DIM:(g + 1) * HEAD_DIM] / l_s

        zeros = jnp.zeros((rows, HEAD_DIM), BF16)
        q_w = jnp.concatenate([qg, zeros] if g == 0 else [zeros, qg], axis=1)

        def win_body(i, carry):
            m, l, acc = carry
            k0 = pl.multiple_of(i * WIN_TILE, WIN_TILE)
            sc_ = _dot_nt(q_w, kw_ref[0, pl.ds(k0, WIN_TILE), :])
            kpos = k0 + lax.broadcasted_iota(jnp.int32, (rows, WIN_TILE), 1)
            ok = (kpos <= t_rows) & (kpos > t_rows - WINDOW)
            sc_ = jnp.where(ok, sc_, -MASK_BIG)
            m_new = jnp.maximum(m, jnp.max(sc_, axis=1, keepdims=True))
            alpha = jnp.exp(m - m_new)
            p = jnp.where(ok, jnp.exp(sc_ - m_new), 0.0)
            l = alpha * l + jnp.sum(p, axis=1, keepdims=True)
            acc = alpha * acc + jnp.dot(p.astype(BF16), vw_ref[0, pl.ds(k0, WIN_TILE), :],
                                        preferred_element_type=F32)
            return m_new, l, acc

        w_lo = jnp.maximum(qb - WINDOW // WIN_TILE, 0)
        _, l_w, acc_w = lax.fori_loop(w_lo, qb + 1, win_body, init)
        o_w = acc_w[:, g * HEAD_DIM:(g + 1) * HEAD_DIM] / l_w

        for r in range(NSA_HPG):
            hh = g * NSA_HPG + r
            rs = slice(r * Q_BLOCK, (r + 1) * Q_BLOCK)
            o = (gates[:, 3 * hh:3 * hh + 1] * o_c[rs] + gates[:, 3 * hh + 1:3 * hh + 2] * o_s[rs]
                 + gates[:, 3 * hh + 2:3 * hh + 3] * o_w[rs])
            o_ref[0, :, hh * HEAD_DIM:(hh + 1) * HEAD_DIM] = o


def _cmp_to_slc_t(s):
    n_cmp = s // CMP_STRIDE
    n_slc = s // SLC_BLOCK
    cs = CMP_STRIDE * np.arange(n_cmp)[:, None]
    ss = SLC_BLOCK * np.arange(n_slc)[None, :]
    overlap = np.clip(np.minimum(cs + CMP_BLOCK, ss + SLC_BLOCK) - np.maximum(cs, ss), 0, None)
    m = np.zeros((MASK_W, n_cmp), np.float32)
    m[:n_slc] = (overlap / CMP_BLOCK).T
    return jnp.asarray(m)


def _nsa(p, k_cmp, v_cmp):
    q = p["q"]
    bn, s, _ = q.shape
    assert s // SLC_BLOCK <= MASK_W and s % SEL_TILE == 0
    top = min(SLC_TOP, s // SLC_BLOCK)
    nc = k_cmp.shape[2]
    kvw = NSA_GROUPS * HEAD_DIM
    per_b = lambda w: pl.BlockSpec((1, s, w), lambda b, i: (b, 0, 0))
    return pl.pallas_call(
        functools.partial(_nsa_kernel, top=top),
        grid=(bn, s // Q_BLOCK),
        in_specs=[pl.BlockSpec((1, Q_BLOCK, NSA_HEADS * HEAD_DIM), lambda b, i: (b, i, 0)),
                  pl.BlockSpec((1, Q_BLOCK, LANES), lambda b, i: (b, i, 0)),
                  pl.BlockSpec((1, NSA_GROUPS, nc, HEAD_DIM), lambda b, i: (b, 0, 0, 0)),
                  pl.BlockSpec((1, NSA_GROUPS, nc, HEAD_DIM), lambda b, i: (b, 0, 0, 0)),
                  per_b(kvw), per_b(kvw), per_b(kvw), per_b(kvw),
                  pl.BlockSpec((MASK_W, nc), lambda b, i: (0, 0))],
        out_specs=pl.BlockSpec((1, Q_BLOCK, NSA_HEADS * HEAD_DIM), lambda b, i: (b, i, 0)),
        out_shape=jax.ShapeDtypeStruct((bn, s, NSA_HEADS * HEAD_DIM), F32),
        scratch_shapes=[pltpu.VMEM((NSA_GROUPS, s, HEAD_DIM + MASK_W), BF16)],
        compiler_params=pltpu.CompilerParams(dimension_semantics=("parallel", "arbitrary"),
                                             vmem_limit_bytes=VMEM_LIMIT),
        name="nsa_attention",
    )(q, p["g"], k_cmp, v_cmp, p["ks"], p["vs"], p["kw"], p["vw"], _cmp_to_slc_t(s))


S5_STRIP = 512


def _gelu_tanh(y):
    return 0.5 * y * (1.0 + jnp.tanh(math.sqrt(2.0 / math.pi) * (y + 0.044715 * (y * y * y))))


def _s5_kernel(u_ref, bre_ref, bim_ref, are_ref, aim_ref, cre_ref, cim_ref, d_ref, gw_ref, gb_ref,
               o_ref, xre_ref, xim_ref, sre_ref, sim_ref):
    tc, bn, width = u_ref.shape
    nstate = are_ref.shape[1]

    @pl.when(pl.program_id(0) == 0)
    def _():
        sre_ref[...] = jnp.zeros_like(sre_ref)
        sim_ref[...] = jnp.zeros_like(sim_ref)

    u = u_ref[...].reshape(tc * bn, width)
    ub = u.astype(BF16)
    xre_ref[...] = jnp.dot(ub, bre_ref[...], preferred_element_type=F32)
    xim_ref[...] = jnp.dot(ub, bim_ref[...], preferred_element_type=F32)

    for c0 in range(0, nstate, S5_STRIP):
        cols = pl.ds(c0, S5_STRIP)
        a_r = jnp.broadcast_to(are_ref[:, c0:c0 + S5_STRIP], (bn, S5_STRIP))
        a_i = jnp.broadcast_to(aim_ref[:, c0:c0 + S5_STRIP], (bn, S5_STRIP))

        def step(t, carry):
            x_r, x_i = carry
            r0 = pl.multiple_of(t * bn, bn)
            n_r = a_r * x_r - a_i * x_i + xre_ref[pl.ds(r0, bn), cols]
            n_i = a_r * x_i + a_i * x_r + xim_ref[pl.ds(r0, bn), cols]
            xre_ref[pl.ds(r0, bn), cols] = n_r
            xim_ref[pl.ds(r0, bn), cols] = n_i
            return n_r, n_i

        x_r, x_i = lax.fori_loop(0, tc, step, (sre_ref[:, cols], sim_ref[:, cols]), unroll=8)
        sre_ref[:, cols] = x_r
        sim_ref[:, cols] = x_i

    y = (jnp.dot(xre_ref[...].astype(BF16), cre_ref[...], preferred_element_type=F32)
         + jnp.dot(xim_ref[...].astype(BF16), cim_ref[...], preferred_element_type=F32)
         + d_ref[...] * u)
    y = _gelu_tanh(y)
    z = _bdot(y, gw_ref[...]) + gb_ref[...]
    o_ref[...] = (y * jax.nn.sigmoid(z)).reshape(tc, bn, width)


def _s5_params(a_re, a_im, b_re, b_im, c_re, c_im, log_dt):
    g, p = a_re.shape
    c = b_re.shape[-1]
    a_re, a_im = a_re.astype(F32), a_im.astype(F32)
    b_re, b_im = b_re.astype(F32), b_im.astype(F32)
    dt = jnp.exp(log_dt.astype(F32))[:, None]
    mag = jnp.exp(a_re * dt)
    abar_re, abar_im = mag * jnp.cos(a_im * dt), mag * jnp.sin(a_im * dt)
    den = a_re * a_re + a_im * a_im
    f_re = ((abar_re - 1.0) * a_re + abar_im * a_im) / den
    f_im = (abar_im * a_re - (abar_re - 1.0) * a_im) / den
    bbar_re = f_re[..., None] * b_re - f_im[..., None] * b_im
    bbar_im = f_re[..., None] * b_im + f_im[..., None] * b_re
    eye = jnp.eye(g, dtype=F32)
    pack_b = lambda m: jnp.einsum("gpc,gh->gchp", m, eye).reshape(g * c, g * p).astype(BF16)
    pack_c = lambda m: jnp.einsum("gcp,gh->gphc", m, eye).reshape(g * p, g * c).astype(BF16)
    return (pack_b(bbar_re), pack_b(bbar_im), abar_re.reshape(1, g * p), abar_im.reshape(1, g * p),
            pack_c(c_re.astype(F32)), pack_c(-c_im.astype(F32)))


def _s5(u_t, params, d, glu_w, glu_b, tc=64):
    s, bn, width = u_t.shape
    assert bn == SUBLANES
    bre, bim, are, aim, cre, cim = params
    nstate = are.shape[1]
    full = lambda a: pl.BlockSpec(a.shape, lambda i: tuple(0 for _ in a.shape))
    d2, gb2, gwb = d.reshape(1, width), glu_b.reshape(1, width), glu_w.astype(BF16)
    return pl.pallas_call(
        _s5_kernel,
        grid=(s // tc,),
        in_specs=[pl.BlockSpec((tc, bn, width), lambda i: (i, 0, 0)),
                  full(bre), full(bim), full(are), full(aim), full(cre), full(cim),
                  full(d2), full(gwb), full(gb2)],
        out_specs=pl.BlockSpec((tc, bn, width), lambda i: (i, 0, 0)),
        out_shape=jax.ShapeDtypeStruct((s, bn, width), F32),
        scratch_shapes=[pltpu.VMEM((tc * bn, nstate), F32), pltpu.VMEM((tc * bn, nstate), F32),
                        pltpu.VMEM((bn, nstate), F32), pltpu.VMEM((bn, nstate), F32)],
        compiler_params=pltpu.CompilerParams(dimension_semantics=("arbitrary",),
                                             vmem_limit_bytes=VMEM_LIMIT),
        name="s5_scan",
    )(u_t, bre, bim, are, aim, cre, cim, d2, gwb, gb2)


RWKV_CHUNK = 64
RWKV_BLOCK = 256


def _split_dot(x, seg):
    hi = x.astype(BF16)
    lo = (x - hi.astype(F32)).astype(BF16)
    return (jnp.dot(hi, seg, preferred_element_type=F32) + jnp.dot(lo, seg, preferred_element_type=F32))


def _rwkv_kernel(mix_ref, mu_ref, w0_ref, w2_ref, a0_ref, a2_ref, kk_ref, ka_ref, rk_ref,
                 lnw_ref, lnb_ref, seg_ref, o_ref, state_ref, carry_ref, y_ref):
    n = HEAD_DIM
    chunk = RWKV_CHUNK
    tb = mix_ref.shape[1]
    w = BRANCH_W

    @pl.when(pl.program_id(1) == 0)
    def _():
        state_ref[...] = jnp.zeros_like(state_ref)
        carry_ref[...] = jnp.zeros_like(carry_ref)

    mix = mix_ref[0]
    row_m = lax.broadcasted_iota(jnp.int32, mix.shape, 0)
    prev = jnp.where(row_m == 0, carry_ref[0:1, :], pltpu.roll(mix, 1, 0))
    carry_ref[0:1, :] = mix[tb - 1:tb, :]
    xs = mix + (prev - mix) * mu_ref[...]
    r, k, v = xs[:, 0:w], xs[:, w:2 * w], xs[:, 2 * w:3 * w]
    wl, al = xs[:, 3 * w:3 * w + LORA_W], xs[:, 3 * w + LORA_W:3 * w + 2 * LORA_W]

    lw = w0_ref[...] + _bdot(jnp.tanh(wl), w2_ref[...])
    z = -lw
    softplus = jnp.maximum(z, 0.0) + jnp.log(1.0 + jnp.exp(-jnp.abs(z)))
    ld = -jnp.exp(-softplus - 0.5)
    a = jax.nn.sigmoid(a0_ref[...] + _bdot(al, a2_ref[...]))
    seg = seg_ref[...]
    kk = k * kk_ref[...]
    kk = kk / jnp.maximum(jnp.sqrt(_split_dot(kk * kk, seg)), 1e-12)
    k2 = k * (1.0 + (a - 1.0) * ka_ref[...])
    aa = -kk
    bb = kk * a

    row = lax.broadcasted_iota(jnp.int32, (tb, w), 0) % chunk
    cum = ld
    sh = 1
    while sh < chunk:
        cum = cum + jnp.where(row >= sh, pltpu.roll(cum, sh, 0), 0.0)
        sh *= 2
    at = (aa * jnp.exp(cum - ld)).astype(BF16)
    rt = (r * jnp.exp(cum)).astype(BF16)
    e_neg = jnp.exp(-cum)
    bt, kt = (bb * e_neg).astype(BF16), (k2 * e_neg).astype(BF16)
    vb = v.astype(BF16)

    ti = lax.broadcasted_iota(jnp.int32, (2 * chunk, chunk), 0)
    si = lax.broadcasted_iota(jnp.int32, (2 * chunk, chunk), 1)
    strict = si < ti
    lower2 = si < jnp.where(ti < chunk, ti, ti - chunk + 1)
    n_double = int(math.log2(chunk)) - 1

    n_chunks = tb // chunk
    pairs = [(ci, h) for ci in range(n_chunks) for h in range(RWKV_HEADS)]
    rows_of = lambda ci: slice(ci * chunk, (ci + 1) * chunk)
    lanes_of = lambda h: slice(h * n, (h + 1) * n)
    bdot32 = lambda p_, q_: jnp.dot(p_, q_, preferred_element_type=F32)

    at_l = [at[rows_of(ci), lanes_of(h)] for ci, h in pairs]
    rt_l = [rt[rows_of(ci), lanes_of(h)] for ci, h in pairs]
    v_l = [vb[rows_of(ci), lanes_of(h)] for ci, h in pairs]
    amat_l = [_dot_nt(jnp.concatenate([at_l[i], rt_l[i]], axis=0),
                      jnp.concatenate([bt[rows_of(ci), lanes_of(h)], kt[rows_of(ci), lanes_of(h)]], axis=0))
              for i, (ci, h) in enumerate(pairs)]
    pw_l = [jnp.where(strict[:chunk], m[:chunk, :chunk], 0.0).astype(BF16) for m in amat_l]
    arb_l = [jnp.where(lower2[chunk:], m[chunk:, :chunk], 0.0).astype(BF16) for m in amat_l]
    axk_l = [jnp.where(lower2, m[:, chunk:], 0.0).astype(BF16) for m in amat_l]
    xv_l = [bdot32(axk_l[i], v_l[i]) for i in range(len(pairs))]
    x_l = [jnp.concatenate([at_l[i].astype(F32), xv_l[i][:chunk]], axis=1) for i in range(len(pairs))]
    x_l = [x + bdot32(pw, x.astype(BF16)) for x, pw in zip(x_l, pw_l)]
    for _ in range(n_double):
        pw_l = [bdot32(pw, pw).astype(BF16) for pw in pw_l]
        x_l = [x + bdot32(pw, x.astype(BF16)) for x, pw in zip(x_l, pw_l)]
    ro_l = [jnp.concatenate([rt_l[i].astype(F32), xv_l[i][chunk:]], axis=1)
            + bdot32(arb_l[i], x_l[i].astype(BF16)) for i in range(len(pairs))]
    wr_l = [jnp.concatenate([x[:, :n], ro[:, :n]], axis=0).astype(BF16) for x, ro in zip(x_l, ro_l)]

    state = [state_ref[h] for h in range(RWKV_HEADS)]
    for ci in range(n_chunks):
        rows = rows_of(ci)
        cum_c = cum[rows]
        tot = cum_c[chunk - 1:chunk, :]
        e_rem = jnp.exp(tot - cum_c)
        bh, kh = (bb[rows] * e_rem).astype(BF16), (k2[rows] * e_rem).astype(BF16)
        p_tot = jnp.exp(tot)
        base = ci * RWKV_HEADS
        g_l = [_dot_nt(wr_l[base + h], state[h].astype(BF16)) for h in range(RWKV_HEADS)]
        for h in range(RWKV_HEADS):
            y_ref[rows, lanes_of(h)] = g_l[h][chunk:] + ro_l[base + h][:, n:]
        uv_l = [jnp.concatenate([g_l[h][:chunk] + x_l[base + h][:, n:], v_l[base + h].astype(F32)], axis=0)
                for h in range(RWKV_HEADS)]
        state = [state[h] * p_tot[:, lanes_of(h)]
                 + bdot32(uv_l[h].T.astype(BF16),
                          jnp.concatenate([bh[:, lanes_of(h)], kh[:, lanes_of(h)]], axis=0))
                 for h in range(RWKV_HEADS)]
    for h in range(RWKV_HEADS):
        state_ref[h] = state[h]

    y = y_ref[...]
    inv_n = 1.0 / n
    mean = _split_dot(y, seg) * inv_n
    dev = y - mean
    var = _split_dot(dev * dev, seg) * inv_n
    yn = dev * lax.rsqrt(var + RWKV_LN_EPS) * lnw_ref[...] + lnb_ref[...]
    bonus = _split_dot(r * k2 * rk_ref[...], seg) * v
    o_ref[0] = yn + bonus


def _rwkv(mix, mu, w0, w2, a0, a2, k_k, k_a, r_k, ln_w, ln_b):
    bn, s, mw = mix.shape
    w = BRANCH_W
    tb = min(RWKV_BLOCK, s)
    head = np.arange(w) // HEAD_DIM
    seg = jnp.asarray((head[:, None] == head[None, :]).astype(np.float32)).astype(BF16)
    row = lambda t: t.reshape(1, -1).astype(F32)
    args = (row(mu), row(w0), w2.astype(BF16), row(a0), a2.astype(BF16), row(k_k), row(k_a), row(r_k),
            row(ln_w), row(ln_b), seg)
    full = lambda a: pl.BlockSpec(a.shape, lambda b, i: tuple(0 for _ in a.shape))
    return pl.pallas_call(
        _rwkv_kernel,
        grid=(bn, s // tb),
        in_specs=[pl.BlockSpec((1, tb, mw), lambda b, i: (b, i, 0))] + [full(t) for t in args],
        out_specs=pl.BlockSpec((1, tb, w), lambda b, i: (b, i, 0)),
        out_shape=jax.ShapeDtypeStruct((bn, s, w), F32),
        scratch_shapes=[pltpu.VMEM((RWKV_HEADS, HEAD_DIM, HEAD_DIM), F32),
                        pltpu.VMEM((SUBLANES, mw), F32),
                        pltpu.VMEM((tb, w), F32)],
        compiler_params=pltpu.CompilerParams(dimension_semantics=("parallel", "arbitrary"),
                                             vmem_limit_bytes=VMEM_LIMIT),
        name="rwkv7",
    )(mix, *args)


def _merge_kernel(on_ref, os_ref, or_ref, gn_ref, gs_ref, gr_ref, mg_ref, x_ref, gate_ref,
                  wup_ref, wout_ref, fnw_ref, o_ref, *, final):
    d = x_ref.shape[2]
    merged = None
    for i, (b_ref, g_ref) in enumerate(((on_ref, gn_ref), (os_ref, gs_ref), (or_ref, gr_ref))):
        branch = b_ref[0] * _silu(g_ref[0])
        up = _bdot(branch, wup_ref[i])
        term = jax.nn.sigmoid(mg_ref[0, :, i * d:(i + 1) * d]) * up
        merged = term if merged is None else merged + term
    out = x_ref[0] + gate_ref[0] * _bdot(merged, wout_ref[...])
    if final:
        ms = jnp.mean(out * out, axis=-1, keepdims=True)
        out = out * lax.rsqrt(ms + NORM_EPS) * fnw_ref[...]
    o_ref[0] = out


def _merge(o_nsa, o_s5, o_rwkv, p, x, gate, w_up, w_out, fnw, final, tm=256):
    bn, s, d = x.shape
    w = BRANCH_W
    rows = lambda width: pl.BlockSpec((1, tm, width), lambda b, i: (b, i, 0))
    wupb, woutb = w_up.astype(BF16), w_out.astype(BF16)
    return pl.pallas_call(
        functools.partial(_merge_kernel, final=final),
        grid=(bn, s // tm),
        in_specs=[rows(w)] * 6 + [rows(3 * d), rows(d),
                                  pl.BlockSpec((1, 1, d), lambda b, i: (b, 0, 0)),
                                  pl.BlockSpec(wupb.shape, lambda b, i: (0, 0, 0)),
                                  pl.BlockSpec(woutb.shape, lambda b, i: (0, 0)),
                                  pl.BlockSpec((1, d), lambda b, i: (0, 0))],
        out_specs=rows(d),
        out_shape=jax.ShapeDtypeStruct((bn, s, d), F32),
        compiler_params=pltpu.CompilerParams(dimension_semantics=("parallel", "parallel"),
                                             vmem_limit_bytes=VMEM_LIMIT),
        name="merge_out",
    )(o_nsa, o_s5, o_rwkv, p["ng"], p["sg"], p["rg"], p["mg"], x, gate.reshape(bn, 1, d),
      wupb, woutb, fnw.reshape(1, d))


def kernel(x, c, norm_w, mod_w, mod_b, w_in, cmp_pos_k, cmp_pos_v, cmp_w1_k, cmp_w2_k, cmp_w1_v, cmp_w2_v, s5_a_re, s5_a_im, s5_b_re, s5_b_im, s5_c_re, s5_c_im, s5_d, s5_log_dt, s5_glu_w, s5_glu_b, rwkv_mu, rwkv_w0, rwkv_w2, rwkv_a0, rwkv_a2, rwkv_k_k, rwkv_k_a, rwkv_r_k, rwkv_ln_w, rwkv_ln_b, w_up, w_out, final_norm_w):
    bn, s, d = x.shape
    depth = norm_w.shape[0]
    cos_t, sin_t = _rope_tables(s)
    mod = _modulation(c, mod_w, mod_b)
    for l in range(depth):
        shift, scale, gate = mod[l, :, 0:d], mod[l, :, d:2 * d], mod[l, :, 2 * d:3 * d]
        p = _inproj(x, norm_w[l], scale, shift, cos_t, sin_t, _pack_w_in(w_in[l]))
        k_cmp, v_cmp = _compress(p["kc"], p["vc"], cmp_pos_k[l], cmp_pos_v[l], cmp_w1_k[l], cmp_w2_k[l],
                                 cmp_w1_v[l], cmp_w2_v[l])
        o_nsa = _nsa(p, k_cmp, v_cmp)
        s5p = _s5_params(s5_a_re[l], s5_a_im[l], s5_b_re[l], s5_b_im[l], s5_c_re[l], s5_c_im[l],
                         s5_log_dt[l])
        o_s5 = _s5(jnp.swapaxes(p["su"], 0, 1), s5p, s5_d[l], s5_glu_w[l], s5_glu_b[l])
        o_s5 = jnp.swapaxes(o_s5, 0, 1)
        o_rwkv = _rwkv(p["mix"], rwkv_mu[l], rwkv_w0[l], rwkv_w2[l], rwkv_a0[l], rwkv_a2[l],
                       rwkv_k_k[l], rwkv_k_a[l], rwkv_r_k[l], rwkv_ln_w[l], rwkv_ln_b[l])
        x = _merge(o_nsa, o_s5, o_rwkv, p, x, gate, w_up[l], w_out[l], final_norm_w,
                   final=(l == depth - 1))
    return x
```

```python
import functools
import math

import numpy as np
import jax
import jax.numpy as jnp
from jax import lax
from jax.experimental import pallas as pl
from jax.experimental.pallas import tpu as pltpu

F32 = jnp.float32
BF16 = jnp.bfloat16
HIGHEST = lax.Precision.HIGHEST

HEAD_DIM = 64
NSA_HEADS = 8
NSA_GROUPS = 2
NSA_HPG = NSA_HEADS // NSA_GROUPS
CMP_BLOCK = 32
CMP_STRIDE = 16
SLC_BLOCK = 64
SLC_TOP = 16
WINDOW = 512
Q_BLOCK = 128
S5_GROUPS = 32
S5_GROUP_CH = 16
S5_STATE = 64
RWKV_HEADS = 8
BRANCH_W = 512
LORA_W = 64
RWKV_MIX_W = 3 * BRANCH_W + 2 * LORA_W
ROPE_THETA = 10000.0
NORM_EPS = 1e-6
RWKV_LN_EPS = 64e-5
NEG_BIG = -1e30
MASK_BIG = 2.0 ** 100
M_INIT = -3.0e38

VMEM_LIMIT = 56 * 1024 * 1024
LANES = 128
SUBLANES = 8

C_Q = 0
C_KC, C_KS, C_KW, C_VC, C_VS, C_VW = 512, 640, 768, 896, 1024, 1152
C_G = 1280
C_NG = 1408
C_SU = 1920
C_SG = 2432
C_MIX = 2944
C_RG = C_MIX + RWKV_MIX_W
C_MG = C_RG + BRANCH_W
IN_PACKED = C_MG + 3 * 1024


def _silu(z):
    return z * jax.nn.sigmoid(z)


def _bdot(a, b):
    return jnp.dot(a.astype(BF16), b.astype(BF16), preferred_element_type=F32)


def _dot_nt(a, b, precision=None):
    return lax.dot_general(a, b, (((1,), (1,)), ((), ())), precision=precision,
                           preferred_element_type=F32)


def _hdot(a, b):
    return jnp.dot(a, b, precision=HIGHEST, preferred_element_type=F32)


def _mod_kernel(c_ref, w_ref, b_ref, o_ref):
    cond = _silu(c_ref[...])
    o_ref[0] = _bdot(cond, w_ref[0]) + b_ref[0]


def _modulation(c, mod_w, mod_b):
    depth, d, d3 = mod_w.shape
    bn = c.shape[0]
    nj = d3 // d
    return pl.pallas_call(
        _mod_kernel,
        grid=(depth, nj),
        in_specs=[pl.BlockSpec((bn, d), lambda l, j: (0, 0)),
                  pl.BlockSpec((1, d, d), lambda l, j: (l, 0, j)),
                  pl.BlockSpec((1, 1, d), lambda l, j: (l, 0, j))],
        out_specs=pl.BlockSpec((1, bn, d), lambda l, j: (l, 0, j)),
        out_shape=jax.ShapeDtypeStruct((depth, bn, d3), F32),
        name="adaln_mod",
    )(c, mod_w, mod_b.reshape(depth, 1, d3))


_INPROJ_OUTS = (
    ("q", C_Q, 512, BF16), ("kc", C_KC, 128, F32), ("ks", C_KS, 128, BF16), ("kw", C_KW, 128, BF16),
    ("vc", C_VC, 128, F32), ("vs", C_VS, 128, BF16), ("vw", C_VW, 128, BF16), ("g", C_G, 128, F32),
    ("ng", C_NG, 512, F32), ("su", C_SU, 512, F32), ("sg", C_SG, 512, F32),
    ("mix", C_MIX, RWKV_MIX_W, F32), ("rg", C_RG, 512, F32), ("mg", C_MG, 3072, F32))
_ROPED = ("q", "kc", "ks", "kw")


def _inproj_kernel(x_ref, nw_ref, sc_ref, sh_ref, cos_ref, sin_ref, w_ref, *out_refs):
    x = x_ref[0]
    tm = x.shape[0]
    ms = jnp.mean(x * x, axis=-1, keepdims=True)
    h = x * lax.rsqrt(ms + NORM_EPS) * nw_ref[...]
    h = h * (1.0 + sc_ref[0]) + sh_ref[0]
    hb = h.astype(BF16)
    cos = cos_ref[...]
    sin = sin_ref[...]
    lane = lax.broadcasted_iota(jnp.int32, (tm, LANES), 1)
    first_half = (lane % HEAD_DIM) < (HEAD_DIM // 2)

    def rope(t):
        partner = jnp.where(first_half, pltpu.roll(t, 96, 1), pltpu.roll(t, 32, 1))
        return t * cos + partner * sin

    for (name, c0, width, dt), o_ref in zip(_INPROJ_OUTS, out_refs):
        step = min(width, 512)
        for j0 in range(0, width, step):
            w = min(step, width - j0)
            y = jnp.dot(hb, w_ref[:, c0 + j0:c0 + j0 + w], preferred_element_type=F32)
            if name in _ROPED:
                y = jnp.concatenate([rope(y[:, k:k + LANES]) for k in range(0, w, LANES)], axis=1)
            if name == "q":
                y = y * (HEAD_DIM ** -0.5)
            o_ref[0, :, j0:j0 + w] = y.astype(dt)


def _inproj(x, nw, scale, shift, cos_t, sin_t, w_packed, tm=256):
    bn, s, d = x.shape
    out_shape = [jax.ShapeDtypeStruct((bn, s, w), dt) for (_, _, w, dt) in _INPROJ_OUTS]
    out_specs = [pl.BlockSpec((1, tm, w), lambda b, i: (b, i, 0)) for (_, _, w, _) in _INPROJ_OUTS]
    outs = pl.pallas_call(
        _inproj_kernel,
        grid=(bn, s // tm),
        in_specs=[pl.BlockSpec((1, tm, d), lambda b, i: (b, i, 0)),
                  pl.BlockSpec((1, d), lambda b, i: (0, 0)),
                  pl.BlockSpec((1, 1, d), lambda b, i: (b, 0, 0)),
                  pl.BlockSpec((1, 1, d), lambda b, i: (b, 0, 0)),
                  pl.BlockSpec((tm, LANES), lambda b, i: (i, 0)),
                  pl.BlockSpec((tm, LANES), lambda b, i: (i, 0)),
                  pl.BlockSpec((d, IN_PACKED), lambda b, i: (0, 0), pipeline_mode=pl.Buffered(1))],
        out_specs=out_specs,
        out_shape=out_shape,
        compiler_params=pltpu.CompilerParams(dimension_semantics=("parallel", "parallel"),
                                             vmem_limit_bytes=VMEM_LIMIT),
        name="inproj",
    )(x, nw.reshape(1, d), scale.reshape(bn, 1, d), shift.reshape(bn, 1, d), cos_t, sin_t, w_packed)
    return dict(zip([o[0] for o in _INPROJ_OUTS], outs))


def _pack_w_in(w_in):
    d = w_in.shape[0]
    sizes = (512, 768, 24, 512, 512, 512, RWKV_MIX_W, 512, 3072)
    offs = np.concatenate([[0], np.cumsum(sizes)])
    q, kv, g, ng, su, sg, mix, rg, mg = [w_in[:, offs[i]:offs[i + 1]] for i in range(len(sizes))]
    kc, vc, ks, vs, kw, vw = [kv[:, i * 128:(i + 1) * 128] for i in range(6)]
    gpad = jnp.pad(g, ((0, 0), (0, 128 - 24)))
    return jnp.concatenate([q, kc, ks, kw, vc, vs, vw, gpad, ng, su, sg, mix, rg, mg], axis=1).astype(BF16)


def _rope_tables(s):
    half = HEAD_DIM // 2
    inv = jnp.exp(-math.log(ROPE_THETA) * jnp.arange(half, dtype=F32) / half)
    ang = jnp.arange(s, dtype=F32)[:, None] * inv[None, :]
    cos, sin = jnp.cos(ang), jnp.sin(ang)
    cos_t = jnp.tile(cos, (1, LANES // half))
    sin_t = jnp.tile(jnp.concatenate([-sin, sin], axis=1), (1, LANES // HEAD_DIM))
    return cos_t, sin_t


def _compress_kernel(kc_ref, vc_ref, pk_ref, pv_ref, wkt_ref, wkb_ref, wvt_ref, wvb_ref,
                     w2k_ref, w2v_ref, ko_ref, vo_ref):
    def one(x_ref, p_ref, wt_ref, wb_ref, w2_ref, o_ref):
        x = x_ref[0]
        n = x.shape[0]
        top = _bdot(x + p_ref[0:1, :], wt_ref[...])
        bot = _bdot(x + p_ref[1:2, :], wb_ref[...])
        hid = top + pltpu.roll(bot, n - 1, 0)
        act = _silu(hid)
        hw = act.shape[1] // NSA_GROUPS
        for g in range(NSA_GROUPS):
            o_ref[0, g] = _bdot(act[:, g * hw:(g + 1) * hw], w2_ref[...])

    one(kc_ref, pk_ref, wkt_ref, wkb_ref, w2k_ref, ko_ref)
    one(vc_ref, pv_ref, wvt_ref, wvb_ref, w2v_ref, vo_ref)


def _compress_weights(pos, w1, w2):
    hid = w1.shape[1]
    half = CMP_BLOCK // 2
    w1r = w1.reshape(2, half, HEAD_DIM, hid)
    eye = jnp.eye(NSA_GROUPS, dtype=w1.dtype)
    wd = jnp.einsum("tldj,gh->tlgdhj", w1r, eye).reshape(2, half * NSA_GROUPS * HEAD_DIM,
                                                         NSA_GROUPS * hid)
    pr = pos.reshape(2, half, 1, HEAD_DIM)
    pt = jnp.broadcast_to(pr, (2, half, NSA_GROUPS, HEAD_DIM)).reshape(2, -1)
    return pt, wd[0].astype(BF16), wd[1].astype(BF16), w2.astype(BF16)


def _compress(kc, vc, pos_k, pos_v, w1k, w2k, w1v, w2v):
    bn, s, kvw = kc.shape
    n16 = s // CMP_STRIDE
    row_w = CMP_STRIDE * kvw
    pk, wkt, wkb, w2kb = _compress_weights(pos_k, w1k, w2k)
    pv, wvt, wvb, w2vb = _compress_weights(pos_v, w1v, w2v)
    hid2 = wkt.shape[1]
    full = lambda shape: pl.BlockSpec(shape, lambda b: tuple(0 for _ in shape))
    out = jax.ShapeDtypeStruct((bn, NSA_GROUPS, n16, HEAD_DIM), F32)
    return pl.pallas_call(
        _compress_kernel,
        grid=(bn,),
        in_specs=[pl.BlockSpec((1, n16, row_w), lambda b: (b, 0, 0)),
                  pl.BlockSpec((1, n16, row_w), lambda b: (b, 0, 0)),
                  full((2, row_w)), full((2, row_w)),
                  full((row_w, hid2)), full((row_w, hid2)), full((row_w, hid2)), full((row_w, hid2)),
                  full(w2kb.shape), full(w2vb.shape)],
        out_specs=[pl.BlockSpec((1, NSA_GROUPS, n16, HEAD_DIM), lambda b: (b, 0, 0, 0))] * 2,
        out_shape=[out, out],
        compiler_params=pltpu.CompilerParams(dimension_semantics=("parallel",),
                                             vmem_limit_bytes=VMEM_LIMIT),
        name="nsa_compress",
    )(kc.reshape(bn, n16, row_w), vc.reshape(bn, n16, row_w), pk, pv, wkt, wkb, wvt, wvb, w2kb, w2vb)


SEL_TILE = 256
KEY_TILE = 128
WIN_TILES = WINDOW // KEY_TILE + 1
MASK_W = 64


def _nsa_kernel(q_ref, g_ref, kcmp_ref, vcmp_ref, ks_ref, vs_ref, kw_ref, vw_ref, c2s_ref,
                o_ref, kaug_ref, vst_ref, vwt_ref, vct_ref, sca_ref, scb_ref, *, top):
    qb = pl.program_id(1)
    s0 = qb * Q_BLOCK
    s = ks_ref.shape[1]
    nc = kcmp_ref.shape[2]
    d = HEAD_DIM
    nq = Q_BLOCK

    @pl.when(qb == 0)
    def _():
        blk = lax.broadcasted_iota(jnp.int32, (s, MASK_W), 0) // SLC_BLOCK
        col = lax.broadcasted_iota(jnp.int32, (s, MASK_W), 1)
        onehot = jnp.where(blk == col, 1.0, 0.0).astype(BF16)
        for g in range(NSA_GROUPS):
            kaug_ref[g, :, 0:d] = ks_ref[0, :, g * d:(g + 1) * d]
            kaug_ref[g, :, d:d + MASK_W] = onehot
            vct_ref[g] = vcmp_ref[0, g].T.astype(BF16)

        def transpose_tile(j, carry):
            r0 = pl.multiple_of(j * KEY_TILE, KEY_TILE)
            vst_ref[j] = vs_ref[0, pl.ds(r0, KEY_TILE), :].astype(F32).T.astype(BF16)
            vwt_ref[j] = vw_ref[0, pl.ds(r0, KEY_TILE), :].astype(F32).T.astype(BF16)
            return carry

        lax.fori_loop(0, s // KEY_TILE, transpose_tile, 0)

    q_all = q_ref[0]
    gates_t = jax.nn.sigmoid(g_ref[0]).T
    lane_minus_sub = (lax.broadcasted_iota(jnp.int32, (KEY_TILE, nq), 1)
                      - lax.broadcasted_iota(jnp.int32, (KEY_TILE, nq), 0))
    w_start = jnp.maximum(s0 - WINDOW, 0)
    w_tile0 = w_start // KEY_TILE

    for g in range(NSA_GROUPS):
        gd = slice(g * d, (g + 1) * d)
        q_h = [q_all[:, (g * NSA_HPG + r) * d:(g * NSA_HPG + r + 1) * d] for r in range(NSA_HPG)]
        qg = jnp.concatenate(q_h, axis=0)

        kc = kcmp_ref[0, g].astype(BF16)
        st = _dot_nt(kc, qg)
        n_row = lax.broadcasted_iota(jnp.int32, (nc, NSA_HPG * nq), 0)
        t_col = s0 + lax.broadcasted_iota(jnp.int32, (nc, NSA_HPG * nq), 1) % nq
        valid_t = (CMP_STRIDE * n_row + CMP_BLOCK - 1) <= t_col
        stm = jnp.where(valid_t, st, NEG_BIG)
        et = jnp.exp(stm - jnp.max(stm, axis=0, keepdims=True))
        pt = jnp.where(valid_t, et / jnp.sum(et, axis=0, keepdims=True), 0.0)
        o_ct = jnp.dot(vct_ref[g], pt.astype(BF16), preferred_element_type=F32)

        psum = pt[:, 0:nq]
        for r in range(1, NSA_HPG):
            psum = psum + pt[:, r * nq:(r + 1) * nq]
        imp = _hdot(c2s_ref[...], psum)
        j_idx = lax.broadcasted_iota(jnp.int32, (MASK_W, nq), 0)
        blk_t = (s0 + lax.broadcasted_iota(jnp.int32, (MASK_W, nq), 1)) // SLC_BLOCK
        causal = j_idx <= blk_t
        forced = (j_idx == 0) | (j_idx == blk_t)
        imp = jnp.where(forced, jnp.inf, jnp.where(causal, imp, -jnp.inf))
        cnt = jnp.zeros((MASK_W, nq), jnp.int32)
        for i in range(s // SLC_BLOCK):
            row = imp[i:i + 1, :]
            tie = jnp.where(j_idx > i, 1, 0)
            cnt = cnt + jnp.where(row > imp, 1, jnp.where(row == imp, tie, 0))
        sel_t = jnp.where(causal, jnp.where(cnt < top, 0.0, -MASK_BIG), -MASK_BIG)
        mterm = sel_t.T.astype(BF16)
        q_aug = [jnp.concatenate([q_h[r], mterm], axis=1) for r in range(NSA_HPG)]

        def sel_scores(i, dst_ref):
            kt = kaug_ref[g, pl.ds(pl.multiple_of(i * SEL_TILE, SEL_TILE), SEL_TILE), :]
            for r in range(NSA_HPG):
                dst_ref[r] = _dot_nt(kt, q_aug[r])

        def sel_step(i, src_ref, carry, masked):
            k0 = i * SEL_TILE
            sc = [src_ref[r] for r in range(NSA_HPG)]
            if masked:
                sc = [jnp.concatenate(
                    [jnp.where(lane_minus_sub + (s0 - k0 - j * KEY_TILE) >= 0,
                               x[j * KEY_TILE:(j + 1) * KEY_TILE], -MASK_BIG)
                     for j in range(SEL_TILE // KEY_TILE)], axis=0) for x in sc]
            heads = range(NSA_HPG)
            m_new = [jnp.maximum(carry[r][0], jnp.max(sc[r], axis=0, keepdims=True)) for r in heads]
            alpha = [jnp.exp(carry[r][0] - m_new[r]) for r in heads]
            p = [jnp.exp(sc[r] - m_new[r]) for r in heads]
            l_new = [alpha[r] * carry[r][1] + jnp.sum(p[r], axis=0, keepdims=True) for r in heads]
            pb = [p[r].astype(BF16) for r in heads]
            v_tiles = [vst_ref[i * (SEL_TILE // KEY_TILE) + j, gd, :] for j in range(SEL_TILE // KEY_TILE)]
            pv = [sum(jnp.dot(v_tiles[j], pb[r][j * KEY_TILE:(j + 1) * KEY_TILE], preferred_element_type=F32)
                      for j in range(1, SEL_TILE // KEY_TILE))
                  + jnp.dot(v_tiles[0], pb[r][0:KEY_TILE], preferred_element_type=F32) for r in heads]
            return tuple((m_new[r], l_new[r], alpha[r] * carry[r][2] + pv[r]) for r in heads)

        init = tuple((jnp.full((1, nq), M_INIT, F32), jnp.zeros((1, nq), F32), jnp.zeros((d, nq), F32))
                     for _ in range(NSA_HPG))
        n_full = s0 // SEL_TILE

        sel_scores(0, sca_ref)

        def sel_body(k, carry):
            sel_scores(2 * k + 1, scb_ref)
            carry = sel_step(2 * k, sca_ref, carry, masked=False)
            sel_scores(2 * k + 2, sca_ref)
            return sel_step(2 * k + 1, scb_ref, carry, masked=False)

        carry = lax.fori_loop(0, n_full // 2, sel_body, init)

        def odd_tail(carry):
            sel_scores(n_full, scb_ref)
            carry = sel_step(n_full - 1, sca_ref, carry, masked=False)
            sca_ref[...] = scb_ref[...]
            return carry

        carry = lax.cond(n_full % 2 == 1, odd_tail, lambda c: c, carry)
        carry = sel_step(n_full, sca_ref, carry, masked=True)
        o_st = [acc / l for (_, l, acc) in carry]

        zeros = jnp.zeros((nq, d), BF16)
        q_w = [jnp.concatenate([q_h[r], zeros] if g == 0 else [zeros, q_h[r]], axis=1)
               for r in range(NSA_HPG)]
        k_tiles = [kw_ref[0, pl.ds(pl.multiple_of(w_start + j * KEY_TILE, KEY_TILE), KEY_TILE), :]
                   for j in range(WIN_TILES)]

        def win_scores(r):
            out = []
            for j in range(WIN_TILES):
                dist = lane_minus_sub + (s0 - w_start - j * KEY_TILE)
                ok = dist.astype(jnp.uint32) < WINDOW
                out.append(jnp.where(ok, _dot_nt(k_tiles[j], q_w[r]), -MASK_BIG))
            return out

        def win_softmax(sc):
            m = sc[0]
            for x in sc[1:]:
                m = jnp.maximum(m, x)
            m = jnp.max(m, axis=0, keepdims=True)
            p = [jnp.exp(x - m) for x in sc]
            tot = p[0]
            for x in p[1:]:
                tot = tot + x
            return [x.astype(BF16) for x in p], jnp.sum(tot, axis=0, keepdims=True)

        def win_out(pl_):
            p, l = pl_
            acc = None
            for j in range(WIN_TILES):
                term = jnp.dot(vwt_ref[w_tile0 + j, gd, :], p[j], preferred_element_type=F32)
                acc = term if acc is None else acc + term
            return acc / l

        sc_w = {0: win_scores(0), 1: win_scores(1)}
        p_w = {0: win_softmax(sc_w[0])}
        sc_w[2] = win_scores(2)
        p_w[1] = win_softmax(sc_w[1])
        o_wt = {0: win_out(p_w[0])}
        sc_w[3] = win_scores(3)
        p_w[2] = win_softmax(sc_w[2])
        o_wt[1] = win_out(p_w[1])
        p_w[3] = win_softmax(sc_w[3])
        o_wt[2] = win_out(p_w[2])
        o_wt[3] = win_out(p_w[3])

        for pair in range(NSA_HPG // 2):
            halves = []
            for r in (2 * pair, 2 * pair + 1):
                hh = g * NSA_HPG + r
                halves.append(gates_t[3 * hh:3 * hh + 1, :] * o_ct[:, r * nq:(r + 1) * nq]
                              + gates_t[3 * hh + 1:3 * hh + 2, :] * o_st[r]
                              + gates_t[3 * hh + 2:3 * hh + 3, :] * o_wt[r])
            col0 = (g * NSA_HPG + 2 * pair) * d
            o_ref[0, :, col0:col0 + 2 * d] = jnp.concatenate(halves, axis=0).T


def _cmp_to_slc_t(s):
    n_cmp = s // CMP_STRIDE
    n_slc = s // SLC_BLOCK
    cs = CMP_STRIDE * np.arange(n_cmp)[:, None]
    ss = SLC_BLOCK * np.arange(n_slc)[None, :]
    overlap = np.clip(np.minimum(cs + CMP_BLOCK, ss + SLC_BLOCK) - np.maximum(cs, ss), 0, None)
    m = np.zeros((MASK_W, n_cmp), np.float32)
    m[:n_slc] = (overlap / CMP_BLOCK).T
    return jnp.asarray(m)


def _nsa(p, k_cmp, v_cmp):
    q = p["q"]
    bn, s, _ = q.shape
    assert s // SLC_BLOCK <= MASK_W and s % SEL_TILE == 0
    top = min(SLC_TOP, s // SLC_BLOCK)
    nc = k_cmp.shape[2]
    kvw = NSA_GROUPS * HEAD_DIM
    per_b = lambda w: pl.BlockSpec((1, s, w), lambda b, i: (b, 0, 0))
    return pl.pallas_call(
        functools.partial(_nsa_kernel, top=top),
        grid=(bn, s // Q_BLOCK),
        in_specs=[pl.BlockSpec((1, Q_BLOCK, NSA_HEADS * HEAD_DIM), lambda b, i: (b, i, 0)),
                  pl.BlockSpec((1, Q_BLOCK, LANES), lambda b, i: (b, i, 0)),
                  pl.BlockSpec((1, NSA_GROUPS, nc, HEAD_DIM), lambda b, i: (b, 0, 0, 0)),
                  pl.BlockSpec((1, NSA_GROUPS, nc, HEAD_DIM), lambda b, i: (b, 0, 0, 0)),
                  per_b(kvw), per_b(kvw), per_b(kvw), per_b(kvw),
                  pl.BlockSpec((MASK_W, nc), lambda b, i: (0, 0))],
        out_specs=pl.BlockSpec((1, Q_BLOCK, NSA_HEADS * HEAD_DIM), lambda b, i: (b, i, 0)),
        out_shape=jax.ShapeDtypeStruct((bn, s, NSA_HEADS * HEAD_DIM), F32),
        scratch_shapes=[pltpu.VMEM((NSA_GROUPS, s, HEAD_DIM + MASK_W), BF16),
                        pltpu.VMEM((s // KEY_TILE, kvw, KEY_TILE), BF16),
                        pltpu.VMEM((s // KEY_TILE, kvw, KEY_TILE), BF16),
                        pltpu.VMEM((NSA_GROUPS, HEAD_DIM, nc), BF16),
                        pltpu.VMEM((NSA_HPG, SEL_TILE, Q_BLOCK), F32),
                        pltpu.VMEM((NSA_HPG, SEL_TILE, Q_BLOCK), F32)],
        compiler_params=pltpu.CompilerParams(dimension_semantics=("parallel", "arbitrary"),
                                             vmem_limit_bytes=VMEM_LIMIT),
        name="nsa_attention",
    )(q, p["g"], k_cmp, v_cmp, p["ks"], p["vs"], p["kw"], p["vw"], _cmp_to_slc_t(s))


S5_STRIP = 512


def _gelu_tanh(y):
    return 0.5 * y * (1.0 + jnp.tanh(math.sqrt(2.0 / math.pi) * (y + 0.044715 * (y * y * y))))


def _s5_kernel(u_ref, bre_ref, bim_ref, are_ref, aim_ref, cre_ref, cim_ref, d_ref, gw_ref, gb_ref,
               o_ref, xre_ref, xim_ref, sre_ref, sim_ref):
    tc, bn, width = u_ref.shape
    nstate = are_ref.shape[1]

    @pl.when(pl.program_id(0) == 0)
    def _():
        sre_ref[...] = jnp.zeros_like(sre_ref)
        sim_ref[...] = jnp.zeros_like(sim_ref)

    u = u_ref[...].reshape(tc * bn, width)
    ub = u.astype(BF16)
    xre_ref[...] = jnp.dot(ub, bre_ref[...], preferred_element_type=F32)
    xim_ref[...] = jnp.dot(ub, bim_ref[...], preferred_element_type=F32)

    for c0 in range(0, nstate, S5_STRIP):
        cols = pl.ds(c0, S5_STRIP)
        a_r = jnp.broadcast_to(are_ref[:, c0:c0 + S5_STRIP], (bn, S5_STRIP))
        a_i = jnp.broadcast_to(aim_ref[:, c0:c0 + S5_STRIP], (bn, S5_STRIP))

        def step(t, carry):
            x_r, x_i = carry
            r0 = pl.multiple_of(t * bn, bn)
            n_r = a_r * x_r - a_i * x_i + xre_ref[pl.ds(r0, bn), cols]
            n_i = a_r * x_i + a_i * x_r + xim_ref[pl.ds(r0, bn), cols]
            xre_ref[pl.ds(r0, bn), cols] = n_r
            xim_ref[pl.ds(r0, bn), cols] = n_i
            return n_r, n_i

        x_r, x_i = lax.fori_loop(0, tc, step, (sre_ref[:, cols], sim_ref[:, cols]), unroll=8)
        sre_ref[:, cols] = x_r
        sim_ref[:, cols] = x_i

    y = (jnp.dot(xre_ref[...].astype(BF16), cre_ref[...], preferred_element_type=F32)
         + jnp.dot(xim_ref[...].astype(BF16), cim_ref[...], preferred_element_type=F32)
         + d_ref[...] * u)
    y = _gelu_tanh(y)
    z = _bdot(y, gw_ref[...]) + gb_ref[...]
    o_ref[...] = (y * jax.nn.sigmoid(z)).reshape(tc, bn, width)


def _s5_params(a_re, a_im, b_re, b_im, c_re, c_im, log_dt):
    g, p = a_re.shape
    c = b_re.shape[-1]
    a_re, a_im = a_re.astype(F32), a_im.astype(F32)
    b_re, b_im = b_re.astype(F32), b_im.astype(F32)
    dt = jnp.exp(log_dt.astype(F32))[:, None]
    mag = jnp.exp(a_re * dt)
    abar_re, abar_im = mag * jnp.cos(a_im * dt), mag * jnp.sin(a_im * dt)
    den = a_re * a_re + a_im * a_im
    f_re = ((abar_re - 1.0) * a_re + abar_im * a_im) / den
    f_im = (abar_im * a_re - (abar_re - 1.0) * a_im) / den
    bbar_re = f_re[..., None] * b_re - f_im[..., None] * b_im
    bbar_im = f_re[..., None] * b_im + f_im[..., None] * b_re
    eye = jnp.eye(g, dtype=F32)
    pack_b = lambda m: jnp.einsum("gpc,gh->gchp", m, eye).reshape(g * c, g * p).astype(BF16)
    pack_c = lambda m: jnp.einsum("gcp,gh->gphc", m, eye).reshape(g * p, g * c).astype(BF16)
    return (pack_b(bbar_re), pack_b(bbar_im), abar_re.reshape(1, g * p), abar_im.reshape(1, g * p),
            pack_c(c_re.astype(F32)), pack_c(-c_im.astype(F32)))


def _s5(u_t, params, d, glu_w, glu_b, tc=64):
    s, bn, width = u_t.shape
    assert bn == SUBLANES
    bre, bim, are, aim, cre, cim = params
    nstate = are.shape[1]
    full = lambda a: pl.BlockSpec(a.shape, lambda i: tuple(0 for _ in a.shape))
    d2, gb2, gwb = d.reshape(1, width), glu_b.reshape(1, width), glu_w.astype(BF16)
    return pl.pallas_call(
        _s5_kernel,
        grid=(s // tc,),
        in_specs=[pl.BlockSpec((tc, bn, width), lambda i: (i, 0, 0)),
                  full(bre), full(bim), full(are), full(aim), full(cre), full(cim),
                  full(d2), full(gwb), full(gb2)],
        out_specs=pl.BlockSpec((tc, bn, width), lambda i: (i, 0, 0)),
        out_shape=jax.ShapeDtypeStruct((s, bn, width), F32),
        scratch_shapes=[pltpu.VMEM((tc * bn, nstate), F32), pltpu.VMEM((tc * bn, nstate), F32),
                        pltpu.VMEM((bn, nstate), F32), pltpu.VMEM((bn, nstate), F32)],
        compiler_params=pltpu.CompilerParams(dimension_semantics=("arbitrary",),
                                             vmem_limit_bytes=VMEM_LIMIT),
        name="s5_scan",
    )(u_t, bre, bim, are, aim, cre, cim, d2, gwb, gb2)


RWKV_CHUNK = 64
RWKV_BLOCK = 256


def _split_dot(x, seg):
    hi = x.astype(BF16)
    lo = (x - hi.astype(F32)).astype(BF16)
    return (jnp.dot(hi, seg, preferred_element_type=F32) + jnp.dot(lo, seg, preferred_element_type=F32))


def _rwkv_kernel(mix_ref, mu_ref, w0_ref, w2_ref, a0_ref, a2_ref, kk_ref, ka_ref, rk_ref,
                 lnw_ref, lnb_ref, seg_ref, o_ref, state_ref, carry_ref, y_ref):
    n = HEAD_DIM
    chunk = RWKV_CHUNK
    tb = mix_ref.shape[1]
    w = BRANCH_W

    @pl.when(pl.program_id(1) == 0)
    def _():
        state_ref[...] = jnp.zeros_like(state_ref)
        carry_ref[...] = jnp.zeros_like(carry_ref)

    mix = mix_ref[0]
    row_m = lax.broadcasted_iota(jnp.int32, mix.shape, 0)
    prev = jnp.where(row_m == 0, carry_ref[0:1, :], pltpu.roll(mix, 1, 0))
    carry_ref[0:1, :] = mix[tb - 1:tb, :]
    xs = mix + (prev - mix) * mu_ref[...]
    r, k, v = xs[:, 0:w], xs[:, w:2 * w], xs[:, 2 * w:3 * w]
    wl, al = xs[:, 3 * w:3 * w + LORA_W], xs[:, 3 * w + LORA_W:3 * w + 2 * LORA_W]

    lw = w0_ref[...] + _bdot(jnp.tanh(wl), w2_ref[...])
    z = -lw
    softplus = jnp.maximum(z, 0.0) + jnp.log(1.0 + jnp.exp(-jnp.abs(z)))
    ld = -jnp.exp(-softplus - 0.5)
    a = jax.nn.sigmoid(a0_ref[...] + _bdot(al, a2_ref[...]))
    seg = seg_ref[...]
    kk = k * kk_ref[...]
    kk = kk / jnp.maximum(jnp.sqrt(_split_dot(kk * kk, seg)), 1e-12)
    k2 = k * (1.0 + (a - 1.0) * ka_ref[...])
    aa = -kk
    bb = kk * a

    row = lax.broadcasted_iota(jnp.int32, (tb, w), 0) % chunk
    cum = ld
    sh = 1
    while sh < chunk:
        cum = cum + jnp.where(row >= sh, pltpu.roll(cum, sh, 0), 0.0)
        sh *= 2
    at = (aa * jnp.exp(cum - ld)).astype(BF16)
    rt = (r * jnp.exp(cum)).astype(BF16)
    e_neg = jnp.exp(-cum)
    bt, kt = (bb * e_neg).astype(BF16), (k2 * e_neg).astype(BF16)
    vb = v.astype(BF16)

    ti = lax.broadcasted_iota(jnp.int32, (2 * chunk, chunk), 0)
    si = lax.broadcasted_iota(jnp.int32, (2 * chunk, chunk), 1)
    strict = si < ti
    lower2 = si < jnp.where(ti < chunk, ti, ti - chunk + 1)
    n_double = int(math.log2(chunk)) - 1

    n_chunks = tb // chunk
    pairs = [(ci, h) for ci in range(n_chunks) for h in range(RWKV_HEADS)]
    rows_of = lambda ci: slice(ci * chunk, (ci + 1) * chunk)
    lanes_of = lambda h: slice(h * n, (h + 1) * n)
    bdot32 = lambda p_, q_: jnp.dot(p_, q_, preferred_element_type=F32)

    at_l = [at[rows_of(ci), lanes_of(h)] for ci, h in pairs]
    rt_l = [rt[rows_of(ci), lanes_of(h)] for ci, h in pairs]
    v_l = [vb[rows_of(ci), lanes_of(h)] for ci, h in pairs]
    amat_l = [_dot_nt(jnp.concatenate([at_l[i], rt_l[i]], axis=0),
                      jnp.concatenate([bt[rows_of(ci), lanes_of(h)], kt[rows_of(ci), lanes_of(h)]], axis=0))
              for i, (ci, h) in enumerate(pairs)]
    pw_l = [jnp.where(strict[:chunk], m[:chunk, :chunk], 0.0).astype(BF16) for m in amat_l]
    arb_l = [jnp.where(lower2[chunk:], m[chunk:, :chunk], 0.0).astype(BF16) for m in amat_l]
    axk_l = [jnp.where(lower2, m[:, chunk:], 0.0).astype(BF16) for m in amat_l]
    xv_l = [bdot32(axk_l[i], v_l[i]) for i in range(len(pairs))]
    x_l = [jnp.concatenate([at_l[i].astype(F32), xv_l[i][:chunk]], axis=1) for i in range(len(pairs))]
    x_l = [x + bdot32(pw, x.astype(BF16)) for x, pw in zip(x_l, pw_l)]
    for _ in range(n_double):
        pw_l = [bdot32(pw, pw).astype(BF16) for pw in pw_l]
        x_l = [x + bdot32(pw, x.astype(BF16)) for x, pw in zip(x_l, pw_l)]
    ro_l = [jnp.concatenate([rt_l[i].astype(F32), xv_l[i][chunk:]], axis=1)
            + bdot32(arb_l[i], x_l[i].astype(BF16)) for i in range(len(pairs))]
    wr_l = [jnp.concatenate([x[:, :n], ro[:, :n]], axis=0).astype(BF16) for x, ro in zip(x_l, ro_l)]

    state = [state_ref[h] for h in range(RWKV_HEADS)]
    for ci in range(n_chunks):
        rows = rows_of(ci)
        cum_c = cum[rows]
        tot = cum_c[chunk - 1:chunk, :]
        e_rem = jnp.exp(tot - cum_c)
        bh, kh = (bb[rows] * e_rem).astype(BF16), (k2[rows] * e_rem).astype(BF16)
        p_tot = jnp.exp(tot)
        base = ci * RWKV_HEADS
        g_l = [_dot_nt(wr_l[base + h], state[h].astype(BF16)) for h in range(RWKV_HEADS)]
        for h in range(RWKV_HEADS):
            y_ref[rows, lanes_of(h)] = g_l[h][chunk:] + ro_l[base + h][:, n:]
        uv_l = [jnp.concatenate([g_l[h][:chunk] + x_l[base + h][:, n:], v_l[base + h].astype(F32)], axis=0)
                for h in range(RWKV_HEADS)]
        state = [state[h] * p_tot[:, lanes_of(h)]
                 + bdot32(uv_l[h].T.astype(BF16),
                          jnp.concatenate([bh[:, lanes_of(h)], kh[:, lanes_of(h)]], axis=0))
                 for h in range(RWKV_HEADS)]
    for h in range(RWKV_HEADS):
        state_ref[h] = state[h]

    y = y_ref[...]
    inv_n = 1.0 / n
    mean = _split_dot(y, seg) * inv_n
    dev = y - mean
    var = _split_dot(dev * dev, seg) * inv_n
    yn = dev * lax.rsqrt(var + RWKV_LN_EPS) * lnw_ref[...] + lnb_ref[...]
    bonus = _split_dot(r * k2 * rk_ref[...], seg) * v
    o_ref[0] = yn + bonus


def _rwkv(mix, mu, w0, w2, a0, a2, k_k, k_a, r_k, ln_w, ln_b):
    bn, s, mw = mix.shape
    w = BRANCH_W
    tb = min(RWKV_BLOCK, s)
    head = np.arange(w) // HEAD_DIM
    seg = jnp.asarray((head[:, None] == head[None, :]).astype(np.float32)).astype(BF16)
    row = lambda t: t.reshape(1, -1).astype(F32)
    args = (row(mu), row(w0), w2.astype(BF16), row(a0), a2.astype(BF16), row(k_k), row(k_a), row(r_k),
            row(ln_w), row(ln_b), seg)
    full = lambda a: pl.BlockSpec(a.shape, lambda b, i: tuple(0 for _ in a.shape))
    return pl.pallas_call(
        _rwkv_kernel,
        grid=(bn, s // tb),
        in_specs=[pl.BlockSpec((1, tb, mw), lambda b, i: (b, i, 0))] + [full(t) for t in args],
        out_specs=pl.BlockSpec((1, tb, w), lambda b, i: (b, i, 0)),
        out_shape=jax.ShapeDtypeStruct((bn, s, w), F32),
        scratch_shapes=[pltpu.VMEM((RWKV_HEADS, HEAD_DIM, HEAD_DIM), F32),
                        pltpu.VMEM((SUBLANES, mw), F32),
                        pltpu.VMEM((tb, w), F32)],
        compiler_params=pltpu.CompilerParams(dimension_semantics=("parallel", "arbitrary"),
                                             vmem_limit_bytes=VMEM_LIMIT),
        name="rwkv7",
    )(mix, *args)


def _merge_kernel(on_ref, os_ref, or_ref, gn_ref, gs_ref, gr_ref, mg_ref, x_ref, gate_ref,
                  wup_ref, wout_ref, fnw_ref, o_ref, *, final):
    d = x_ref.shape[2]
    merged = None
    for i, (b_ref, g_ref) in enumerate(((on_ref, gn_ref), (os_ref, gs_ref), (or_ref, gr_ref))):
        branch = b_ref[0] * _silu(g_ref[0])
        up = _bdot(branch, wup_ref[i])
        term = jax.nn.sigmoid(mg_ref[0, :, i * d:(i + 1) * d]) * up
        merged = term if merged is None else merged + term
    out = x_ref[0] + gate_ref[0] * _bdot(merged, wout_ref[...])
    if final:
        ms = jnp.mean(out * out, axis=-1, keepdims=True)
        out = out * lax.rsqrt(ms + NORM_EPS) * fnw_ref[...]
    o_ref[0] = out


def _merge(o_nsa, o_s5, o_rwkv, p, x, gate, w_up, w_out, fnw, final, tm=256):
    bn, s, d = x.shape
    w = BRANCH_W
    rows = lambda width: pl.BlockSpec((1, tm, width), lambda b, i: (b, i, 0))
    wupb, woutb = w_up.astype(BF16), w_out.astype(BF16)
    return pl.pallas_call(
        functools.partial(_merge_kernel, final=final),
        grid=(bn, s // tm),
        in_specs=[rows(w)] * 6 + [rows(3 * d), rows(d),
                                  pl.BlockSpec((1, 1, d), lambda b, i: (b, 0, 0)),
                                  pl.BlockSpec(wupb.shape, lambda b, i: (0, 0, 0)),
                                  pl.BlockSpec(woutb.shape, lambda b, i: (0, 0)),
                                  pl.BlockSpec((1, d), lambda b, i: (0, 0))],
        out_specs=rows(d),
        out_shape=jax.ShapeDtypeStruct((bn, s, d), F32),
        compiler_params=pltpu.CompilerParams(dimension_semantics=("parallel", "parallel"),
                                             vmem_limit_bytes=VMEM_LIMIT),
        name="merge_out",
    )(o_nsa, o_s5, o_rwkv, p["ng"], p["sg"], p["rg"], p["mg"], x, gate.reshape(bn, 1, d),
      wupb, woutb, fnw.reshape(1, d))


def kernel(x, c, norm_w, mod_w, mod_b, w_in, cmp_pos_k, cmp_pos_v, cmp_w1_k, cmp_w2_k, cmp_w1_v, cmp_w2_v, s5_a_re, s5_a_im, s5_b_re, s5_b_im, s5_c_re, s5_c_im, s5_d, s5_log_dt, s5_glu_w, s5_glu_b, rwkv_mu, rwkv_w0, rwkv_w2, rwkv_a0, rwkv_a2, rwkv_k_k, rwkv_k_a, rwkv_r_k, rwkv_ln_w, rwkv_ln_b, w_up, w_out, final_norm_w):
    bn, s, d = x.shape
    depth = norm_w.shape[0]
    cos_t, sin_t = _rope_tables(s)
    mod = _modulation(c, mod_w, mod_b)
    for l in range(depth):
        shift, scale, gate = mod[l, :, 0:d], mod[l, :, d:2 * d], mod[l, :, 2 * d:3 * d]
        p = _inproj(x, norm_w[l], scale, shift, cos_t, sin_t, _pack_w_in(w_in[l]))
        k_cmp, v_cmp = _compress(p["kc"], p["vc"], cmp_pos_k[l], cmp_pos_v[l], cmp_w1_k[l], cmp_w2_k[l],
                                 cmp_w1_v[l], cmp_w2_v[l])
        o_nsa = _nsa(p, k_cmp, v_cmp)
        s5p = _s5_params(s5_a_re[l], s5_a_im[l], s5_b_re[l], s5_b_im[l], s5_c_re[l], s5_c_im[l],
                         s5_log_dt[l])
        o_s5 = _s5(jnp.swapaxes(p["su"], 0, 1), s5p, s5_d[l], s5_glu_w[l], s5_glu_b[l])
        o_s5 = jnp.swapaxes(o_s5, 0, 1)
        o_rwkv = _rwkv(p["mix"], rwkv_mu[l], rwkv_w0[l], rwkv_w2[l], rwkv_a0[l], rwkv_a2[l],
                       rwkv_k_k[l], rwkv_k_a[l], rwkv_r_k[l], rwkv_ln_w[l], rwkv_ln_b[l])
        x = _merge(o_nsa, o_s5, o_rwkv, p, x, gate, w_up[l], w_out[l], final_norm_w,
                   final=(l == depth - 1))
    return x
```

```python
import functools
import math

import numpy as np
import jax
import jax.numpy as jnp
from jax import lax
from jax.experimental import pallas as pl
from jax.experimental.pallas import tpu as pltpu

F32 = jnp.float32
BF16 = jnp.bfloat16
HIGHEST = lax.Precision.HIGHEST

HEAD_DIM = 64
NSA_HEADS = 8
NSA_GROUPS = 2
NSA_HPG = NSA_HEADS // NSA_GROUPS
CMP_BLOCK = 32
CMP_STRIDE = 16
SLC_BLOCK = 64
SLC_TOP = 16
WINDOW = 512
Q_BLOCK = 128
S5_GROUPS = 32
S5_GROUP_CH = 16
S5_STATE = 64
RWKV_HEADS = 8
BRANCH_W = 512
LORA_W = 64
RWKV_MIX_W = 3 * BRANCH_W + 2 * LORA_W
ROPE_THETA = 10000.0
NORM_EPS = 1e-6
RWKV_LN_EPS = 64e-5
NEG_BIG = -1e30
MASK_BIG = 2.0 ** 100
M_INIT = -3.0e38

VMEM_LIMIT = 56 * 1024 * 1024
LANES = 128
SUBLANES = 8

C_Q = 0
C_KC, C_KS, C_KW, C_VC, C_VS, C_VW = 512, 640, 768, 896, 1024, 1152
C_G = 1280
C_NG = 1408
C_SU = 1920
C_SG = 2432
C_MIX = 2944
C_RG = C_MIX + RWKV_MIX_W
C_MG = C_RG + BRANCH_W
IN_PACKED = C_MG + 3 * 1024


def _silu(z):
    return z * jax.nn.sigmoid(z)


def _bdot(a, b):
    return jnp.dot(a.astype(BF16), b.astype(BF16), preferred_element_type=F32)


def _dot_nt(a, b, precision=None):
    return lax.dot_general(a, b, (((1,), (1,)), ((), ())), precision=precision,
                           preferred_element_type=F32)


def _hdot(a, b):
    return jnp.dot(a, b, precision=HIGHEST, preferred_element_type=F32)


def _mod_kernel(c_ref, w_ref, b_ref, o_ref):
    cond = _silu(c_ref[...])
    o_ref[0] = _bdot(cond, w_ref[0]) + b_ref[0]


def _modulation(c, mod_w, mod_b):
    depth, d, d3 = mod_w.shape
    bn = c.shape[0]
    nj = d3 // d
    return pl.pallas_call(
        _mod_kernel,
        grid=(depth, nj),
        in_specs=[pl.BlockSpec((bn, d), lambda l, j: (0, 0)),
                  pl.BlockSpec((1, d, d), lambda l, j: (l, 0, j)),
                  pl.BlockSpec((1, 1, d), lambda l, j: (l, 0, j))],
        out_specs=pl.BlockSpec((1, bn, d), lambda l, j: (l, 0, j)),
        out_shape=jax.ShapeDtypeStruct((depth, bn, d3), F32),
        name="adaln_mod",
    )(c, mod_w, mod_b.reshape(depth, 1, d3))


_INPROJ_OUTS = (
    ("q", C_Q, 512, BF16), ("kc", C_KC, 128, F32), ("ks", C_KS, 128, BF16), ("kw", C_KW, 128, BF16),
    ("vc", C_VC, 128, F32), ("vs", C_VS, 128, BF16), ("vw", C_VW, 128, BF16), ("g", C_G, 128, F32),
    ("ng", C_NG, 512, BF16), ("su", C_SU, 512, BF16), ("sg", C_SG, 512, BF16),
    ("mix", C_MIX, RWKV_MIX_W, F32), ("rg", C_RG, 512, BF16), ("mg", C_MG, 3072, BF16))
_ROPED = ("q", "kc", "ks", "kw")


def _inproj_kernel(x_ref, nw_ref, sc_ref, sh_ref, cos_ref, sin_ref, w_ref, *out_refs):
    x = x_ref[0]
    tm = x.shape[0]
    ms = jnp.mean(x * x, axis=-1, keepdims=True)
    h = x * lax.rsqrt(ms + NORM_EPS) * nw_ref[...]
    h = h * (1.0 + sc_ref[0]) + sh_ref[0]
    hb = h.astype(BF16)
    cos = cos_ref[...]
    sin = sin_ref[...]
    lane = lax.broadcasted_iota(jnp.int32, (tm, LANES), 1)
    first_half = (lane % HEAD_DIM) < (HEAD_DIM // 2)

    def rope(t):
        partner = jnp.where(first_half, pltpu.roll(t, 96, 1), pltpu.roll(t, 32, 1))
        return t * cos + partner * sin

    for (name, c0, width, dt), o_ref in zip(_INPROJ_OUTS, out_refs):
        step = min(width, 512)
        for j0 in range(0, width, step):
            w = min(step, width - j0)
            y = jnp.dot(hb, w_ref[:, c0 + j0:c0 + j0 + w], preferred_element_type=F32)
            if name in _ROPED:
                y = jnp.concatenate([rope(y[:, k:k + LANES]) for k in range(0, w, LANES)], axis=1)
            if name == "q":
                y = y * (HEAD_DIM ** -0.5)
            o_ref[0, :, j0:j0 + w] = y.astype(dt)


def _inproj(x, nw, scale, shift, cos_t, sin_t, w_packed, tm=512):
    bn, s, d = x.shape
    out_shape = [jax.ShapeDtypeStruct((bn, s, w), dt) for (_, _, w, dt) in _INPROJ_OUTS]
    out_specs = [pl.BlockSpec((1, tm, w), lambda b, i: (b, i, 0)) for (_, _, w, _) in _INPROJ_OUTS]
    outs = pl.pallas_call(
        _inproj_kernel,
        grid=(bn, s // tm),
        in_specs=[pl.BlockSpec((1, tm, d), lambda b, i: (b, i, 0)),
                  pl.BlockSpec((1, d), lambda b, i: (0, 0)),
                  pl.BlockSpec((1, 1, d), lambda b, i: (b, 0, 0)),
                  pl.BlockSpec((1, 1, d), lambda b, i: (b, 0, 0)),
                  pl.BlockSpec((tm, LANES), lambda b, i: (i, 0)),
                  pl.BlockSpec((tm, LANES), lambda b, i: (i, 0)),
                  pl.BlockSpec((d, IN_PACKED), lambda b, i: (0, 0), pipeline_mode=pl.Buffered(1))],
        out_specs=out_specs,
        out_shape=out_shape,
        compiler_params=pltpu.CompilerParams(dimension_semantics=("parallel", "parallel"),
                                             vmem_limit_bytes=VMEM_LIMIT),
        name="inproj",
    )(x, nw.reshape(1, d), scale.reshape(bn, 1, d), shift.reshape(bn, 1, d), cos_t, sin_t, w_packed)
    return dict(zip([o[0] for o in _INPROJ_OUTS], outs))


def _pack_w_in(w_in):
    d = w_in.shape[0]
    sizes = (512, 768, 24, 512, 512, 512, RWKV_MIX_W, 512, 3072)
    offs = np.concatenate([[0], np.cumsum(sizes)])
    q, kv, g, ng, su, sg, mix, rg, mg = [w_in[:, offs[i]:offs[i + 1]] for i in range(len(sizes))]
    kc, vc, ks, vs, kw, vw = [kv[:, i * 128:(i + 1) * 128] for i in range(6)]
    gpad = jnp.pad(g, ((0, 0), (0, 128 - 24)))
    return jnp.concatenate([q, kc, ks, kw, vc, vs, vw, gpad, ng, su, sg, mix, rg, mg], axis=1).astype(BF16)


def _rope_tables(s):
    half = HEAD_DIM // 2
    inv = jnp.exp(-math.log(ROPE_THETA) * jnp.arange(half, dtype=F32) / half)
    ang = jnp.arange(s, dtype=F32)[:, None] * inv[None, :]
    cos, sin = jnp.cos(ang), jnp.sin(ang)
    cos_t = jnp.tile(cos, (1, LANES // half))
    sin_t = jnp.tile(jnp.concatenate([-sin, sin], axis=1), (1, LANES // HEAD_DIM))
    return cos_t, sin_t


def _compress_kernel(kc_ref, vc_ref, pk_ref, pv_ref, wkt_ref, wkb_ref, wvt_ref, wvb_ref,
                     w2k_ref, w2v_ref, ko_ref, vo_ref):
    def one(x_ref, p_ref, wt_ref, wb_ref, w2_ref, o_ref):
        x = x_ref[0]
        n = x.shape[0]
        top = _bdot(x + p_ref[0:1, :], wt_ref[...])
        bot = _bdot(x + p_ref[1:2, :], wb_ref[...])
        hid = top + pltpu.roll(bot, n - 1, 0)
        act = _silu(hid)
        hw = act.shape[1] // NSA_GROUPS
        for g in range(NSA_GROUPS):
            o_ref[0, g] = _bdot(act[:, g * hw:(g + 1) * hw], w2_ref[...])

    one(kc_ref, pk_ref, wkt_ref, wkb_ref, w2k_ref, ko_ref)
    one(vc_ref, pv_ref, wvt_ref, wvb_ref, w2v_ref, vo_ref)


def _compress_weights(pos, w1, w2):
    hid = w1.shape[1]
    half = CMP_BLOCK // 2
    w1r = w1.reshape(2, half, HEAD_DIM, hid)
    eye = jnp.eye(NSA_GROUPS, dtype=w1.dtype)
    wd = jnp.einsum("tldj,gh->tlgdhj", w1r, eye).reshape(2, half * NSA_GROUPS * HEAD_DIM,
                                                         NSA_GROUPS * hid)
    pr = pos.reshape(2, half, 1, HEAD_DIM)
    pt = jnp.broadcast_to(pr, (2, half, NSA_GROUPS, HEAD_DIM)).reshape(2, -1)
    return pt, wd[0].astype(BF16), wd[1].astype(BF16), w2.astype(BF16)


def _compress(kc, vc, pos_k, pos_v, w1k, w2k, w1v, w2v):
    bn, s, kvw = kc.shape
    n16 = s // CMP_STRIDE
    row_w = CMP_STRIDE * kvw
    pk, wkt, wkb, w2kb = _compress_weights(pos_k, w1k, w2k)
    pv, wvt, wvb, w2vb = _compress_weights(pos_v, w1v, w2v)
    hid2 = wkt.shape[1]
    full = lambda shape: pl.BlockSpec(shape, lambda b: tuple(0 for _ in shape))
    out = jax.ShapeDtypeStruct((bn, NSA_GROUPS, n16, HEAD_DIM), F32)
    return pl.pallas_call(
        _compress_kernel,
        grid=(bn,),
        in_specs=[pl.BlockSpec((1, n16, row_w), lambda b: (b, 0, 0)),
                  pl.BlockSpec((1, n16, row_w), lambda b: (b, 0, 0)),
                  full((2, row_w)), full((2, row_w)),
                  full((row_w, hid2)), full((row_w, hid2)), full((row_w, hid2)), full((row_w, hid2)),
                  full(w2kb.shape), full(w2vb.shape)],
        out_specs=[pl.BlockSpec((1, NSA_GROUPS, n16, HEAD_DIM), lambda b: (b, 0, 0, 0))] * 2,
        out_shape=[out, out],
        compiler_params=pltpu.CompilerParams(dimension_semantics=("parallel",),
                                             vmem_limit_bytes=VMEM_LIMIT),
        name="nsa_compress",
    )(kc.reshape(bn, n16, row_w), vc.reshape(bn, n16, row_w), pk, pv, wkt, wkb, wvt, wvb, w2kb, w2vb)


SEL_TILE = 256
KEY_TILE = 128
WIN_TILES = WINDOW // KEY_TILE + 1
MASK_W = 64


def _nsa_kernel(q_ref, g_ref, kcmp_ref, vcmp_ref, ks_ref, vs_ref, kw_ref, vw_ref, c2s_ref,
                o_ref, kaug_ref, vst_ref, vwt_ref, vct_ref, sca_ref, scb_ref, imp_ref, *, top):
    qb = pl.program_id(1)
    s0 = qb * Q_BLOCK
    s = ks_ref.shape[1]
    nc = kcmp_ref.shape[2]
    d = HEAD_DIM
    nq = Q_BLOCK

    @pl.when(qb == 0)
    def _():
        blk = lax.broadcasted_iota(jnp.int32, (s, MASK_W), 0) // SLC_BLOCK
        col = lax.broadcasted_iota(jnp.int32, (s, MASK_W), 1)
        onehot = jnp.where(blk == col, 1.0, 0.0).astype(BF16)
        for g in range(NSA_GROUPS):
            kaug_ref[g, :, 0:d] = ks_ref[0, :, g * d:(g + 1) * d]
            kaug_ref[g, :, d:d + MASK_W] = onehot
            vct_ref[g] = vcmp_ref[0, g].T.astype(BF16)

        def transpose_tile(j, carry):
            r0 = pl.multiple_of(j * KEY_TILE, KEY_TILE)
            vst_ref[j] = vs_ref[0, pl.ds(r0, KEY_TILE), :].astype(F32).T.astype(BF16)
            vwt_ref[j] = vw_ref[0, pl.ds(r0, KEY_TILE), :].astype(F32).T.astype(BF16)
            return carry

        lax.fori_loop(0, s // KEY_TILE, transpose_tile, 0)

    q_all = q_ref[0]
    gates_t = jax.nn.sigmoid(g_ref[0]).T
    lane_minus_sub = (lax.broadcasted_iota(jnp.int32, (KEY_TILE, nq), 1)
                      - lax.broadcasted_iota(jnp.int32, (KEY_TILE, nq), 0))
    w_start = jnp.maximum(s0 - WINDOW, 0)
    w_tile0 = w_start // KEY_TILE

    for g in range(NSA_GROUPS):
        gd = slice(g * d, (g + 1) * d)
        q_h = [q_all[:, (g * NSA_HPG + r) * d:(g * NSA_HPG + r + 1) * d] for r in range(NSA_HPG)]
        qg = jnp.concatenate(q_h, axis=0)

        kc = kcmp_ref[0, g].astype(BF16)
        st = _dot_nt(kc, qg)
        n_row = lax.broadcasted_iota(jnp.int32, (nc, NSA_HPG * nq), 0)
        t_col = s0 + lax.broadcasted_iota(jnp.int32, (nc, NSA_HPG * nq), 1) % nq
        valid_t = (CMP_STRIDE * n_row + CMP_BLOCK - 1) <= t_col
        stm = jnp.where(valid_t, st, NEG_BIG)
        et = jnp.exp(stm - jnp.max(stm, axis=0, keepdims=True))
        pt = jnp.where(valid_t, et / jnp.sum(et, axis=0, keepdims=True), 0.0)
        o_ct = jnp.dot(vct_ref[g], pt.astype(BF16), preferred_element_type=F32)

        psum = pt[:, 0:nq]
        for r in range(1, NSA_HPG):
            psum = psum + pt[:, r * nq:(r + 1) * nq]
        imp = _hdot(c2s_ref[...], psum)
        j_idx = lax.broadcasted_iota(jnp.int32, (MASK_W, nq), 0)
        blk_t = (s0 + lax.broadcasted_iota(jnp.int32, (MASK_W, nq), 1)) // SLC_BLOCK
        causal = j_idx <= blk_t
        forced = (j_idx == 0) | (j_idx == blk_t)
        imp = jnp.where(forced, jnp.inf, jnp.where(causal, imp, -jnp.inf))
        imp_ref[...] = imp

        def rank_body(i2, cnt, imp=imp):
            for i in (2 * i2, 2 * i2 + 1):
                row = imp_ref[pl.ds(i, 1), :]
                tie = jnp.where(j_idx > i, 1, 0)
                cnt = cnt + jnp.where(row > imp, 1, jnp.where(row == imp, tie, 0))
            return cnt

        n_seen = (s0 + nq - 1) // SLC_BLOCK + 1
        cnt = lax.fori_loop(0, n_seen // 2, rank_body, jnp.zeros((MASK_W, nq), jnp.int32))
        sel_t = jnp.where(causal, jnp.where(cnt < top, 0.0, -MASK_BIG), -MASK_BIG)
        mterm = sel_t.T.astype(BF16)
        q_aug = [jnp.concatenate([q_h[r], mterm], axis=1) for r in range(NSA_HPG)]

        def sel_scores(i, dst_ref):
            kt = kaug_ref[g, pl.ds(pl.multiple_of(i * SEL_TILE, SEL_TILE), SEL_TILE), :]
            for r in range(NSA_HPG):
                dst_ref[r] = _dot_nt(kt, q_aug[r])

        def sel_step(i, src_ref, carry, masked):
            k0 = i * SEL_TILE
            sc = [src_ref[r] for r in range(NSA_HPG)]
            if masked:
                sc = [jnp.concatenate(
                    [jnp.where(lane_minus_sub + (s0 - k0 - j * KEY_TILE) >= 0,
                               x[j * KEY_TILE:(j + 1) * KEY_TILE], -MASK_BIG)
                     for j in range(SEL_TILE // KEY_TILE)], axis=0) for x in sc]
            heads = range(NSA_HPG)
            m_new = [jnp.maximum(carry[r][0], jnp.max(sc[r], axis=0, keepdims=True)) for r in heads]
            alpha = [jnp.exp(carry[r][0] - m_new[r]) for r in heads]
            p = [jnp.exp(sc[r] - m_new[r]) for r in heads]
            l_new = [alpha[r] * carry[r][1] + jnp.sum(p[r], axis=0, keepdims=True) for r in heads]
            pb = [p[r].astype(BF16) for r in heads]
            v_tiles = [vst_ref[i * (SEL_TILE // KEY_TILE) + j, gd, :] for j in range(SEL_TILE // KEY_TILE)]
            pv = [sum(jnp.dot(v_tiles[j], pb[r][j * KEY_TILE:(j + 1) * KEY_TILE], preferred_element_type=F32)
                      for j in range(1, SEL_TILE // KEY_TILE))
                  + jnp.dot(v_tiles[0], pb[r][0:KEY_TILE], preferred_element_type=F32) for r in heads]
            return tuple((m_new[r], l_new[r], alpha[r] * carry[r][2] + pv[r]) for r in heads)

        init = tuple((jnp.full((1, nq), M_INIT, F32), jnp.zeros((1, nq), F32), jnp.zeros((d, nq), F32))
                     for _ in range(NSA_HPG))
        n_full = s0 // SEL_TILE

        sel_scores(0, sca_ref)

        def sel_body(k, carry):
            sel_scores(2 * k + 1, scb_ref)
            carry = sel_step(2 * k, sca_ref, carry, masked=False)
            sel_scores(2 * k + 2, sca_ref)
            return sel_step(2 * k + 1, scb_ref, carry, masked=False)

        carry = lax.fori_loop(0, n_full // 2, sel_body, init)

        def odd_tail(carry):
            sel_scores(n_full, scb_ref)
            carry = sel_step(n_full - 1, sca_ref, carry, masked=False)
            sca_ref[...] = scb_ref[...]
            return carry

        carry = lax.cond(n_full % 2 == 1, odd_tail, lambda c: c, carry)
        carry = sel_step(n_full, sca_ref, carry, masked=True)
        o_st = [acc / l for (_, l, acc) in carry]

        zeros = jnp.zeros((nq, d), BF16)
        q_w = [jnp.concatenate([q_h[r], zeros] if g == 0 else [zeros, q_h[r]], axis=1)
               for r in range(NSA_HPG)]
        k_tiles = [kw_ref[0, pl.ds(pl.multiple_of(w_start + j * KEY_TILE, KEY_TILE), KEY_TILE), :]
                   for j in range(WIN_TILES)]

        def win_scores(r):
            out = []
            for j in range(WIN_TILES):
                dist = lane_minus_sub + (s0 - w_start - j * KEY_TILE)
                ok = dist.astype(jnp.uint32) < WINDOW
                out.append(jnp.where(ok, _dot_nt(k_tiles[j], q_w[r]), -MASK_BIG))
            return out

        def win_softmax(sc):
            m = sc[0]
            for x in sc[1:]:
                m = jnp.maximum(m, x)
            m = jnp.max(m, axis=0, keepdims=True)
            p = [jnp.exp(x - m) for x in sc]
            tot = p[0]
            for x in p[1:]:
                tot = tot + x
            return [x.astype(BF16) for x in p], jnp.sum(tot, axis=0, keepdims=True)

        def win_out(pl_):
            p, l = pl_
            acc = None
            for j in range(WIN_TILES):
                term = jnp.dot(vwt_ref[w_tile0 + j, gd, :], p[j], preferred_element_type=F32)
                acc = term if acc is None else acc + term
            return acc / l

        sc_w = {0: win_scores(0), 1: win_scores(1)}
        p_w = {0: win_softmax(sc_w[0])}
        sc_w[2] = win_scores(2)
        p_w[1] = win_softmax(sc_w[1])
        o_wt = {0: win_out(p_w[0])}
        sc_w[3] = win_scores(3)
        p_w[2] = win_softmax(sc_w[2])
        o_wt[1] = win_out(p_w[1])
        p_w[3] = win_softmax(sc_w[3])
        o_wt[2] = win_out(p_w[2])
        o_wt[3] = win_out(p_w[3])

        for pair in range(NSA_HPG // 2):
            halves = []
            for r in (2 * pair, 2 * pair + 1):
                hh = g * NSA_HPG + r
                halves.append(gates_t[3 * hh:3 * hh + 1, :] * o_ct[:, r * nq:(r + 1) * nq]
                              + gates_t[3 * hh + 1:3 * hh + 2, :] * o_st[r]
                              + gates_t[3 * hh + 2:3 * hh + 3, :] * o_wt[r])
            col0 = (g * NSA_HPG + 2 * pair) * d
            o_ref[0, :, col0:col0 + 2 * d] = jnp.concatenate(halves, axis=0).T.astype(o_ref.dtype)


def _cmp_to_slc_t(s):
    n_cmp = s // CMP_STRIDE
    n_slc = s // SLC_BLOCK
    cs = CMP_STRIDE * np.arange(n_cmp)[:, None]
    ss = SLC_BLOCK * np.arange(n_slc)[None, :]
    overlap = np.clip(np.minimum(cs + CMP_BLOCK, ss + SLC_BLOCK) - np.maximum(cs, ss), 0, None)
    m = np.zeros((MASK_W, n_cmp), np.float32)
    m[:n_slc] = (overlap / CMP_BLOCK).T
    return jnp.asarray(m)


def _nsa(p, k_cmp, v_cmp):
    q = p["q"]
    bn, s, _ = q.shape
    assert s // SLC_BLOCK <= MASK_W and s % SEL_TILE == 0
    top = min(SLC_TOP, s // SLC_BLOCK)
    nc = k_cmp.shape[2]
    kvw = NSA_GROUPS * HEAD_DIM
    per_b = lambda w: pl.BlockSpec((1, s, w), lambda b, i: (b, 0, 0))
    return pl.pallas_call(
        functools.partial(_nsa_kernel, top=top),
        grid=(bn, s // Q_BLOCK),
        in_specs=[pl.BlockSpec((1, Q_BLOCK, NSA_HEADS * HEAD_DIM), lambda b, i: (b, i, 0)),
                  pl.BlockSpec((1, Q_BLOCK, LANES), lambda b, i: (b, i, 0)),
                  pl.BlockSpec((1, NSA_GROUPS, nc, HEAD_DIM), lambda b, i: (b, 0, 0, 0)),
                  pl.BlockSpec((1, NSA_GROUPS, nc, HEAD_DIM), lambda b, i: (b, 0, 0, 0)),
                  per_b(kvw), per_b(kvw), per_b(kvw), per_b(kvw),
                  pl.BlockSpec((MASK_W, nc), lambda b, i: (0, 0))],
        out_specs=pl.BlockSpec((1, Q_BLOCK, NSA_HEADS * HEAD_DIM), lambda b, i: (b, i, 0)),
        out_shape=jax.ShapeDtypeStruct((bn, s, NSA_HEADS * HEAD_DIM), BF16),
        scratch_shapes=[pltpu.VMEM((NSA_GROUPS, s, HEAD_DIM + MASK_W), BF16),
                        pltpu.VMEM((s // KEY_TILE, kvw, KEY_TILE), BF16),
                        pltpu.VMEM((s // KEY_TILE, kvw, KEY_TILE), BF16),
                        pltpu.VMEM((NSA_GROUPS, HEAD_DIM, nc), BF16),
                        pltpu.VMEM((NSA_HPG, SEL_TILE, Q_BLOCK), F32),
                        pltpu.VMEM((NSA_HPG, SEL_TILE, Q_BLOCK), F32),
                        pltpu.VMEM((MASK_W, Q_BLOCK), F32)],
        compiler_params=pltpu.CompilerParams(dimension_semantics=("parallel", "arbitrary"),
                                             vmem_limit_bytes=VMEM_LIMIT),
        name="nsa_attention",
    )(q, p["g"], k_cmp, v_cmp, p["ks"], p["vs"], p["kw"], p["vw"], _cmp_to_slc_t(s))


S5_STRIP = 512


def _gelu_tanh(y):
    return 0.5 * y * (1.0 + jnp.tanh(math.sqrt(2.0 / math.pi) * (y + 0.044715 * (y * y * y))))


def _s5_kernel(u_ref, bre_ref, bim_ref, are_ref, aim_ref, cre_ref, cim_ref, d_ref, gw_ref, gb_ref,
               o_ref, xre_ref, xim_ref, sre_ref, sim_ref):
    tc, bn, width = u_ref.shape
    nstate = are_ref.shape[1]

    @pl.when(pl.program_id(0) == 0)
    def _():
        sre_ref[...] = jnp.zeros_like(sre_ref)
        sim_ref[...] = jnp.zeros_like(sim_ref)

    u = u_ref[...].reshape(tc * bn, width)
    ub = u.astype(BF16)
    n_strips = nstate // S5_STRIP
    cw = width // n_strips
    for k in range(n_strips):
        cols = pl.ds(k * S5_STRIP, S5_STRIP)
        xre_ref[:, cols] = jnp.dot(ub[:, k * cw:(k + 1) * cw], bre_ref[k], preferred_element_type=F32)
        xim_ref[:, cols] = jnp.dot(ub[:, k * cw:(k + 1) * cw], bim_ref[k], preferred_element_type=F32)

    for k in range(n_strips):
        c0 = k * S5_STRIP
        cols = pl.ds(c0, S5_STRIP)
        a_r = jnp.broadcast_to(are_ref[:, c0:c0 + S5_STRIP], (bn, S5_STRIP))
        a_i = jnp.broadcast_to(aim_ref[:, c0:c0 + S5_STRIP], (bn, S5_STRIP))

        def step(t, carry):
            x_r, x_i = carry
            r0 = pl.multiple_of(t * bn, bn)
            n_r = a_r * x_r - a_i * x_i + xre_ref[pl.ds(r0, bn), cols]
            n_i = a_r * x_i + a_i * x_r + xim_ref[pl.ds(r0, bn), cols]
            xre_ref[pl.ds(r0, bn), cols] = n_r
            xim_ref[pl.ds(r0, bn), cols] = n_i
            return n_r, n_i

        x_r, x_i = lax.fori_loop(0, tc, step, (sre_ref[:, cols], sim_ref[:, cols]), unroll=8)
        sre_ref[:, cols] = x_r
        sim_ref[:, cols] = x_i

    y = jnp.concatenate(
        [jnp.dot(xre_ref[:, k * S5_STRIP:(k + 1) * S5_STRIP].astype(BF16), cre_ref[k],
                 preferred_element_type=F32)
         + jnp.dot(xim_ref[:, k * S5_STRIP:(k + 1) * S5_STRIP].astype(BF16), cim_ref[k],
                   preferred_element_type=F32) for k in range(n_strips)], axis=1) + d_ref[...] * u
    y = _gelu_tanh(y)
    z = _bdot(y, gw_ref[...]) + gb_ref[...]
    o_ref[...] = (y * jax.nn.sigmoid(z)).reshape(tc, bn, width)


def _s5_params(a_re, a_im, b_re, b_im, c_re, c_im, log_dt):
    g, p = a_re.shape
    c = b_re.shape[-1]
    a_re, a_im = a_re.astype(F32), a_im.astype(F32)
    b_re, b_im = b_re.astype(F32), b_im.astype(F32)
    dt = jnp.exp(log_dt.astype(F32))[:, None]
    mag = jnp.exp(a_re * dt)
    abar_re, abar_im = mag * jnp.cos(a_im * dt), mag * jnp.sin(a_im * dt)
    den = a_re * a_re + a_im * a_im
    f_re = ((abar_re - 1.0) * a_re + abar_im * a_im) / den
    f_im = (abar_im * a_re - (abar_re - 1.0) * a_im) / den
    bbar_re = f_re[..., None] * b_re - f_im[..., None] * b_im
    bbar_im = f_re[..., None] * b_im + f_im[..., None] * b_re
    gs = S5_STRIP // p
    nb = g // gs
    eye = jnp.eye(gs, dtype=F32)
    pack_b = lambda m: jnp.einsum("kgpc,gh->kgchp", m.reshape(nb, gs, p, c), eye).reshape(
        nb, gs * c, gs * p).astype(BF16)
    pack_c = lambda m: jnp.einsum("kgcp,gh->kgphc", m.reshape(nb, gs, c, p), eye).reshape(
        nb, gs * p, gs * c).astype(BF16)
    return (pack_b(bbar_re), pack_b(bbar_im), abar_re.reshape(1, g * p), abar_im.reshape(1, g * p),
            pack_c(c_re.astype(F32)), pack_c(-c_im.astype(F32)))


def _s5(u_t, params, d, glu_w, glu_b, tc=64):
    s, bn, width = u_t.shape
    assert bn == SUBLANES
    bre, bim, are, aim, cre, cim = params
    nstate = are.shape[1]
    full = lambda a: pl.BlockSpec(a.shape, lambda i: tuple(0 for _ in a.shape))
    d2, gb2, gwb = d.reshape(1, width), glu_b.reshape(1, width), glu_w.astype(BF16)
    return pl.pallas_call(
        _s5_kernel,
        grid=(s // tc,),
        in_specs=[pl.BlockSpec((tc, bn, width), lambda i: (i, 0, 0)),
                  full(bre), full(bim), full(are), full(aim), full(cre), full(cim),
                  full(d2), full(gwb), full(gb2)],
        out_specs=pl.BlockSpec((tc, bn, width), lambda i: (i, 0, 0)),
        out_shape=jax.ShapeDtypeStruct((s, bn, width), F32),
        scratch_shapes=[pltpu.VMEM((tc * bn, nstate), F32), pltpu.VMEM((tc * bn, nstate), F32),
                        pltpu.VMEM((bn, nstate), F32), pltpu.VMEM((bn, nstate), F32)],
        compiler_params=pltpu.CompilerParams(dimension_semantics=("arbitrary",),
                                             vmem_limit_bytes=VMEM_LIMIT),
        name="s5_scan",
    )(u_t, bre, bim, are, aim, cre, cim, d2, gwb, gb2)


RWKV_CHUNK = 64
RWKV_BLOCK = 256


def _split_dot(x, seg):
    hi = x.astype(BF16)
    lo = (x - hi.astype(F32)).astype(BF16)
    return (jnp.dot(hi, seg, preferred_element_type=F32) + jnp.dot(lo, seg, preferred_element_type=F32))


def _rwkv_kernel(mix_ref, mu_ref, w0_ref, w2_ref, a0_ref, a2_ref, kk_ref, ka_ref, rk_ref,
                 lnw_ref, lnb_ref, seg_ref, o_ref, state_ref, carry_ref, y_ref):
    n = HEAD_DIM
    chunk = RWKV_CHUNK
    tb = mix_ref.shape[1]
    w = BRANCH_W

    @pl.when(pl.program_id(1) == 0)
    def _():
        state_ref[...] = jnp.zeros_like(state_ref)
        carry_ref[...] = jnp.zeros_like(carry_ref)

    mix = mix_ref[0]
    row_m = lax.broadcasted_iota(jnp.int32, mix.shape, 0)
    prev = jnp.where(row_m == 0, carry_ref[0:1, :], pltpu.roll(mix, 1, 0))
    carry_ref[0:1, :] = mix[tb - 1:tb, :]
    xs = mix + (prev - mix) * mu_ref[...]
    r, k, v = xs[:, 0:w], xs[:, w:2 * w], xs[:, 2 * w:3 * w]
    wl, al = xs[:, 3 * w:3 * w + LORA_W], xs[:, 3 * w + LORA_W:3 * w + 2 * LORA_W]

    lw = w0_ref[...] + _bdot(jnp.tanh(wl), w2_ref[...])
    z = -lw
    softplus = jnp.maximum(z, 0.0) + jnp.log(1.0 + jnp.exp(-jnp.abs(z)))
    ld = -jnp.exp(-softplus - 0.5)
    a = jax.nn.sigmoid(a0_ref[...] + _bdot(al, a2_ref[...]))
    seg = seg_ref[...]
    kk = k * kk_ref[...]
    kk = kk / jnp.maximum(jnp.sqrt(_split_dot(kk * kk, seg)), 1e-12)
    k2 = k * (1.0 + (a - 1.0) * ka_ref[...])
    aa = -kk
    bb = kk * a

    row = lax.broadcasted_iota(jnp.int32, (tb, w), 0) % chunk
    cum = ld
    sh = 1
    while sh < chunk:
        cum = cum + jnp.where(row >= sh, pltpu.roll(cum, sh, 0), 0.0)
        sh *= 2
    at = (aa * jnp.exp(cum - ld)).astype(BF16)
    rt = (r * jnp.exp(cum)).astype(BF16)
    e_neg = jnp.exp(-cum)
    bt, kt = (bb * e_neg).astype(BF16), (k2 * e_neg).astype(BF16)
    vb = v.astype(BF16)

    ti = lax.broadcasted_iota(jnp.int32, (2 * chunk, chunk), 0)
    si = lax.broadcasted_iota(jnp.int32, (2 * chunk, chunk), 1)
    strict = si < ti
    lower2 = si < jnp.where(ti < chunk, ti, ti - chunk + 1)
    n_double = int(math.log2(chunk)) - 1

    n_chunks = tb // chunk
    pairs = [(ci, h) for ci in range(n_chunks) for h in range(RWKV_HEADS)]
    rows_of = lambda ci: slice(ci * chunk, (ci + 1) * chunk)
    lanes_of = lambda h: slice(h * n, (h + 1) * n)
    bdot32 = lambda p_, q_: jnp.dot(p_, q_, preferred_element_type=F32)

    at_l = [at[rows_of(ci), lanes_of(h)] for ci, h in pairs]
    rt_l = [rt[rows_of(ci), lanes_of(h)] for ci, h in pairs]
    v_l = [vb[rows_of(ci), lanes_of(h)] for ci, h in pairs]
    amat_l = [_dot_nt(jnp.concatenate([at_l[i], rt_l[i]], axis=0),
                      jnp.concatenate([bt[rows_of(ci), lanes_of(h)], kt[rows_of(ci), lanes_of(h)]], axis=0))
              for i, (ci, h) in enumerate(pairs)]
    pw_l = [jnp.where(strict[:chunk], m[:chunk, :chunk], 0.0).astype(BF16) for m in amat_l]
    arb_l = [jnp.where(lower2[chunk:], m[chunk:, :chunk], 0.0).astype(BF16) for m in amat_l]
    axk_l = [jnp.where(lower2, m[:, chunk:], 0.0).astype(BF16) for m in amat_l]
    xv_l = [bdot32(axk_l[i], v_l[i]) for i in range(len(pairs))]
    x_l = [jnp.concatenate([at_l[i].astype(F32), xv_l[i][:chunk]], axis=1) for i in range(len(pairs))]
    x_l = [x + bdot32(pw, x.astype(BF16)) for x, pw in zip(x_l, pw_l)]
    for _ in range(n_double):
        pw_l = [bdot32(pw, pw).astype(BF16) for pw in pw_l]
        x_l = [x + bdot32(pw, x.astype(BF16)) for x, pw in zip(x_l, pw_l)]
    ro_l = [jnp.concatenate([rt_l[i].astype(F32), xv_l[i][chunk:]], axis=1)
            + bdot32(arb_l[i], x_l[i].astype(BF16)) for i in range(len(pairs))]
    wr_l = [jnp.concatenate([x[:, :n], ro[:, :n]], axis=0).astype(BF16) for x, ro in zip(x_l, ro_l)]

    state = [state_ref[h] for h in range(RWKV_HEADS)]
    for ci in range(n_chunks):
        rows = rows_of(ci)
        cum_c = cum[rows]
        tot = cum_c[chunk - 1:chunk, :]
        e_rem = jnp.exp(tot - cum_c)
        bh, kh = (bb[rows] * e_rem).astype(BF16), (k2[rows] * e_rem).astype(BF16)
        p_tot = jnp.exp(tot)
        base = ci * RWKV_HEADS
        g_l = [_dot_nt(wr_l[base + h], state[h].astype(BF16)) for h in range(RWKV_HEADS)]
        for h in range(RWKV_HEADS):
            y_ref[rows, lanes_of(h)] = g_l[h][chunk:] + ro_l[base + h][:, n:]
        uv_l = [jnp.concatenate([g_l[h][:chunk] + x_l[base + h][:, n:], v_l[base + h].astype(F32)], axis=0)
                for h in range(RWKV_HEADS)]
        state = [state[h] * p_tot[:, lanes_of(h)]
                 + bdot32(uv_l[h].T.astype(BF16),
                          jnp.concatenate([bh[:, lanes_of(h)], kh[:, lanes_of(h)]], axis=0))
                 for h in range(RWKV_HEADS)]
    for h in range(RWKV_HEADS):
        state_ref[h] = state[h]

    y = y_ref[...]
    inv_n = 1.0 / n
    mean = _split_dot(y, seg) * inv_n
    dev = y - mean
    var = _split_dot(dev * dev, seg) * inv_n
    yn = dev * lax.rsqrt(var + RWKV_LN_EPS) * lnw_ref[...] + lnb_ref[...]
    bonus = _split_dot(r * k2 * rk_ref[...], seg) * v
    o_ref[0] = (yn + bonus).astype(o_ref.dtype)


def _rwkv(mix, mu, w0, w2, a0, a2, k_k, k_a, r_k, ln_w, ln_b):
    bn, s, mw = mix.shape
    w = BRANCH_W
    tb = min(RWKV_BLOCK, s)
    head = np.arange(w) // HEAD_DIM
    seg = jnp.asarray((head[:, None] == head[None, :]).astype(np.float32)).astype(BF16)
    row = lambda t: t.reshape(1, -1).astype(F32)
    args = (row(mu), row(w0), w2.astype(BF16), row(a0), a2.astype(BF16), row(k_k), row(k_a), row(r_k),
            row(ln_w), row(ln_b), seg)
    full = lambda a: pl.BlockSpec(a.shape, lambda b, i: tuple(0 for _ in a.shape))
    return pl.pallas_call(
        _rwkv_kernel,
        grid=(bn, s // tb),
        in_specs=[pl.BlockSpec((1, tb, mw), lambda b, i: (b, i, 0))] + [full(t) for t in args],
        out_specs=pl.BlockSpec((1, tb, w), lambda b, i: (b, i, 0)),
        out_shape=jax.ShapeDtypeStruct((bn, s, w), BF16),
        scratch_shapes=[pltpu.VMEM((RWKV_HEADS, HEAD_DIM, HEAD_DIM), F32),
                        pltpu.VMEM((SUBLANES, mw), F32),
                        pltpu.VMEM((tb, w), F32)],
        compiler_params=pltpu.CompilerParams(dimension_semantics=("parallel", "arbitrary"),
                                             vmem_limit_bytes=VMEM_LIMIT),
        name="rwkv7",
    )(mix, *args)


def _merge_kernel(on_ref, os_ref, or_ref, gn_ref, gs_ref, gr_ref, mg_ref, x_ref, gate_ref,
                  wup_ref, wout_ref, fnw_ref, o_ref, *, final):
    d = x_ref.shape[2]
    merged = None
    for i, (b_ref, g_ref) in enumerate(((on_ref, gn_ref), (os_ref, gs_ref), (or_ref, gr_ref))):
        branch = b_ref[0].astype(F32) * _silu(g_ref[0].astype(F32))
        up = _bdot(branch, wup_ref[i])
        term = jax.nn.sigmoid(mg_ref[0, :, i * d:(i + 1) * d].astype(F32)) * up
        merged = term if merged is None else merged + term
    out = x_ref[0] + gate_ref[0] * _bdot(merged, wout_ref[...])
    if final:
        ms = jnp.mean(out * out, axis=-1, keepdims=True)
        out = out * lax.rsqrt(ms + NORM_EPS) * fnw_ref[...]
    o_ref[0] = out


def _merge(o_nsa, o_s5, o_rwkv, p, x, gate, w_up, w_out, fnw, final, tm=512):
    bn, s, d = x.shape
    w = BRANCH_W
    rows = lambda width: pl.BlockSpec((1, tm, width), lambda b, i: (b, i, 0))
    wupb, woutb = w_up.astype(BF16), w_out.astype(BF16)
    return pl.pallas_call(
        functools.partial(_merge_kernel, final=final),
        grid=(bn, s // tm),
        in_specs=[rows(w)] * 6 + [rows(3 * d), rows(d),
                                  pl.BlockSpec((1, 1, d), lambda b, i: (b, 0, 0)),
                                  pl.BlockSpec(wupb.shape, lambda b, i: (0, 0, 0)),
                                  pl.BlockSpec(woutb.shape, lambda b, i: (0, 0)),
                                  pl.BlockSpec((1, d), lambda b, i: (0, 0))],
        out_specs=rows(d),
        out_shape=jax.ShapeDtypeStruct((bn, s, d), F32),
        compiler_params=pltpu.CompilerParams(dimension_semantics=("parallel", "parallel"),
                                             vmem_limit_bytes=VMEM_LIMIT),
        name="merge_out",
    )(o_nsa, o_s5, o_rwkv, p["ng"], p["sg"], p["rg"], p["mg"], x, gate.reshape(bn, 1, d),
      wupb, woutb, fnw.reshape(1, d))


def kernel(x, c, norm_w, mod_w, mod_b, w_in, cmp_pos_k, cmp_pos_v, cmp_w1_k, cmp_w2_k, cmp_w1_v, cmp_w2_v, s5_a_re, s5_a_im, s5_b_re, s5_b_im, s5_c_re, s5_c_im, s5_d, s5_log_dt, s5_glu_w, s5_glu_b, rwkv_mu, rwkv_w0, rwkv_w2, rwkv_a0, rwkv_a2, rwkv_k_k, rwkv_k_a, rwkv_r_k, rwkv_ln_w, rwkv_ln_b, w_up, w_out, final_norm_w):
    bn, s, d = x.shape
    depth = norm_w.shape[0]
    cos_t, sin_t = _rope_tables(s)
    mod = _modulation(c, mod_w, mod_b)
    for l in range(depth):
        shift, scale, gate = mod[l, :, 0:d], mod[l, :, d:2 * d], mod[l, :, 2 * d:3 * d]
        p = _inproj(x, norm_w[l], scale, shift, cos_t, sin_t, _pack_w_in(w_in[l]))
        k_cmp, v_cmp = _compress(p["kc"], p["vc"], cmp_pos_k[l], cmp_pos_v[l], cmp_w1_k[l], cmp_w2_k[l],
                                 cmp_w1_v[l], cmp_w2_v[l])
        o_nsa = _nsa(p, k_cmp, v_cmp)
        s5p = _s5_params(s5_a_re[l], s5_a_im[l], s5_b_re[l], s5_b_im[l], s5_c_re[l], s5_c_im[l],
                         s5_log_dt[l])
        o_s5 = _s5(jnp.swapaxes(p["su"], 0, 1).astype(F32), s5p, s5_d[l], s5_glu_w[l], s5_glu_b[l])
        o_s5 = jnp.swapaxes(o_s5, 0, 1).astype(BF16)
        o_rwkv = _rwkv(p["mix"], rwkv_mu[l], rwkv_w0[l], rwkv_w2[l], rwkv_a0[l], rwkv_a2[l],
                       rwkv_k_k[l], rwkv_k_a[l], rwkv_r_k[l], rwkv_ln_w[l], rwkv_ln_b[l])
        x = _merge(o_nsa, o_s5, o_rwkv, p, x, gate, w_up[l], w_out[l], final_norm_w,
                   final=(l == depth - 1))
    return x
```

```python
import functools
import math

import numpy as np
import jax
import jax.numpy as jnp
from jax import lax
from jax.experimental import pallas as pl
from jax.experimental.pallas import tpu as pltpu

F32 = jnp.float32
BF16 = jnp.bfloat16
HIGHEST = lax.Precision.HIGHEST

HEAD_DIM = 64
NSA_HEADS = 8
NSA_GROUPS = 2
NSA_HPG = NSA_HEADS // NSA_GROUPS
CMP_BLOCK = 32
CMP_STRIDE = 16
SLC_BLOCK = 64
SLC_TOP = 16
WINDOW = 512
Q_BLOCK = 128
S5_GROUPS = 32
S5_GROUP_CH = 16
S5_STATE = 64
RWKV_HEADS = 8
BRANCH_W = 512
LORA_W = 64
RWKV_MIX_W = 3 * BRANCH_W + 2 * LORA_W
ROPE_THETA = 10000.0
NORM_EPS = 1e-6
RWKV_LN_EPS = 64e-5
NEG_BIG = -1e30
MASK_BIG = 2.0 ** 100
M_INIT = -3.0e38
Q_SCALE = HEAD_DIM ** -0.5 * math.log2(math.e)

VMEM_LIMIT = 56 * 1024 * 1024
LANES = 128
SUBLANES = 8

C_Q = 0
C_KC, C_KS, C_KW, C_VC, C_VS, C_VW = 512, 640, 768, 896, 1024, 1152
C_G = 1280
C_NG = 1408
C_SU = 1920
C_SG = 2432
C_MIX = 2944
C_RG = C_MIX + RWKV_MIX_W
C_MG = C_RG + BRANCH_W
IN_PACKED = C_MG + 3 * 1024


def _silu(z):
    return z * jax.nn.sigmoid(z)


def _bdot(a, b):
    return jnp.dot(a.astype(BF16), b.astype(BF16), preferred_element_type=F32)


def _dot_nt(a, b, precision=None):
    return lax.dot_general(a, b, (((1,), (1,)), ((), ())), precision=precision,
                           preferred_element_type=F32)


def _hdot(a, b):
    return jnp.dot(a, b, precision=HIGHEST, preferred_element_type=F32)


def _mod_kernel(c_ref, w_ref, b_ref, o_ref):
    cond = _silu(c_ref[...])
    o_ref[0] = _bdot(cond, w_ref[0]) + b_ref[0]


def _modulation(c, mod_w, mod_b):
    depth, d, d3 = mod_w.shape
    bn = c.shape[0]
    nj = d3 // d
    return pl.pallas_call(
        _mod_kernel,
        grid=(depth, nj),
        in_specs=[pl.BlockSpec((bn, d), lambda l, j: (0, 0)),
                  pl.BlockSpec((1, d, d), lambda l, j: (l, 0, j)),
                  pl.BlockSpec((1, 1, d), lambda l, j: (l, 0, j))],
        out_specs=pl.BlockSpec((1, bn, d), lambda l, j: (l, 0, j)),
        out_shape=jax.ShapeDtypeStruct((depth, bn, d3), F32),
        name="adaln_mod",
    )(c, mod_w, mod_b.reshape(depth, 1, d3))


_INPROJ_OUTS = (
    ("q", C_Q, 512, BF16), ("kc", C_KC, 128, F32), ("ks", C_KS, 128, BF16), ("kw", C_KW, 128, BF16),
    ("vc", C_VC, 128, F32), ("vs", C_VS, 128, BF16), ("vw", C_VW, 128, BF16), ("g", C_G, 128, F32),
    ("ng", C_NG, 512, BF16), ("su", C_SU, 512, BF16), ("sg", C_SG, 512, BF16),
    ("mix", C_MIX, RWKV_MIX_W, F32), ("rg", C_RG, 512, BF16), ("mg", C_MG, 3072, BF16))
_ROPED = ("q", "kc", "ks", "kw")


def _inproj_kernel(x_ref, nw_ref, sc_ref, sh_ref, cos_ref, sin_ref, w_ref, *out_refs):
    x = x_ref[0]
    tm = x.shape[0]
    ms = jnp.mean(x * x, axis=-1, keepdims=True)
    h = x * lax.rsqrt(ms + NORM_EPS) * nw_ref[...]
    h = h * (1.0 + sc_ref[0]) + sh_ref[0]
    hb = h.astype(BF16)
    cos = cos_ref[...]
    sin = sin_ref[...]
    lane = lax.broadcasted_iota(jnp.int32, (tm, LANES), 1)
    first_half = (lane % HEAD_DIM) < (HEAD_DIM // 2)

    def rope(t):
        partner = jnp.where(first_half, pltpu.roll(t, 96, 1), pltpu.roll(t, 32, 1))
        return t * cos + partner * sin

    for (name, c0, width, dt), o_ref in zip(_INPROJ_OUTS, out_refs):
        step = min(width, 512)
        for j0 in range(0, width, step):
            w = min(step, width - j0)
            y = jnp.dot(hb, w_ref[:, c0 + j0:c0 + j0 + w], preferred_element_type=F32)
            if name in _ROPED:
                y = jnp.concatenate([rope(y[:, k:k + LANES]) for k in range(0, w, LANES)], axis=1)
            if name == "q":
                y = y * Q_SCALE
            o_ref[0, :, j0:j0 + w] = y.astype(dt)


def _inproj(x, nw, scale, shift, cos_t, sin_t, w_packed, tm=512):
    bn, s, d = x.shape
    out_shape = [jax.ShapeDtypeStruct((bn, s, w), dt) for (_, _, w, dt) in _INPROJ_OUTS]
    out_specs = [pl.BlockSpec((1, tm, w), lambda b, i: (b, i, 0)) for (_, _, w, _) in _INPROJ_OUTS]
    outs = pl.pallas_call(
        _inproj_kernel,
        grid=(bn, s // tm),
        in_specs=[pl.BlockSpec((1, tm, d), lambda b, i: (b, i, 0)),
                  pl.BlockSpec((1, d), lambda b, i: (0, 0)),
                  pl.BlockSpec((1, 1, d), lambda b, i: (b, 0, 0)),
                  pl.BlockSpec((1, 1, d), lambda b, i: (b, 0, 0)),
                  pl.BlockSpec((tm, LANES), lambda b, i: (i, 0)),
                  pl.BlockSpec((tm, LANES), lambda b, i: (i, 0)),
                  pl.BlockSpec((d, IN_PACKED), lambda b, i: (0, 0), pipeline_mode=pl.Buffered(1))],
        out_specs=out_specs,
        out_shape=out_shape,
        compiler_params=pltpu.CompilerParams(dimension_semantics=("parallel", "parallel"),
                                             vmem_limit_bytes=VMEM_LIMIT),
        name="inproj",
    )(x, nw.reshape(1, d), scale.reshape(bn, 1, d), shift.reshape(bn, 1, d), cos_t, sin_t, w_packed)
    return dict(zip([o[0] for o in _INPROJ_OUTS], outs))


def _pack_w_in(w_in):
    d = w_in.shape[0]
    sizes = (512, 768, 24, 512, 512, 512, RWKV_MIX_W, 512, 3072)
    offs = np.concatenate([[0], np.cumsum(sizes)])
    q, kv, g, ng, su, sg, mix, rg, mg = [w_in[:, offs[i]:offs[i + 1]] for i in range(len(sizes))]
    kc, vc, ks, vs, kw, vw = [kv[:, i * 128:(i + 1) * 128] for i in range(6)]
    gpad = jnp.pad(g, ((0, 0), (0, 128 - 24)))
    return jnp.concatenate([q, kc, ks, kw, vc, vs, vw, gpad, ng, su, sg, mix, rg, mg], axis=1).astype(BF16)


def _rope_tables(s):
    half = HEAD_DIM // 2
    inv = jnp.exp(-math.log(ROPE_THETA) * jnp.arange(half, dtype=F32) / half)
    ang = jnp.arange(s, dtype=F32)[:, None] * inv[None, :]
    cos, sin = jnp.cos(ang), jnp.sin(ang)
    cos_t = jnp.tile(cos, (1, LANES // half))
    sin_t = jnp.tile(jnp.concatenate([-sin, sin], axis=1), (1, LANES // HEAD_DIM))
    return cos_t, sin_t


def _compress_kernel(kc_ref, vc_ref, pk_ref, pv_ref, wkt_ref, wkb_ref, wvt_ref, wvb_ref,
                     w2k_ref, w2v_ref, ko_ref, vo_ref):
    def one(x_ref, p_ref, wt_ref, wb_ref, w2_ref, o_ref):
        x = x_ref[0]
        n = x.shape[0]
        top = _bdot(x + p_ref[0:1, :], wt_ref[...])
        bot = _bdot(x + p_ref[1:2, :], wb_ref[...])
        hid = top + pltpu.roll(bot, n - 1, 0)
        act = _silu(hid)
        hw = act.shape[1] // NSA_GROUPS
        for g in range(NSA_GROUPS):
            o_ref[0, g] = _bdot(act[:, g * hw:(g + 1) * hw], w2_ref[...])

    one(kc_ref, pk_ref, wkt_ref, wkb_ref, w2k_ref, ko_ref)
    one(vc_ref, pv_ref, wvt_ref, wvb_ref, w2v_ref, vo_ref)


def _compress_weights(pos, w1, w2):
    hid = w1.shape[1]
    half = CMP_BLOCK // 2
    w1r = w1.reshape(2, half, HEAD_DIM, hid)
    eye = jnp.eye(NSA_GROUPS, dtype=w1.dtype)
    wd = jnp.einsum("tldj,gh->tlgdhj", w1r, eye).reshape(2, half * NSA_GROUPS * HEAD_DIM,
                                                         NSA_GROUPS * hid)
    pr = pos.reshape(2, half, 1, HEAD_DIM)
    pt = jnp.broadcast_to(pr, (2, half, NSA_GROUPS, HEAD_DIM)).reshape(2, -1)
    return pt, wd[0].astype(BF16), wd[1].astype(BF16), w2.astype(BF16)


def _compress(kc, vc, pos_k, pos_v, w1k, w2k, w1v, w2v):
    bn, s, kvw = kc.shape
    n16 = s // CMP_STRIDE
    row_w = CMP_STRIDE * kvw
    pk, wkt, wkb, w2kb = _compress_weights(pos_k, w1k, w2k)
    pv, wvt, wvb, w2vb = _compress_weights(pos_v, w1v, w2v)
    hid2 = wkt.shape[1]
    full = lambda shape: pl.BlockSpec(shape, lambda b: tuple(0 for _ in shape))
    out = jax.ShapeDtypeStruct((bn, NSA_GROUPS, n16, HEAD_DIM), F32)
    return pl.pallas_call(
        _compress_kernel,
        grid=(bn,),
        in_specs=[pl.BlockSpec((1, n16, row_w), lambda b: (b, 0, 0)),
                  pl.BlockSpec((1, n16, row_w), lambda b: (b, 0, 0)),
                  full((2, row_w)), full((2, row_w)),
                  full((row_w, hid2)), full((row_w, hid2)), full((row_w, hid2)), full((row_w, hid2)),
                  full(w2kb.shape), full(w2vb.shape)],
        out_specs=[pl.BlockSpec((1, NSA_GROUPS, n16, HEAD_DIM), lambda b: (b, 0, 0, 0))] * 2,
        out_shape=[out, out],
        compiler_params=pltpu.CompilerParams(dimension_semantics=("parallel",),
                                             vmem_limit_bytes=VMEM_LIMIT),
        name="nsa_compress",
    )(kc.reshape(bn, n16, row_w), vc.reshape(bn, n16, row_w), pk, pv, wkt, wkb, wvt, wvb, w2kb, w2vb)


SEL_TILE = 256
KEY_TILE = 128
WIN_TILES = WINDOW // KEY_TILE + 1
MASK_W = 64


def _nsa_kernel(q_ref, g_ref, kcmp_ref, vcmp_ref, ks_ref, vs_ref, kw_ref, vw_ref, c2s_ref,
                o_ref, kaug_ref, vst_ref, vwt_ref, vct_ref, sca_ref, scb_ref, imp_ref, *, top):
    qb = pl.program_id(1)
    s0 = qb * Q_BLOCK
    s = ks_ref.shape[1]
    nc = kcmp_ref.shape[2]
    d = HEAD_DIM
    nq = Q_BLOCK

    @pl.when(qb == 0)
    def _():
        blk = lax.broadcasted_iota(jnp.int32, (s, MASK_W), 0) // SLC_BLOCK
        col = lax.broadcasted_iota(jnp.int32, (s, MASK_W), 1)
        onehot = jnp.where(blk == col, 1.0, 0.0).astype(BF16)
        for g in range(NSA_GROUPS):
            kaug_ref[g, :, 0:d] = ks_ref[0, :, g * d:(g + 1) * d]
            kaug_ref[g, :, d:d + MASK_W] = onehot
            vct_ref[g] = vcmp_ref[0, g].T.astype(BF16)

        def transpose_tile(j, carry):
            r0 = pl.multiple_of(j * KEY_TILE, KEY_TILE)
            vst_ref[j] = vs_ref[0, pl.ds(r0, KEY_TILE), :].astype(F32).T.astype(BF16)
            vwt_ref[j] = vw_ref[0, pl.ds(r0, KEY_TILE), :].astype(F32).T.astype(BF16)
            return carry

        lax.fori_loop(0, s // KEY_TILE, transpose_tile, 0)

    q_all = q_ref[0]
    gates_t = jax.nn.sigmoid(g_ref[0]).T
    lane_minus_sub = (lax.broadcasted_iota(jnp.int32, (KEY_TILE, nq), 1)
                      - lax.broadcasted_iota(jnp.int32, (KEY_TILE, nq), 0))
    w_start = jnp.maximum(s0 - WINDOW, 0)
    w_tile0 = w_start // KEY_TILE

    groups = range(NSA_GROUPS)
    heads = range(NSA_HEADS)
    grp = lambda h: h // NSA_HPG
    gd = [slice(g * d, (g + 1) * d) for g in groups]
    q_h = [q_all[:, h * d:(h + 1) * d] for h in heads]

    n_row = lax.broadcasted_iota(jnp.int32, (nc, NSA_HPG * nq), 0)
    t_col = s0 + lax.broadcasted_iota(jnp.int32, (nc, NSA_HPG * nq), 1) % nq
    valid_t = (CMP_STRIDE * n_row + CMP_BLOCK - 1) <= t_col
    j_idx = lax.broadcasted_iota(jnp.int32, (MASK_W, nq), 0)
    blk_t = (s0 + lax.broadcasted_iota(jnp.int32, (MASK_W, nq), 1)) // SLC_BLOCK
    causal = j_idx <= blk_t
    forced = (j_idx == 0) | (j_idx == blk_t)
    c2s = c2s_ref[...]
    o_ct, imp = [], []
    for g in groups:
        qg = jnp.concatenate(q_h[g * NSA_HPG:(g + 1) * NSA_HPG], axis=0)
        st = _dot_nt(kcmp_ref[0, g].astype(BF16), qg)
        stm = jnp.where(valid_t, st, NEG_BIG)
        et = jnp.exp2(stm - jnp.max(stm, axis=0, keepdims=True))
        pt = jnp.where(valid_t, et / jnp.sum(et, axis=0, keepdims=True), 0.0)
        o_ct.append(jnp.dot(vct_ref[g], pt.astype(BF16), preferred_element_type=F32))
        psum = pt[:, 0:nq]
        for r in range(1, NSA_HPG):
            psum = psum + pt[:, r * nq:(r + 1) * nq]
        p_hi = psum.astype(BF16)
        p_lo = (psum - p_hi.astype(F32)).astype(BF16)
        imp_g = (jnp.dot(c2s, p_hi, preferred_element_type=F32)
                 + jnp.dot(c2s, p_lo, preferred_element_type=F32))
        imp_g = jnp.where(forced, jnp.inf, jnp.where(causal, imp_g, -jnp.inf))
        imp_ref[g] = imp_g
        imp.append(imp_g)

    def rank_body(i2, cnts):
        cnts = list(cnts)
        for i in (2 * i2, 2 * i2 + 1):
            tie = jnp.where(j_idx > i, 1, 0)
            for g in groups:
                row = imp_ref[g, pl.ds(i, 1), :]
                cnts[g] = cnts[g] + jnp.where(row > imp[g], 1, jnp.where(row == imp[g], tie, 0))
        return tuple(cnts)

    n_seen = (s0 + nq - 1) // SLC_BLOCK + 1
    cnts = lax.fori_loop(0, n_seen // 2, rank_body,
                         tuple(jnp.zeros((MASK_W, nq), jnp.int32) for _ in groups))
    q_aug = []
    for g in groups:
        sel_t = jnp.where(causal, jnp.where(cnts[g] < top, 0.0, -MASK_BIG), -MASK_BIG)
        mterm = sel_t.T.astype(BF16)
        q_aug += [jnp.concatenate([q_h[g * NSA_HPG + r], mterm], axis=1) for r in range(NSA_HPG)]

    sub_tiles = SEL_TILE // KEY_TILE

    def sel_scores(i, dst_ref):
        for g in groups:
            kt = kaug_ref[g, pl.ds(pl.multiple_of(i * SEL_TILE, SEL_TILE), SEL_TILE), :]
            for h in range(g * NSA_HPG, (g + 1) * NSA_HPG):
                dst_ref[h] = _dot_nt(kt, q_aug[h])

    def sel_step(i, src_ref, carry, masked):
        k0 = i * SEL_TILE
        sc = [src_ref[h] for h in heads]
        if masked:
            sc = [jnp.concatenate(
                [jnp.where(lane_minus_sub + (s0 - k0 - j * KEY_TILE) >= 0,
                           x[j * KEY_TILE:(j + 1) * KEY_TILE], -MASK_BIG)
                 for j in range(sub_tiles)], axis=0) for x in sc]
        m_new = [jnp.maximum(carry[h][0], jnp.max(sc[h], axis=0, keepdims=True)) for h in heads]
        alpha = [jnp.exp2(carry[h][0] - m_new[h]) for h in heads]
        p = [jnp.exp2(sc[h] - m_new[h]) for h in heads]
        l_new = [alpha[h] * carry[h][1] + jnp.sum(p[h], axis=0, keepdims=True) for h in heads]
        pb = [p[h].astype(BF16) for h in heads]
        v_tiles = [[vst_ref[i * sub_tiles + j, gd[g], :] for j in range(sub_tiles)] for g in groups]
        pv = [sum(jnp.dot(v_tiles[grp(h)][j], pb[h][j * KEY_TILE:(j + 1) * KEY_TILE],
                          preferred_element_type=F32) for j in range(1, sub_tiles))
              + jnp.dot(v_tiles[grp(h)][0], pb[h][0:KEY_TILE], preferred_element_type=F32) for h in heads]
        return tuple((m_new[h], l_new[h], alpha[h] * carry[h][2] + pv[h]) for h in heads)

    init = tuple((jnp.full((1, nq), M_INIT, F32), jnp.zeros((1, nq), F32), jnp.zeros((d, nq), F32))
                 for _ in heads)
    n_full = s0 // SEL_TILE

    sel_scores(0, sca_ref)

    zeros = jnp.zeros((nq, d), BF16)
    q_w = [jnp.concatenate([q_h[h], zeros] if grp(h) == 0 else [zeros, q_h[h]], axis=1)
           for h in heads]
    k_tiles = [kw_ref[0, pl.ds(pl.multiple_of(w_start + j * KEY_TILE, KEY_TILE), KEY_TILE), :]
               for j in range(WIN_TILES)]
    win_ok = [(lane_minus_sub + (s0 - w_start - j * KEY_TILE)).astype(jnp.uint32) < WINDOW
              for j in range(WIN_TILES)]

    def win_scores(h):
        return [jnp.where(win_ok[j], _dot_nt(k_tiles[j], q_w[h]), -MASK_BIG) for j in range(WIN_TILES)]

    def win_softmax(sc):
        m = sc[0]
        for x in sc[1:]:
            m = jnp.maximum(m, x)
        m = jnp.max(m, axis=0, keepdims=True)
        p = [jnp.exp2(x - m) for x in sc]
        tot = p[0]
        for x in p[1:]:
            tot = tot + x
        return [x.astype(BF16) for x in p], jnp.sum(tot, axis=0, keepdims=True)

    def win_out(h, pl_):
        p, l = pl_
        acc = None
        for j in range(WIN_TILES):
            term = jnp.dot(vwt_ref[w_tile0 + j, gd[grp(h)], :], p[j], preferred_element_type=F32)
            acc = term if acc is None else acc + term
        return acc / l

    sc_w, p_w, o_wt = {}, {}, {}
    for step in range(NSA_HEADS + 2):
        if step < NSA_HEADS:
            sc_w[step] = win_scores(step)
        if 0 <= step - 1 < NSA_HEADS:
            p_w[step - 1] = win_softmax(sc_w.pop(step - 1))
        if 0 <= step - 2 < NSA_HEADS:
            o_wt[step - 2] = win_out(step - 2, p_w.pop(step - 2))

    def sel_body(k, carry):
        sel_scores(2 * k + 1, scb_ref)
        carry = sel_step(2 * k, sca_ref, carry, masked=False)
        sel_scores(2 * k + 2, sca_ref)
        return sel_step(2 * k + 1, scb_ref, carry, masked=False)

    carry = lax.fori_loop(0, n_full // 2, sel_body, init)

    def odd_tail(carry):
        sel_scores(n_full, scb_ref)
        carry = sel_step(n_full - 1, sca_ref, carry, masked=False)
        sca_ref[...] = scb_ref[...]
        return carry

    carry = lax.cond(n_full % 2 == 1, odd_tail, lambda c: c, carry)
    carry = sel_step(n_full, sca_ref, carry, masked=True)
    o_st = [acc / l for (_, l, acc) in carry]

    for pair in range(NSA_HEADS // 2):
        halves = []
        for h in (2 * pair, 2 * pair + 1):
            r = h % NSA_HPG
            halves.append(gates_t[3 * h:3 * h + 1, :] * o_ct[grp(h)][:, r * nq:(r + 1) * nq]
                          + gates_t[3 * h + 1:3 * h + 2, :] * o_st[h]
                          + gates_t[3 * h + 2:3 * h + 3, :] * o_wt[h])
        o_ref[0, :, 2 * pair * d:(2 * pair + 2) * d] = jnp.concatenate(halves, axis=0).T.astype(o_ref.dtype)


def _cmp_to_slc_t(s):
    n_cmp = s // CMP_STRIDE
    n_slc = s // SLC_BLOCK
    cs = CMP_STRIDE * np.arange(n_cmp)[:, None]
    ss = SLC_BLOCK * np.arange(n_slc)[None, :]
    overlap = np.clip(np.minimum(cs + CMP_BLOCK, ss + SLC_BLOCK) - np.maximum(cs, ss), 0, None)
    m = np.zeros((MASK_W, n_cmp), np.float32)
    m[:n_slc] = (overlap / CMP_BLOCK).T
    return jnp.asarray(m).astype(BF16)


def _nsa(p, k_cmp, v_cmp):
    q = p["q"]
    bn, s, _ = q.shape
    assert s // SLC_BLOCK <= MASK_W and s % SEL_TILE == 0
    top = min(SLC_TOP, s // SLC_BLOCK)
    nc = k_cmp.shape[2]
    kvw = NSA_GROUPS * HEAD_DIM
    per_b = lambda w: pl.BlockSpec((1, s, w), lambda b, i: (b, 0, 0))
    return pl.pallas_call(
        functools.partial(_nsa_kernel, top=top),
        grid=(bn, s // Q_BLOCK),
        in_specs=[pl.BlockSpec((1, Q_BLOCK, NSA_HEADS * HEAD_DIM), lambda b, i: (b, i, 0)),
                  pl.BlockSpec((1, Q_BLOCK, LANES), lambda b, i: (b, i, 0)),
                  pl.BlockSpec((1, NSA_GROUPS, nc, HEAD_DIM), lambda b, i: (b, 0, 0, 0)),
                  pl.BlockSpec((1, NSA_GROUPS, nc, HEAD_DIM), lambda b, i: (b, 0, 0, 0)),
                  per_b(kvw), per_b(kvw), per_b(kvw), per_b(kvw),
                  pl.BlockSpec((MASK_W, nc), lambda b, i: (0, 0))],
        out_specs=pl.BlockSpec((1, Q_BLOCK, NSA_HEADS * HEAD_DIM), lambda b, i: (b, i, 0)),
        out_shape=jax.ShapeDtypeStruct((bn, s, NSA_HEADS * HEAD_DIM), BF16),
        scratch_shapes=[pltpu.VMEM((NSA_GROUPS, s, HEAD_DIM + MASK_W), BF16),
                        pltpu.VMEM((s // KEY_TILE, kvw, KEY_TILE), BF16),
                        pltpu.VMEM((s // KEY_TILE, kvw, KEY_TILE), BF16),
                        pltpu.VMEM((NSA_GROUPS, HEAD_DIM, nc), BF16),
                        pltpu.VMEM((NSA_HEADS, SEL_TILE, Q_BLOCK), F32),
                        pltpu.VMEM((NSA_HEADS, SEL_TILE, Q_BLOCK), F32),
                        pltpu.VMEM((NSA_GROUPS, MASK_W, Q_BLOCK), F32)],
        compiler_params=pltpu.CompilerParams(dimension_semantics=("parallel", "arbitrary"),
                                             vmem_limit_bytes=VMEM_LIMIT),
        name="nsa_attention",
    )(q, p["g"], k_cmp, v_cmp, p["ks"], p["vs"], p["kw"], p["vw"], _cmp_to_slc_t(s))


S5_STRIP = 512


def _gelu_tanh(y):
    return 0.5 * y * (1.0 + jnp.tanh(math.sqrt(2.0 / math.pi) * (y + 0.044715 * (y * y * y))))


def _s5_kernel(u_ref, bre_ref, bim_ref, are_ref, aim_ref, cre_ref, cim_ref, d_ref, gw_ref, gb_ref,
               o_ref, xre_ref, xim_ref, sre_ref, sim_ref):
    tc, bn, width = u_ref.shape
    nstate = are_ref.shape[1]

    @pl.when(pl.program_id(0) == 0)
    def _():
        sre_ref[...] = jnp.zeros_like(sre_ref)
        sim_ref[...] = jnp.zeros_like(sim_ref)

    u = u_ref[...].reshape(tc * bn, width)
    ub = u.astype(BF16)
    n_strips = nstate // S5_STRIP
    cw = width // n_strips
    for k in range(n_strips):
        cols = pl.ds(k * S5_STRIP, S5_STRIP)
        xre_ref[:, cols] = jnp.dot(ub[:, k * cw:(k + 1) * cw], bre_ref[k], preferred_element_type=F32)
        xim_ref[:, cols] = jnp.dot(ub[:, k * cw:(k + 1) * cw], bim_ref[k], preferred_element_type=F32)

    for k in range(n_strips):
        c0 = k * S5_STRIP
        cols = pl.ds(c0, S5_STRIP)
        a_r = jnp.broadcast_to(are_ref[:, c0:c0 + S5_STRIP], (bn, S5_STRIP))
        a_i = jnp.broadcast_to(aim_ref[:, c0:c0 + S5_STRIP], (bn, S5_STRIP))

        def step(t, carry):
            x_r, x_i = carry
            r0 = pl.multiple_of(t * bn, bn)
            n_r = a_r * x_r - a_i * x_i + xre_ref[pl.ds(r0, bn), cols]
            n_i = a_r * x_i + a_i * x_r + xim_ref[pl.ds(r0, bn), cols]
            xre_ref[pl.ds(r0, bn), cols] = n_r
            xim_ref[pl.ds(r0, bn), cols] = n_i
            return n_r, n_i

        x_r, x_i = lax.fori_loop(0, tc, step, (sre_ref[:, cols], sim_ref[:, cols]), unroll=8)
        sre_ref[:, cols] = x_r
        sim_ref[:, cols] = x_i

    y = jnp.concatenate(
        [jnp.dot(xre_ref[:, k * S5_STRIP:(k + 1) * S5_STRIP].astype(BF16), cre_ref[k],
                 preferred_element_type=F32)
         + jnp.dot(xim_ref[:, k * S5_STRIP:(k + 1) * S5_STRIP].astype(BF16), cim_ref[k],
                   preferred_element_type=F32) for k in range(n_strips)], axis=1) + d_ref[...] * u
    y = _gelu_tanh(y)
    z = _bdot(y, gw_ref[...]) + gb_ref[...]
    o_ref[...] = (y * jax.nn.sigmoid(z)).reshape(tc, bn, width)


def _s5_params(a_re, a_im, b_re, b_im, c_re, c_im, log_dt):
    g, p = a_re.shape
    c = b_re.shape[-1]
    a_re, a_im = a_re.astype(F32), a_im.astype(F32)
    b_re, b_im = b_re.astype(F32), b_im.astype(F32)
    dt = jnp.exp(log_dt.astype(F32))[:, None]
    mag = jnp.exp(a_re * dt)
    abar_re, abar_im = mag * jnp.cos(a_im * dt), mag * jnp.sin(a_im * dt)
    den = a_re * a_re + a_im * a_im
    f_re = ((abar_re - 1.0) * a_re + abar_im * a_im) / den
    f_im = (abar_im * a_re - (abar_re - 1.0) * a_im) / den
    bbar_re = f_re[..., None] * b_re - f_im[..., None] * b_im
    bbar_im = f_re[..., None] * b_im + f_im[..., None] * b_re
    gs = S5_STRIP // p
    nb = g // gs
    eye = jnp.eye(gs, dtype=F32)
    pack_b = lambda m: jnp.einsum("kgpc,gh->kgchp", m.reshape(nb, gs, p, c), eye).reshape(
        nb, gs * c, gs * p).astype(BF16)
    pack_c = lambda m: jnp.einsum("kgcp,gh->kgphc", m.reshape(nb, gs, c, p), eye).reshape(
        nb, gs * p, gs * c).astype(BF16)
    return (pack_b(bbar_re), pack_b(bbar_im), abar_re.reshape(1, g * p), abar_im.reshape(1, g * p),
            pack_c(c_re.astype(F32)), pack_c(-c_im.astype(F32)))


def _s5(u_t, params, d, glu_w, glu_b, tc=64):
    s, bn, width = u_t.shape
    assert bn == SUBLANES
    bre, bim, are, aim, cre, cim = params
    nstate = are.shape[1]
    full = lambda a: pl.BlockSpec(a.shape, lambda i: tuple(0 for _ in a.shape))
    d2, gb2, gwb = d.reshape(1, width), glu_b.reshape(1, width), glu_w.astype(BF16)
    return pl.pallas_call(
        _s5_kernel,
        grid=(s // tc,),
        in_specs=[pl.BlockSpec((tc, bn, width), lambda i: (i, 0, 0)),
                  full(bre), full(bim), full(are), full(aim), full(cre), full(cim),
                  full(d2), full(gwb), full(gb2)],
        out_specs=pl.BlockSpec((tc, bn, width), lambda i: (i, 0, 0)),
        out_shape=jax.ShapeDtypeStruct((s, bn, width), F32),
        scratch_shapes=[pltpu.VMEM((tc * bn, nstate), F32), pltpu.VMEM((tc * bn, nstate), F32),
                        pltpu.VMEM((bn, nstate), F32), pltpu.VMEM((bn, nstate), F32)],
        compiler_params=pltpu.CompilerParams(dimension_semantics=("arbitrary",),
                                             vmem_limit_bytes=VMEM_LIMIT),
        name="s5_scan",
    )(u_t, bre, bim, are, aim, cre, cim, d2, gwb, gb2)


RWKV_CHUNK = 64
RWKV_BLOCK = 256


def _split_dot(x, seg):
    hi = x.astype(BF16)
    lo = (x - hi.astype(F32)).astype(BF16)
    return (jnp.dot(hi, seg, preferred_element_type=F32) + jnp.dot(lo, seg, preferred_element_type=F32))


def _rwkv_kernel(mix_ref, mu_ref, w0_ref, w2_ref, a0_ref, a2_ref, kk_ref, ka_ref, rk_ref,
                 lnw_ref, lnb_ref, seg_ref, o_ref, state_ref, carry_ref, y_ref):
    n = HEAD_DIM
    chunk = RWKV_CHUNK
    tb = mix_ref.shape[1]
    w = BRANCH_W

    @pl.when(pl.program_id(1) == 0)
    def _():
        state_ref[...] = jnp.zeros_like(state_ref)
        carry_ref[...] = jnp.zeros_like(carry_ref)

    mix = mix_ref[0]
    row_m = lax.broadcasted_iota(jnp.int32, mix.shape, 0)
    prev = jnp.where(row_m == 0, carry_ref[0:1, :], pltpu.roll(mix, 1, 0))
    carry_ref[0:1, :] = mix[tb - 1:tb, :]
    xs = mix + (prev - mix) * mu_ref[...]
    r, k, v = xs[:, 0:w], xs[:, w:2 * w], xs[:, 2 * w:3 * w]
    wl, al = xs[:, 3 * w:3 * w + LORA_W], xs[:, 3 * w + LORA_W:3 * w + 2 * LORA_W]

    lw = w0_ref[...] + _bdot(jnp.tanh(wl), w2_ref[...])
    z = -lw
    softplus = jnp.maximum(z, 0.0) + jnp.log(1.0 + jnp.exp(-jnp.abs(z)))
    ld = -jnp.exp(-softplus - 0.5)
    a = jax.nn.sigmoid(a0_ref[...] + _bdot(al, a2_ref[...]))
    seg = seg_ref[...]
    kk = k * kk_ref[...]
    kk = kk / jnp.maximum(jnp.sqrt(_split_dot(kk * kk, seg)), 1e-12)
    k2 = k * (1.0 + (a - 1.0) * ka_ref[...])
    aa = -kk
    bb = kk * a

    row = lax.broadcasted_iota(jnp.int32, (tb, w), 0) % chunk
    cum = ld
    sh = 1
    while sh < chunk:
        cum = cum + jnp.where(row >= sh, pltpu.roll(cum, sh, 0), 0.0)
        sh *= 2
    at = (aa * jnp.exp(cum - ld)).astype(BF16)
    rt = (r * jnp.exp(cum)).astype(BF16)
    e_neg = jnp.exp(-cum)
    bt, kt = (bb * e_neg).astype(BF16), (k2 * e_neg).astype(BF16)
    vb = v.astype(BF16)

    ti = lax.broadcasted_iota(jnp.int32, (2 * chunk, chunk), 0)
    si = lax.broadcasted_iota(jnp.int32, (2 * chunk, chunk), 1)
    strict = si < ti
    lower2 = si < jnp.where(ti < chunk, ti, ti - chunk + 1)
    n_double = int(math.log2(chunk)) - 1

    n_chunks = tb // chunk
    pairs = [(ci, h) for ci in range(n_chunks) for h in range(RWKV_HEADS)]
    rows_of = lambda ci: slice(ci * chunk, (ci + 1) * chunk)
    lanes_of = lambda h: slice(h * n, (h + 1) * n)
    bdot32 = lambda p_, q_: jnp.dot(p_, q_, preferred_element_type=F32)

    at_l = [at[rows_of(ci), lanes_of(h)] for ci, h in pairs]
    rt_l = [rt[rows_of(ci), lanes_of(h)] for ci, h in pairs]
    v_l = [vb[rows_of(ci), lanes_of(h)] for ci, h in pairs]
    amat_l = [_dot_nt(jnp.concatenate([at_l[i], rt_l[i]], axis=0),
                      jnp.concatenate([bt[rows_of(ci), lanes_of(h)], kt[rows_of(ci), lanes_of(h)]], axis=0))
              for i, (ci, h) in enumerate(pairs)]
    pw_l = [jnp.where(strict[:chunk], m[:chunk, :chunk], 0.0).astype(BF16) for m in amat_l]
    arb_l = [jnp.where(lower2[chunk:], m[chunk:, :chunk], 0.0).astype(BF16) for m in amat_l]
    axk_l = [jnp.where(lower2, m[:, chunk:], 0.0).astype(BF16) for m in amat_l]
    xv_l = [bdot32(axk_l[i], v_l[i]) for i in range(len(pairs))]
    x_l = [jnp.concatenate([at_l[i].astype(F32), xv_l[i][:chunk]], axis=1) for i in range(len(pairs))]
    x_l = [x + bdot32(pw, x.astype(BF16)) for x, pw in zip(x_l, pw_l)]
    for _ in range(n_double):
        pw_l = [bdot32(pw, pw).astype(BF16) for pw in pw_l]
        x_l = [x + bdot32(pw, x.astype(BF16)) for x, pw in zip(x_l, pw_l)]
    ro_l = [jnp.concatenate([rt_l[i].astype(F32), xv_l[i][chunk:]], axis=1)
            + bdot32(arb_l[i], x_l[i].astype(BF16)) for i in range(len(pairs))]
    wr_l = [jnp.concatenate([x[:, :n], ro[:, :n]], axis=0).astype(BF16) for x, ro in zip(x_l, ro_l)]

    state = [state_ref[h] for h in range(RWKV_HEADS)]
    for ci in range(n_chunks):
        rows = rows_of(ci)
        cum_c = cum[rows]
        tot = cum_c[chunk - 1:chunk, :]
        e_rem = jnp.exp(tot - cum_c)
        bh, kh = (bb[rows] * e_rem).astype(BF16), (k2[rows] * e_rem).astype(BF16)
        p_tot = jnp.exp(tot)
        base = ci * RWKV_HEADS
        g_l = [_dot_nt(wr_l[base + h], state[h].astype(BF16)) for h in range(RWKV_HEADS)]
        for h in range(RWKV_HEADS):
            y_ref[rows, lanes_of(h)] = g_l[h][chunk:] + ro_l[base + h][:, n:]
        uv_l = [jnp.concatenate([g_l[h][:chunk] + x_l[base + h][:, n:], v_l[base + h].astype(F32)], axis=0)
                for h in range(RWKV_HEADS)]
        state = [state[h] * p_tot[:, lanes_of(h)]
                 + bdot32(uv_l[h].T.astype(BF16),
                          jnp.concatenate([bh[:, lanes_of(h)], kh[:, lanes_of(h)]], axis=0))
                 for h in range(RWKV_HEADS)]
    for h in range(RWKV_HEADS):
        state_ref[h] = state[h]

    y = y_ref[...]
    inv_n = 1.0 / n
    mean = _split_dot(y, seg) * inv_n
    dev = y - mean
    var = _split_dot(dev * dev, seg) * inv_n
    yn = dev * lax.rsqrt(var + RWKV_LN_EPS) * lnw_ref[...] + lnb_ref[...]
    bonus = _split_dot(r * k2 * rk_ref[...], seg) * v
    o_ref[0] = (yn + bonus).astype(o_ref.dtype)


def _rwkv(mix, mu, w0, w2, a0, a2, k_k, k_a, r_k, ln_w, ln_b):
    bn, s, mw = mix.shape
    w = BRANCH_W
    tb = min(RWKV_BLOCK, s)
    head = np.arange(w) // HEAD_DIM
    seg = jnp.asarray((head[:, None] == head[None, :]).astype(np.float32)).astype(BF16)
    row = lambda t: t.reshape(1, -1).astype(F32)
    args = (row(mu), row(w0), w2.astype(BF16), row(a0), a2.astype(BF16), row(k_k), row(k_a), row(r_k),
            row(ln_w), row(ln_b), seg)
    full = lambda a: pl.BlockSpec(a.shape, lambda b, i: tuple(0 for _ in a.shape))
    return pl.pallas_call(
        _rwkv_kernel,
        grid=(bn, s // tb),
        in_specs=[pl.BlockSpec((1, tb, mw), lambda b, i: (b, i, 0))] + [full(t) for t in args],
        out_specs=pl.BlockSpec((1, tb, w), lambda b, i: (b, i, 0)),
        out_shape=jax.ShapeDtypeStruct((bn, s, w), BF16),
        scratch_shapes=[pltpu.VMEM((RWKV_HEADS, HEAD_DIM, HEAD_DIM), F32),
                        pltpu.VMEM((SUBLANES, mw), F32),
                        pltpu.VMEM((tb, w), F32)],
        compiler_params=pltpu.CompilerParams(dimension_semantics=("parallel", "arbitrary"),
                                             vmem_limit_bytes=VMEM_LIMIT),
        name="rwkv7",
    )(mix, *args)


def _merge_kernel(on_ref, os_ref, or_ref, gn_ref, gs_ref, gr_ref, mg_ref, x_ref, gate_ref,
                  wup_ref, wout_ref, fnw_ref, o_ref, *, final):
    d = x_ref.shape[2]
    merged = None
    for i, (b_ref, g_ref) in enumerate(((on_ref, gn_ref), (os_ref, gs_ref), (or_ref, gr_ref))):
        branch = b_ref[0].astype(F32) * _silu(g_ref[0].astype(F32))
        up = _bdot(branch, wup_ref[i])
        term = jax.nn.sigmoid(mg_ref[0, :, i * d:(i + 1) * d].astype(F32)) * up
        merged = term if merged is None else merged + term
    out = x_ref[0] + gate_ref[0] * _bdot(merged, wout_ref[...])
    if final:
        ms = jnp.mean(out * out, axis=-1, keepdims=True)
        out = out * lax.rsqrt(ms + NORM_EPS) * fnw_ref[...]
    o_ref[0] = out


def _merge(o_nsa, o_s5, o_rwkv, p, x, gate, w_up, w_out, fnw, final, tm=512):
    bn, s, d = x.shape
    w = BRANCH_W
    rows = lambda width: pl.BlockSpec((1, tm, width), lambda b, i: (b, i, 0))
    wupb, woutb = w_up.astype(BF16), w_out.astype(BF16)
    return pl.pallas_call(
        functools.partial(_merge_kernel, final=final),
        grid=(bn, s // tm),
        in_specs=[rows(w)] * 6 + [rows(3 * d), rows(d),
                                  pl.BlockSpec((1, 1, d), lambda b, i: (b, 0, 0)),
                                  pl.BlockSpec(wupb.shape, lambda b, i: (0, 0, 0)),
                                  pl.BlockSpec(woutb.shape, lambda b, i: (0, 0)),
                                  pl.BlockSpec((1, d), lambda b, i: (0, 0))],
        out_specs=rows(d),
        out_shape=jax.ShapeDtypeStruct((bn, s, d), F32),
        compiler_params=pltpu.CompilerParams(dimension_semantics=("parallel", "parallel"),
                                             vmem_limit_bytes=VMEM_LIMIT),
        name="merge_out",
    )(o_nsa, o_s5, o_rwkv, p["ng"], p["sg"], p["rg"], p["mg"], x, gate.reshape(bn, 1, d),
      wupb, woutb, fnw.reshape(1, d))


def kernel(x, c, norm_w, mod_w, mod_b, w_in, cmp_pos_k, cmp_pos_v, cmp_w1_k, cmp_w2_k, cmp_w1_v, cmp_w2_v, s5_a_re, s5_a_im, s5_b_re, s5_b_im, s5_c_re, s5_c_im, s5_d, s5_log_dt, s5_glu_w, s5_glu_b, rwkv_mu, rwkv_w0, rwkv_w2, rwkv_a0, rwkv_a2, rwkv_k_k, rwkv_k_a, rwkv_r_k, rwkv_ln_w, rwkv_ln_b, w_up, w_out, final_norm_w):
    bn, s, d = x.shape
    depth = norm_w.shape[0]
    cos_t, sin_t = _rope_tables(s)
    mod = _modulation(c, mod_w, mod_b)
    for l in range(depth):
        shift, scale, gate = mod[l, :, 0:d], mod[l, :, d:2 * d], mod[l, :, 2 * d:3 * d]
        p = _inproj(x, norm_w[l], scale, shift, cos_t, sin_t, _pack_w_in(w_in[l]))
        k_cmp, v_cmp = _compress(p["kc"], p["vc"], cmp_pos_k[l], cmp_pos_v[l], cmp_w1_k[l], cmp_w2_k[l],
                                 cmp_w1_v[l], cmp_w2_v[l])
        o_nsa = _nsa(p, k_cmp, v_cmp)
        s5p = _s5_params(s5_a_re[l], s5_a_im[l], s5_b_re[l], s5_b_im[l], s5_c_re[l], s5_c_im[l],
                         s5_log_dt[l])
        o_s5 = _s5(jnp.swapaxes(p["su"], 0, 1).astype(F32), s5p, s5_d[l], s5_glu_w[l], s5_glu_b[l])
        o_s5 = jnp.swapaxes(o_s5, 0, 1).astype(BF16)
        o_rwkv = _rwkv(p["mix"], rwkv_mu[l], rwkv_w0[l], rwkv_w2[l], rwkv_a0[l], rwkv_a2[l],
                       rwkv_k_k[l], rwkv_k_a[l], rwkv_r_k[l], rwkv_ln_w[l], rwkv_ln_b[l])
        x = _merge(o_nsa, o_s5, o_rwkv, p, x, gate, w_up[l], w_out[l], final_norm_w,
                   final=(l == depth - 1))
    return x
```

```python
import functools
import math

import numpy as np
import jax
import jax.numpy as jnp
from jax import lax
from jax.experimental import pallas as pl
from jax.experimental.pallas import tpu as pltpu

F32 = jnp.float32
BF16 = jnp.bfloat16
HIGHEST = lax.Precision.HIGHEST

HEAD_DIM = 64
NSA_HEADS = 8
NSA_GROUPS = 2
NSA_HPG = NSA_HEADS // NSA_GROUPS
CMP_BLOCK = 32
CMP_STRIDE = 16
SLC_BLOCK = 64
SLC_TOP = 16
WINDOW = 512
Q_BLOCK = 128
S5_GROUPS = 32
S5_GROUP_CH = 16
S5_STATE = 64
RWKV_HEADS = 8
BRANCH_W = 512
LORA_W = 64
RWKV_MIX_W = 3 * BRANCH_W + 2 * LORA_W
ROPE_THETA = 10000.0
NORM_EPS = 1e-6
RWKV_LN_EPS = 64e-5
NEG_BIG = -1e30
MASK_BIG = 2.0 ** 100
M_INIT = -3.0e38
Q_SCALE = HEAD_DIM ** -0.5 * math.log2(math.e)

VMEM_LIMIT = 56 * 1024 * 1024
LANES = 128
SUBLANES = 8

C_Q = 0
C_KC, C_KS, C_KW, C_VC, C_VS, C_VW = 512, 640, 768, 896, 1024, 1152
C_G = 1280
C_NG = 1408
C_SU = 1920
C_SG = 2432
C_MIX = 2944
C_RG = C_MIX + RWKV_MIX_W
C_MG = C_RG + BRANCH_W
IN_PACKED = C_MG + 3 * 1024


def _silu(z):
    return z * jax.nn.sigmoid(z)


def _bdot(a, b):
    return jnp.dot(a.astype(BF16), b.astype(BF16), preferred_element_type=F32)


def _dot_nt(a, b, precision=None):
    return lax.dot_general(a, b, (((1,), (1,)), ((), ())), precision=precision,
                           preferred_element_type=F32)


def _hdot(a, b):
    return jnp.dot(a, b, precision=HIGHEST, preferred_element_type=F32)


def _mod_kernel(c_ref, w_ref, b_ref, o_ref):
    cond = _silu(c_ref[...])
    o_ref[0] = _bdot(cond, w_ref[0]) + b_ref[0]


def _modulation(c, mod_w, mod_b):
    depth, d, d3 = mod_w.shape
    bn = c.shape[0]
    nj = d3 // d
    return pl.pallas_call(
        _mod_kernel,
        grid=(depth, nj),
        in_specs=[pl.BlockSpec((bn, d), lambda l, j: (0, 0)),
                  pl.BlockSpec((1, d, d), lambda l, j: (l, 0, j)),
                  pl.BlockSpec((1, 1, d), lambda l, j: (l, 0, j))],
        out_specs=pl.BlockSpec((1, bn, d), lambda l, j: (l, 0, j)),
        out_shape=jax.ShapeDtypeStruct((depth, bn, d3), F32),
        name="adaln_mod",
    )(c, mod_w, mod_b.reshape(depth, 1, d3))


_INPROJ_OUTS = (
    ("q", C_Q, 512, BF16), ("kc", C_KC, 128, F32), ("ks", C_KS, 128, BF16), ("kw", C_KW, 128, BF16),
    ("vc", C_VC, 128, F32), ("vs", C_VS, 128, BF16), ("vw", C_VW, 128, BF16), ("g", C_G, 128, F32),
    ("ng", C_NG, 512, BF16), ("su", C_SU, 512, BF16), ("sg", C_SG, 512, BF16),
    ("mix", C_MIX, RWKV_MIX_W, F32), ("rg", C_RG, 512, BF16), ("mg", C_MG, 3072, BF16))
_ROPED = ("q", "kc", "ks", "kw")


def _inproj_kernel(x_ref, nw_ref, sc_ref, sh_ref, cos_ref, sin_ref, w_ref, *out_refs):
    x = x_ref[0]
    tm = x.shape[0]
    ms = jnp.mean(x * x, axis=-1, keepdims=True)
    h = x * lax.rsqrt(ms + NORM_EPS) * nw_ref[...]
    h = h * (1.0 + sc_ref[0]) + sh_ref[0]
    hb = h.astype(BF16)
    cos = cos_ref[...]
    sin = sin_ref[...]
    lane = lax.broadcasted_iota(jnp.int32, (tm, LANES), 1)
    first_half = (lane % HEAD_DIM) < (HEAD_DIM // 2)

    def rope(t):
        partner = jnp.where(first_half, pltpu.roll(t, 96, 1), pltpu.roll(t, 32, 1))
        return t * cos + partner * sin

    for (name, c0, width, dt), o_ref in zip(_INPROJ_OUTS, out_refs):
        step = min(width, 512)
        for j0 in range(0, width, step):
            w = min(step, width - j0)
            y = jnp.dot(hb, w_ref[:, c0 + j0:c0 + j0 + w], preferred_element_type=F32)
            if name in _ROPED:
                y = jnp.concatenate([rope(y[:, k:k + LANES]) for k in range(0, w, LANES)], axis=1)
            if name == "q":
                y = y * Q_SCALE
            o_ref[0, :, j0:j0 + w] = y.astype(dt)


def _inproj(x, nw, scale, shift, cos_t, sin_t, w_packed, tm=512):
    bn, s, d = x.shape
    out_shape = [jax.ShapeDtypeStruct((bn, s, w), dt) for (_, _, w, dt) in _INPROJ_OUTS]
    out_specs = [pl.BlockSpec((1, tm, w), lambda b, i: (b, i, 0)) for (_, _, w, _) in _INPROJ_OUTS]
    outs = pl.pallas_call(
        _inproj_kernel,
        grid=(bn, s // tm),
        in_specs=[pl.BlockSpec((1, tm, d), lambda b, i: (b, i, 0)),
                  pl.BlockSpec((1, d), lambda b, i: (0, 0)),
                  pl.BlockSpec((1, 1, d), lambda b, i: (b, 0, 0)),
                  pl.BlockSpec((1, 1, d), lambda b, i: (b, 0, 0)),
                  pl.BlockSpec((tm, LANES), lambda b, i: (i, 0)),
                  pl.BlockSpec((tm, LANES), lambda b, i: (i, 0)),
                  pl.BlockSpec((d, IN_PACKED), lambda b, i: (0, 0), pipeline_mode=pl.Buffered(1))],
        out_specs=out_specs,
        out_shape=out_shape,
        compiler_params=pltpu.CompilerParams(dimension_semantics=("parallel", "parallel"),
                                             vmem_limit_bytes=VMEM_LIMIT),
        name="inproj",
    )(x, nw.reshape(1, d), scale.reshape(bn, 1, d), shift.reshape(bn, 1, d), cos_t, sin_t, w_packed)
    return dict(zip([o[0] for o in _INPROJ_OUTS], outs))


def _pack_w_in(w_in):
    d = w_in.shape[0]
    sizes = (512, 768, 24, 512, 512, 512, RWKV_MIX_W, 512, 3072)
    offs = np.concatenate([[0], np.cumsum(sizes)])
    q, kv, g, ng, su, sg, mix, rg, mg = [w_in[:, offs[i]:offs[i + 1]] for i in range(len(sizes))]
    kc, vc, ks, vs, kw, vw = [kv[:, i * 128:(i + 1) * 128] for i in range(6)]
    gpad = jnp.pad(g, ((0, 0), (0, 128 - 24)))
    return jnp.concatenate([q, kc, ks, kw, vc, vs, vw, gpad, ng, su, sg, mix, rg, mg], axis=1).astype(BF16)


def _rope_tables(s):
    half = HEAD_DIM // 2
    inv = jnp.exp(-math.log(ROPE_THETA) * jnp.arange(half, dtype=F32) / half)
    ang = jnp.arange(s, dtype=F32)[:, None] * inv[None, :]
    cos, sin = jnp.cos(ang), jnp.sin(ang)
    cos_t = jnp.tile(cos, (1, LANES // half))
    sin_t = jnp.tile(jnp.concatenate([-sin, sin], axis=1), (1, LANES // HEAD_DIM))
    return cos_t, sin_t


def _compress_kernel(kc_ref, vc_ref, pk_ref, pv_ref, wkt_ref, wkb_ref, wvt_ref, wvb_ref,
                     w2k_ref, w2v_ref, ko_ref, vo_ref):
    def one(x_ref, p_ref, wt_ref, wb_ref, w2_ref, o_ref):
        kvw = x_ref.shape[2]
        n = x_ref.shape[1] // CMP_STRIDE
        top = bot = None
        for l in range(CMP_STRIDE):
            x_l = x_ref[0, pl.ds(l, n, stride=CMP_STRIDE), :]
            cols = slice(l * kvw, (l + 1) * kvw)
            t = _bdot(x_l + p_ref[0:1, cols], wt_ref[cols, :])
            b = _bdot(x_l + p_ref[1:2, cols], wb_ref[cols, :])
            top, bot = (t, b) if top is None else (top + t, bot + b)
        hid = top + pltpu.roll(bot, n - 1, 0)
        act = _silu(hid)
        hw = act.shape[1] // NSA_GROUPS
        for g in range(NSA_GROUPS):
            o_ref[0, g] = _bdot(act[:, g * hw:(g + 1) * hw], w2_ref[...])

    one(kc_ref, pk_ref, wkt_ref, wkb_ref, w2k_ref, ko_ref)
    one(vc_ref, pv_ref, wvt_ref, wvb_ref, w2v_ref, vo_ref)


def _compress_weights(pos, w1, w2):
    hid = w1.shape[1]
    half = CMP_BLOCK // 2
    w1r = w1.reshape(2, half, HEAD_DIM, hid)
    eye = jnp.eye(NSA_GROUPS, dtype=w1.dtype)
    wd = jnp.einsum("tldj,gh->tlgdhj", w1r, eye).reshape(2, half * NSA_GROUPS * HEAD_DIM,
                                                         NSA_GROUPS * hid)
    pr = pos.reshape(2, half, 1, HEAD_DIM)
    pt = jnp.broadcast_to(pr, (2, half, NSA_GROUPS, HEAD_DIM)).reshape(2, -1)
    return pt, wd[0].astype(BF16), wd[1].astype(BF16), w2.astype(BF16)


def _compress(kc, vc, pos_k, pos_v, w1k, w2k, w1v, w2v):
    bn, s, kvw = kc.shape
    n16 = s // CMP_STRIDE
    row_w = CMP_STRIDE * kvw
    pk, wkt, wkb, w2kb = _compress_weights(pos_k, w1k, w2k)
    pv, wvt, wvb, w2vb = _compress_weights(pos_v, w1v, w2v)
    hid2 = wkt.shape[1]
    full = lambda shape: pl.BlockSpec(shape, lambda b: tuple(0 for _ in shape))
    out = jax.ShapeDtypeStruct((bn, NSA_GROUPS, n16, HEAD_DIM), F32)
    return pl.pallas_call(
        _compress_kernel,
        grid=(bn,),
        in_specs=[pl.BlockSpec((1, s, kvw), lambda b: (b, 0, 0)),
                  pl.BlockSpec((1, s, kvw), lambda b: (b, 0, 0)),
                  full((2, row_w)), full((2, row_w)),
                  full((row_w, hid2)), full((row_w, hid2)), full((row_w, hid2)), full((row_w, hid2)),
                  full(w2kb.shape), full(w2vb.shape)],
        out_specs=[pl.BlockSpec((1, NSA_GROUPS, n16, HEAD_DIM), lambda b: (b, 0, 0, 0))] * 2,
        out_shape=[out, out],
        compiler_params=pltpu.CompilerParams(dimension_semantics=("parallel",),
                                             vmem_limit_bytes=VMEM_LIMIT),
        name="nsa_compress",
    )(kc, vc, pk, pv, wkt, wkb, wvt, wvb, w2kb, w2vb)


SEL_TILE = 256
KEY_TILE = 128
WIN_TILES = WINDOW // KEY_TILE + 1
MASK_W = 64


def _nsa_kernel(q_ref, g_ref, kcmp_ref, vcmp_ref, ks_ref, vs_ref, kw_ref, vw_ref, c2s_ref,
                o_ref, kaug_ref, vst_ref, vwt_ref, vct_ref, sca_ref, scb_ref, imp_ref, *, top):
    qb = pl.program_id(1)
    s0 = qb * Q_BLOCK
    s = ks_ref.shape[1]
    nc = kcmp_ref.shape[2]
    d = HEAD_DIM
    nq = Q_BLOCK

    @pl.when(qb == 0)
    def _():
        blk = lax.broadcasted_iota(jnp.int32, (s, MASK_W), 0) // SLC_BLOCK
        col = lax.broadcasted_iota(jnp.int32, (s, MASK_W), 1)
        onehot = jnp.where(blk == col, 1.0, 0.0).astype(BF16)
        for g in range(NSA_GROUPS):
            kaug_ref[g, :, 0:d] = ks_ref[0, :, g * d:(g + 1) * d]
            kaug_ref[g, :, d:d + MASK_W] = onehot
            vct_ref[g] = vcmp_ref[0, g].T.astype(BF16)

        def transpose_tile(j, carry):
            r0 = pl.multiple_of(j * SEL_TILE, SEL_TILE)
            vst_ref[j] = vs_ref[0, pl.ds(r0, SEL_TILE), :].astype(F32).T.astype(BF16)
            for half in range(SEL_TILE // KEY_TILE):
                r1 = pl.multiple_of(r0 + half * KEY_TILE, KEY_TILE)
                vwt_ref[j * (SEL_TILE // KEY_TILE) + half] = (
                    vw_ref[0, pl.ds(r1, KEY_TILE), :].astype(F32).T.astype(BF16))
            return carry

        lax.fori_loop(0, s // SEL_TILE, transpose_tile, 0)

    q_all = q_ref[0]
    gates_t = jax.nn.sigmoid(g_ref[0]).T
    lane_minus_sub = (lax.broadcasted_iota(jnp.int32, (KEY_TILE, nq), 1)
                      - lax.broadcasted_iota(jnp.int32, (KEY_TILE, nq), 0))
    w_start = jnp.maximum(s0 - WINDOW, 0)
    w_tile0 = w_start // KEY_TILE

    groups = range(NSA_GROUPS)
    heads = range(NSA_HEADS)
    grp = lambda h: h // NSA_HPG
    gd = [slice(g * d, (g + 1) * d) for g in groups]
    q_h = [q_all[:, h * d:(h + 1) * d] for h in heads]

    n_row = lax.broadcasted_iota(jnp.int32, (nc, NSA_HPG * nq), 0)
    t_col = s0 + lax.broadcasted_iota(jnp.int32, (nc, NSA_HPG * nq), 1) % nq
    valid_t = (CMP_STRIDE * n_row + CMP_BLOCK - 1) <= t_col
    j_idx = lax.broadcasted_iota(jnp.int32, (MASK_W, nq), 0)
    blk_t = (s0 + lax.broadcasted_iota(jnp.int32, (MASK_W, nq), 1)) // SLC_BLOCK
    causal = j_idx <= blk_t
    forced = (j_idx == 0) | (j_idx == blk_t)
    c2s = c2s_ref[...]
    o_ct, imp = [], []
    for g in groups:
        qg = jnp.concatenate(q_h[g * NSA_HPG:(g + 1) * NSA_HPG], axis=0)
        st = _dot_nt(kcmp_ref[0, g].astype(BF16), qg)
        stm = jnp.where(valid_t, st, NEG_BIG)
        et = jnp.exp2(stm - jnp.max(stm, axis=0, keepdims=True))
        pt = jnp.where(valid_t, et / jnp.sum(et, axis=0, keepdims=True), 0.0)
        o_ct.append(jnp.dot(vct_ref[g], pt.astype(BF16), preferred_element_type=F32))
        psum = pt[:, 0:nq]
        for r in range(1, NSA_HPG):
            psum = psum + pt[:, r * nq:(r + 1) * nq]
        p_hi = psum.astype(BF16)
        p_lo = (psum - p_hi.astype(F32)).astype(BF16)
        imp_g = (jnp.dot(c2s, p_hi, preferred_element_type=F32)
                 + jnp.dot(c2s, p_lo, preferred_element_type=F32))
        imp_g = jnp.where(forced, jnp.inf, jnp.where(causal, imp_g, -jnp.inf))
        imp_ref[g] = imp_g
        imp.append(imp_g)

    def rank_body(i2, cnts):
        cnts = list(cnts)
        for i in (2 * i2, 2 * i2 + 1):
            tie = jnp.where(j_idx > i, 1, 0)
            for g in groups:
                row = imp_ref[g, pl.ds(i, 1), :]
                cnts[g] = cnts[g] + jnp.where(row > imp[g], 1, jnp.where(row == imp[g], tie, 0))
        return tuple(cnts)

    n_seen = (s0 + nq - 1) // SLC_BLOCK + 1
    cnts = lax.fori_loop(0, n_seen // 2, rank_body,
                         tuple(jnp.zeros((MASK_W, nq), jnp.int32) for _ in groups))
    q_aug = []
    for g in groups:
        sel_t = jnp.where(causal, jnp.where(cnts[g] < top, 0.0, -MASK_BIG), -MASK_BIG)
        mterm = sel_t.T.astype(BF16)
        q_aug += [jnp.concatenate([q_h[g * NSA_HPG + r], mterm], axis=1) for r in range(NSA_HPG)]

    sub_tiles = SEL_TILE // KEY_TILE

    q_aug_g = [jnp.concatenate(q_aug[g * NSA_HPG:(g + 1) * NSA_HPG], axis=0) for g in groups]
    head_lanes = lambda h: slice((h % NSA_HPG) * nq, (h % NSA_HPG + 1) * nq)

    def sel_scores(i, dst_ref):
        for g in groups:
            kt = kaug_ref[g, pl.ds(pl.multiple_of(i * SEL_TILE, SEL_TILE), SEL_TILE), :]
            dst_ref[g] = _dot_nt(kt, q_aug_g[g])

    def sel_step(i, src_ref, carry, masked):
        k0 = i * SEL_TILE
        sc = [src_ref[grp(h), :, head_lanes(h)] for h in heads]
        if masked:
            sc = [jnp.concatenate(
                [jnp.where(lane_minus_sub + (s0 - k0 - j * KEY_TILE) >= 0,
                           x[j * KEY_TILE:(j + 1) * KEY_TILE], -MASK_BIG)
                 for j in range(sub_tiles)], axis=0) for x in sc]
        m_new = [jnp.maximum(carry[h][0], jnp.max(sc[h], axis=0, keepdims=True)) for h in heads]
        alpha = [jnp.exp2(carry[h][0] - m_new[h]) for h in heads]
        p = [jnp.exp2(sc[h] - m_new[h]) for h in heads]
        l_new = [alpha[h] * carry[h][1] + jnp.sum(p[h], axis=0, keepdims=True) for h in heads]
        pb = [jnp.concatenate([p[h].astype(BF16) for h in range(g * NSA_HPG, (g + 1) * NSA_HPG)], axis=1)
              for g in groups]
        pv = [jnp.dot(vst_ref[i, gd[g], :], pb[g], preferred_element_type=F32) for g in groups]
        return tuple((m_new[h], l_new[h], alpha[h] * carry[h][2] + pv[grp(h)][:, head_lanes(h)])
                     for h in heads)

    init = tuple((jnp.full((1, nq), M_INIT, F32), jnp.zeros((1, nq), F32), jnp.zeros((d, nq), F32))
                 for _ in heads)
    n_full = s0 // SEL_TILE

    sel_scores(0, sca_ref)

    zeros = jnp.zeros((nq, d), BF16)
    q_w = [jnp.concatenate([q_h[h], zeros] if grp(h) == 0 else [zeros, q_h[h]], axis=1)
           for h in heads]
    q_w_g = [jnp.concatenate(q_w[g * NSA_HPG:(g + 1) * NSA_HPG], axis=0) for g in groups]
    k_tiles = [kw_ref[0, pl.ds(pl.multiple_of(w_start + j * KEY_TILE, KEY_TILE), KEY_TILE), :]
               for j in range(WIN_TILES)]
    win_ok = [(lane_minus_sub + (s0 - w_start - j * KEY_TILE)).astype(jnp.uint32) < WINDOW
              for j in range(WIN_TILES)]

    def win_scores(g):
        out = []
        for j in range(WIN_TILES):
            sc = _dot_nt(k_tiles[j], q_w_g[g])
            out.append([jnp.where(win_ok[j], sc[:, r * nq:(r + 1) * nq], -MASK_BIG) for r in range(NSA_HPG)])
        return out

    def win_softmax(sc):
        p_all, l_all = [], []
        for r in range(NSA_HPG):
            m = sc[0][r]
            for j in range(1, WIN_TILES):
                m = jnp.maximum(m, sc[j][r])
            m = jnp.max(m, axis=0, keepdims=True)
            p = [jnp.exp2(sc[j][r] - m) for j in range(WIN_TILES)]
            tot = p[0]
            for x in p[1:]:
                tot = tot + x
            p_all.append([x.astype(BF16) for x in p])
            l_all.append(jnp.sum(tot, axis=0, keepdims=True))
        return p_all, l_all

    def win_out(g, pl_):
        p_all, l_all = pl_
        acc = None
        for j in range(WIN_TILES):
            pj = jnp.concatenate([p_all[r][j] for r in range(NSA_HPG)], axis=1)
            term = jnp.dot(vwt_ref[w_tile0 + j, gd[g], :], pj, preferred_element_type=F32)
            acc = term if acc is None else acc + term
        return [acc[:, r * nq:(r + 1) * nq] / l_all[r] for r in range(NSA_HPG)]

    sc_w = [win_scores(g) for g in groups]
    p_w = [win_softmax(sc_w[g]) for g in groups]
    o_wt = []
    for g in groups:
        o_wt += win_out(g, p_w[g])

    def sel_body(k, carry):
        sel_scores(2 * k + 1, scb_ref)
        carry = sel_step(2 * k, sca_ref, carry, masked=False)
        sel_scores(2 * k + 2, sca_ref)
        return sel_step(2 * k + 1, scb_ref, carry, masked=False)

    carry = lax.fori_loop(0, n_full // 2, sel_body, init)

    def odd_tail(carry):
        sel_scores(n_full, scb_ref)
        carry = sel_step(n_full - 1, sca_ref, carry, masked=False)
        sca_ref[...] = scb_ref[...]
        return carry

    carry = lax.cond(n_full % 2 == 1, odd_tail, lambda c: c, carry)
    carry = sel_step(n_full, sca_ref, carry, masked=True)
    o_st = [acc / l for (_, l, acc) in carry]

    for pair in range(NSA_HEADS // 2):
        halves = []
        for h in (2 * pair, 2 * pair + 1):
            r = h % NSA_HPG
            halves.append(gates_t[3 * h:3 * h + 1, :] * o_ct[grp(h)][:, r * nq:(r + 1) * nq]
                          + gates_t[3 * h + 1:3 * h + 2, :] * o_st[h]
                          + gates_t[3 * h + 2:3 * h + 3, :] * o_wt[h])
        o_ref[0, :, 2 * pair * d:(2 * pair + 2) * d] = jnp.concatenate(halves, axis=0).T.astype(o_ref.dtype)


def _cmp_to_slc_t(s):
    n_cmp = s // CMP_STRIDE
    n_slc = s // SLC_BLOCK
    cs = CMP_STRIDE * np.arange(n_cmp)[:, None]
    ss = SLC_BLOCK * np.arange(n_slc)[None, :]
    overlap = np.clip(np.minimum(cs + CMP_BLOCK, ss + SLC_BLOCK) - np.maximum(cs, ss), 0, None)
    m = np.zeros((MASK_W, n_cmp), np.float32)
    m[:n_slc] = (overlap / CMP_BLOCK).T
    return jnp.asarray(m).astype(BF16)


def _nsa(p, k_cmp, v_cmp):
    q = p["q"]
    bn, s, _ = q.shape
    assert s // SLC_BLOCK <= MASK_W and s % SEL_TILE == 0
    top = min(SLC_TOP, s // SLC_BLOCK)
    nc = k_cmp.shape[2]
    kvw = NSA_GROUPS * HEAD_DIM
    per_b = lambda w: pl.BlockSpec((1, s, w), lambda b, i: (b, 0, 0))
    return pl.pallas_call(
        functools.partial(_nsa_kernel, top=top),
        grid=(bn, s // Q_BLOCK),
        in_specs=[pl.BlockSpec((1, Q_BLOCK, NSA_HEADS * HEAD_DIM), lambda b, i: (b, i, 0)),
                  pl.BlockSpec((1, Q_BLOCK, LANES), lambda b, i: (b, i, 0)),
                  pl.BlockSpec((1, NSA_GROUPS, nc, HEAD_DIM), lambda b, i: (b, 0, 0, 0)),
                  pl.BlockSpec((1, NSA_GROUPS, nc, HEAD_DIM), lambda b, i: (b, 0, 0, 0)),
                  per_b(kvw), per_b(kvw), per_b(kvw), per_b(kvw),
                  pl.BlockSpec((MASK_W, nc), lambda b, i: (0, 0))],
        out_specs=pl.BlockSpec((1, Q_BLOCK, NSA_HEADS * HEAD_DIM), lambda b, i: (b, i, 0)),
        out_shape=jax.ShapeDtypeStruct((bn, s, NSA_HEADS * HEAD_DIM), BF16),
        scratch_shapes=[pltpu.VMEM((NSA_GROUPS, s, HEAD_DIM + MASK_W), BF16),
                        pltpu.VMEM((s // SEL_TILE, kvw, SEL_TILE), BF16),
                        pltpu.VMEM((s // KEY_TILE, kvw, KEY_TILE), BF16),
                        pltpu.VMEM((NSA_GROUPS, HEAD_DIM, nc), BF16),
                        pltpu.VMEM((NSA_GROUPS, SEL_TILE, NSA_HPG * Q_BLOCK), F32),
                        pltpu.VMEM((NSA_GROUPS, SEL_TILE, NSA_HPG * Q_BLOCK), F32),
                        pltpu.VMEM((NSA_GROUPS, MASK_W, Q_BLOCK), F32)],
        compiler_params=pltpu.CompilerParams(dimension_semantics=("parallel", "arbitrary"),
                                             vmem_limit_bytes=VMEM_LIMIT),
        name="nsa_attention",
    )(q, p["g"], k_cmp, v_cmp, p["ks"], p["vs"], p["kw"], p["vw"], _cmp_to_slc_t(s))


S5_STRIP = 512


def _gelu_tanh(y):
    return 0.5 * y * (1.0 + jnp.tanh(math.sqrt(2.0 / math.pi) * (y + 0.044715 * (y * y * y))))


def _s5_kernel(u_ref, bre_ref, bim_ref, are_ref, aim_ref, cre_ref, cim_ref, d_ref, gw_ref, gb_ref,
               o_ref, xre_ref, xim_ref, sre_ref, sim_ref):
    bn, tc, width = u_ref.shape
    nstate = are_ref.shape[1]

    @pl.when(pl.program_id(0) == 0)
    def _():
        sre_ref[...] = jnp.zeros_like(sre_ref)
        sim_ref[...] = jnp.zeros_like(sim_ref)

    u = pltpu.einshape("btc->(tb)c", u_ref[...].astype(F32))
    ub = u.astype(BF16)
    n_strips = nstate // S5_STRIP
    cw = width // n_strips
    for k in range(n_strips):
        cols = pl.ds(k * S5_STRIP, S5_STRIP)
        xre_ref[:, cols] = jnp.dot(ub[:, k * cw:(k + 1) * cw], bre_ref[k], preferred_element_type=F32)
        xim_ref[:, cols] = jnp.dot(ub[:, k * cw:(k + 1) * cw], bim_ref[k], preferred_element_type=F32)

    for k in range(n_strips):
        c0 = k * S5_STRIP
        cols = pl.ds(c0, S5_STRIP)
        a_r = jnp.broadcast_to(are_ref[:, c0:c0 + S5_STRIP], (bn, S5_STRIP))
        a_i = jnp.broadcast_to(aim_ref[:, c0:c0 + S5_STRIP], (bn, S5_STRIP))

        def step(t, carry):
            x_r, x_i = carry
            r0 = pl.multiple_of(t * bn, bn)
            n_r = a_r * x_r - a_i * x_i + xre_ref[pl.ds(r0, bn), cols]
            n_i = a_r * x_i + a_i * x_r + xim_ref[pl.ds(r0, bn), cols]
            xre_ref[pl.ds(r0, bn), cols] = n_r
            xim_ref[pl.ds(r0, bn), cols] = n_i
            return n_r, n_i

        x_r, x_i = lax.fori_loop(0, tc, step, (sre_ref[:, cols], sim_ref[:, cols]), unroll=8)
        sre_ref[:, cols] = x_r
        sim_ref[:, cols] = x_i

    y = jnp.concatenate(
        [jnp.dot(xre_ref[:, k * S5_STRIP:(k + 1) * S5_STRIP].astype(BF16), cre_ref[k],
                 preferred_element_type=F32)
         + jnp.dot(xim_ref[:, k * S5_STRIP:(k + 1) * S5_STRIP].astype(BF16), cim_ref[k],
                   preferred_element_type=F32) for k in range(n_strips)], axis=1) + d_ref[...] * u
    y = _gelu_tanh(y)
    z = _bdot(y, gw_ref[...]) + gb_ref[...]
    o_ref[...] = pltpu.einshape("(tb)c->btc", y * jax.nn.sigmoid(z), b=bn).astype(o_ref.dtype)


def _s5_params(a_re, a_im, b_re, b_im, c_re, c_im, log_dt):
    g, p = a_re.shape
    c = b_re.shape[-1]
    a_re, a_im = a_re.astype(F32), a_im.astype(F32)
    b_re, b_im = b_re.astype(F32), b_im.astype(F32)
    dt = jnp.exp(log_dt.astype(F32))[:, None]
    mag = jnp.exp(a_re * dt)
    abar_re, abar_im = mag * jnp.cos(a_im * dt), mag * jnp.sin(a_im * dt)
    den = a_re * a_re + a_im * a_im
    f_re = ((abar_re - 1.0) * a_re + abar_im * a_im) / den
    f_im = (abar_im * a_re - (abar_re - 1.0) * a_im) / den
    bbar_re = f_re[..., None] * b_re - f_im[..., None] * b_im
    bbar_im = f_re[..., None] * b_im + f_im[..., None] * b_re
    gs = S5_STRIP // p
    nb = g // gs
    eye = jnp.eye(gs, dtype=F32)
    pack_b = lambda m: jnp.einsum("kgpc,gh->kgchp", m.reshape(nb, gs, p, c), eye).reshape(
        nb, gs * c, gs * p).astype(BF16)
    pack_c = lambda m: jnp.einsum("kgcp,gh->kgphc", m.reshape(nb, gs, c, p), eye).reshape(
        nb, gs * p, gs * c).astype(BF16)
    return (pack_b(bbar_re), pack_b(bbar_im), abar_re.reshape(1, g * p), abar_im.reshape(1, g * p),
            pack_c(c_re.astype(F32)), pack_c(-c_im.astype(F32)))


def _s5(u, params, d, glu_w, glu_b, tc=64):
    bn, s, width = u.shape
    assert bn == SUBLANES
    bre, bim, are, aim, cre, cim = params
    nstate = are.shape[1]
    full = lambda a: pl.BlockSpec(a.shape, lambda i: tuple(0 for _ in a.shape))
    d2, gb2, gwb = d.reshape(1, width), glu_b.reshape(1, width), glu_w.astype(BF16)
    return pl.pallas_call(
        _s5_kernel,
        grid=(s // tc,),
        in_specs=[pl.BlockSpec((bn, tc, width), lambda i: (0, i, 0)),
                  full(bre), full(bim), full(are), full(aim), full(cre), full(cim),
                  full(d2), full(gwb), full(gb2)],
        out_specs=pl.BlockSpec((bn, tc, width), lambda i: (0, i, 0)),
        out_shape=jax.ShapeDtypeStruct((bn, s, width), BF16),
        scratch_shapes=[pltpu.VMEM((tc * bn, nstate), F32), pltpu.VMEM((tc * bn, nstate), F32),
                        pltpu.VMEM((bn, nstate), F32), pltpu.VMEM((bn, nstate), F32)],
        compiler_params=pltpu.CompilerParams(dimension_semantics=("arbitrary",),
                                             vmem_limit_bytes=VMEM_LIMIT),
        name="s5_scan",
    )(u, bre, bim, are, aim, cre, cim, d2, gwb, gb2)


RWKV_CHUNK = 64
RWKV_BLOCK = 256


def _split_dot(x, seg):
    hi = x.astype(BF16)
    lo = (x - hi.astype(F32)).astype(BF16)
    return (jnp.dot(hi, seg, preferred_element_type=F32) + jnp.dot(lo, seg, preferred_element_type=F32))


def _rwkv_kernel(mix_ref, mu_ref, w0_ref, w2_ref, a0_ref, a2_ref, kk_ref, ka_ref, rk_ref,
                 lnw_ref, lnb_ref, seg_ref, o_ref, state_ref, carry_ref, y_ref):
    n = HEAD_DIM
    chunk = RWKV_CHUNK
    tb = mix_ref.shape[1]
    w = BRANCH_W

    @pl.when(pl.program_id(1) == 0)
    def _():
        state_ref[...] = jnp.zeros_like(state_ref)
        carry_ref[...] = jnp.zeros_like(carry_ref)

    mix = mix_ref[0]
    row_m = lax.broadcasted_iota(jnp.int32, mix.shape, 0)
    prev = jnp.where(row_m == 0, carry_ref[0:1, :], pltpu.roll(mix, 1, 0))
    carry_ref[0:1, :] = mix[tb - 1:tb, :]
    xs = mix + (prev - mix) * mu_ref[...]
    r, k, v = xs[:, 0:w], xs[:, w:2 * w], xs[:, 2 * w:3 * w]
    wl, al = xs[:, 3 * w:3 * w + LORA_W], xs[:, 3 * w + LORA_W:3 * w + 2 * LORA_W]

    lw = w0_ref[...] + _bdot(jnp.tanh(wl), w2_ref[...])
    z = -lw
    softplus = jnp.maximum(z, 0.0) + jnp.log(1.0 + jnp.exp(-jnp.abs(z)))
    ld = -jnp.exp(-softplus - 0.5)
    a = jax.nn.sigmoid(a0_ref[...] + _bdot(al, a2_ref[...]))
    seg = seg_ref[...]
    kk = k * kk_ref[...]
    kk = kk / jnp.maximum(jnp.sqrt(_split_dot(kk * kk, seg)), 1e-12)
    k2 = k * (1.0 + (a - 1.0) * ka_ref[...])
    aa = -kk
    bb = kk * a

    row = lax.broadcasted_iota(jnp.int32, (tb, w), 0) % chunk
    cum = ld
    sh = 1
    while sh < chunk:
        cum = cum + jnp.where(row >= sh, pltpu.roll(cum, sh, 0), 0.0)
        sh *= 2
    at = (aa * jnp.exp(cum - ld)).astype(BF16)
    rt = (r * jnp.exp(cum)).astype(BF16)
    e_neg = jnp.exp(-cum)
    bt, kt = (bb * e_neg).astype(BF16), (k2 * e_neg).astype(BF16)
    vb = v.astype(BF16)

    ti = lax.broadcasted_iota(jnp.int32, (2 * chunk, chunk), 0)
    si = lax.broadcasted_iota(jnp.int32, (2 * chunk, chunk), 1)
    strict = si < ti
    lower2 = si < jnp.where(ti < chunk, ti, ti - chunk + 1)
    n_double = int(math.log2(chunk)) - 1

    n_chunks = tb // chunk
    pairs = [(ci, h) for ci in range(n_chunks) for h in range(RWKV_HEADS)]
    rows_of = lambda ci: slice(ci * chunk, (ci + 1) * chunk)
    lanes_of = lambda h: slice(h * n, (h + 1) * n)
    bdot32 = lambda p_, q_: jnp.dot(p_, q_, preferred_element_type=F32)

    at_l = [at[rows_of(ci), lanes_of(h)] for ci, h in pairs]
    rt_l = [rt[rows_of(ci), lanes_of(h)] for ci, h in pairs]
    v_l = [vb[rows_of(ci), lanes_of(h)] for ci, h in pairs]
    amat_l = [_dot_nt(jnp.concatenate([at_l[i], rt_l[i]], axis=0),
                      jnp.concatenate([bt[rows_of(ci), lanes_of(h)], kt[rows_of(ci), lanes_of(h)]], axis=0))
              for i, (ci, h) in enumerate(pairs)]
    pw_l = [jnp.where(strict[:chunk], m[:chunk, :chunk], 0.0).astype(BF16) for m in amat_l]
    arb_l = [jnp.where(lower2[chunk:], m[chunk:, :chunk], 0.0).astype(BF16) for m in amat_l]
    axk_l = [jnp.where(lower2, m[:, chunk:], 0.0).astype(BF16) for m in amat_l]
    xv_l = [bdot32(axk_l[i], v_l[i]) for i in range(len(pairs))]
    x_l = [jnp.concatenate([at_l[i].astype(F32), xv_l[i][:chunk]], axis=1) for i in range(len(pairs))]
    x_l = [x + bdot32(pw, x.astype(BF16)) for x, pw in zip(x_l, pw_l)]
    for _ in range(n_double):
        pw_l = [bdot32(pw, pw).astype(BF16) for pw in pw_l]
        x_l = [x + bdot32(pw, x.astype(BF16)) for x, pw in zip(x_l, pw_l)]
    ro_l = [jnp.concatenate([rt_l[i].astype(F32), xv_l[i][chunk:]], axis=1)
            + bdot32(arb_l[i], x_l[i].astype(BF16)) for i in range(len(pairs))]
    wr_l = [jnp.concatenate([x[:, :n], ro[:, :n]], axis=0).astype(BF16) for x, ro in zip(x_l, ro_l)]

    state = [state_ref[h] for h in range(RWKV_HEADS)]
    for ci in range(n_chunks):
        rows = rows_of(ci)
        cum_c = cum[rows]
        tot = cum_c[chunk - 1:chunk, :]
        e_rem = jnp.exp(tot - cum_c)
        bh, kh = (bb[rows] * e_rem).astype(BF16), (k2[rows] * e_rem).astype(BF16)
        p_tot = jnp.exp(tot)
        base = ci * RWKV_HEADS
        g_l = [_dot_nt(wr_l[base + h], state[h].astype(BF16)) for h in range(RWKV_HEADS)]
        for h in range(RWKV_HEADS):
            y_ref[rows, lanes_of(h)] = g_l[h][chunk:] + ro_l[base + h][:, n:]
        uv_l = [jnp.concatenate([g_l[h][:chunk] + x_l[base + h][:, n:], v_l[base + h].astype(F32)], axis=0)
                for h in range(RWKV_HEADS)]
        state = [state[h] * p_tot[:, lanes_of(h)]
                 + bdot32(uv_l[h].T.astype(BF16),
                          jnp.concatenate([bh[:, lanes_of(h)], kh[:, lanes_of(h)]], axis=0))
                 for h in range(RWKV_HEADS)]
    for h in range(RWKV_HEADS):
        state_ref[h] = state[h]

    y = y_ref[...]
    inv_n = 1.0 / n
    mean = _split_dot(y, seg) * inv_n
    dev = y - mean
    var = _split_dot(dev * dev, seg) * inv_n
    yn = dev * lax.rsqrt(var + RWKV_LN_EPS) * lnw_ref[...] + lnb_ref[...]
    bonus = _split_dot(r * k2 * rk_ref[...], seg) * v
    o_ref[0] = (yn + bonus).astype(o_ref.dtype)


def _rwkv(mix, mu, w0, w2, a0, a2, k_k, k_a, r_k, ln_w, ln_b):
    bn, s, mw = mix.shape
    w = BRANCH_W
    tb = min(RWKV_BLOCK, s)
    head = np.arange(w) // HEAD_DIM
    seg = jnp.asarray((head[:, None] == head[None, :]).astype(np.float32)).astype(BF16)
    row = lambda t: t.reshape(1, -1).astype(F32)
    args = (row(mu), row(w0), w2.astype(BF16), row(a0), a2.astype(BF16), row(k_k), row(k_a), row(r_k),
            row(ln_w), row(ln_b), seg)
    full = lambda a: pl.BlockSpec(a.shape, lambda b, i: tuple(0 for _ in a.shape))
    return pl.pallas_call(
        _rwkv_kernel,
        grid=(bn, s // tb),
        in_specs=[pl.BlockSpec((1, tb, mw), lambda b, i: (b, i, 0))] + [full(t) for t in args],
        out_specs=pl.BlockSpec((1, tb, w), lambda b, i: (b, i, 0)),
        out_shape=jax.ShapeDtypeStruct((bn, s, w), BF16),
        scratch_shapes=[pltpu.VMEM((RWKV_HEADS, HEAD_DIM, HEAD_DIM), F32),
                        pltpu.VMEM((SUBLANES, mw), F32),
                        pltpu.VMEM((tb, w), F32)],
        compiler_params=pltpu.CompilerParams(dimension_semantics=("parallel", "arbitrary"),
                                             vmem_limit_bytes=VMEM_LIMIT),
        name="rwkv7",
    )(mix, *args)


def _merge_kernel(on_ref, os_ref, or_ref, gn_ref, gs_ref, gr_ref, mg_ref, x_ref, gate_ref,
                  wup_ref, wout_ref, fnw_ref, o_ref, *, final):
    d = x_ref.shape[2]
    merged = None
    for i, (b_ref, g_ref) in enumerate(((on_ref, gn_ref), (os_ref, gs_ref), (or_ref, gr_ref))):
        branch = b_ref[0].astype(F32) * _silu(g_ref[0].astype(F32))
        up = _bdot(branch, wup_ref[i])
        term = jax.nn.sigmoid(mg_ref[0, :, i * d:(i + 1) * d].astype(F32)) * up
        merged = term if merged is None else merged + term
    out = x_ref[0] + gate_ref[0] * _bdot(merged, wout_ref[...])
    if final:
        ms = jnp.mean(out * out, axis=-1, keepdims=True)
        out = out * lax.rsqrt(ms + NORM_EPS) * fnw_ref[...]
    o_ref[0] = out


def _merge(o_nsa, o_s5, o_rwkv, p, x, gate, w_up, w_out, fnw, final, tm=512):
    bn, s, d = x.shape
    w = BRANCH_W
    rows = lambda width: pl.BlockSpec((1, tm, width), lambda b, i: (b, i, 0))
    wupb, woutb = w_up.astype(BF16), w_out.astype(BF16)
    return pl.pallas_call(
        functools.partial(_merge_kernel, final=final),
        grid=(bn, s // tm),
        in_specs=[rows(w)] * 6 + [rows(3 * d), rows(d),
                                  pl.BlockSpec((1, 1, d), lambda b, i: (b, 0, 0)),
                                  pl.BlockSpec(wupb.shape, lambda b, i: (0, 0, 0)),
                                  pl.BlockSpec(woutb.shape, lambda b, i: (0, 0)),
                                  pl.BlockSpec((1, d), lambda b, i: (0, 0))],
        out_specs=rows(d),
        out_shape=jax.ShapeDtypeStruct((bn, s, d), F32),
        compiler_params=pltpu.CompilerParams(dimension_semantics=("parallel", "parallel"),
                                             vmem_limit_bytes=VMEM_LIMIT),
        name="merge_out",
    )(o_nsa, o_s5, o_rwkv, p["ng"], p["sg"], p["rg"], p["mg"], x, gate.reshape(bn, 1, d),
      wupb, woutb, fnw.reshape(1, d))


def kernel(x, c, norm_w, mod_w, mod_b, w_in, cmp_pos_k, cmp_pos_v, cmp_w1_k, cmp_w2_k, cmp_w1_v, cmp_w2_v, s5_a_re, s5_a_im, s5_b_re, s5_b_im, s5_c_re, s5_c_im, s5_d, s5_log_dt, s5_glu_w, s5_glu_b, rwkv_mu, rwkv_w0, rwkv_w2, rwkv_a0, rwkv_a2, rwkv_k_k, rwkv_k_a, rwkv_r_k, rwkv_ln_w, rwkv_ln_b, w_up, w_out, final_norm_w):
    bn, s, d = x.shape
    depth = norm_w.shape[0]
    cos_t, sin_t = _rope_tables(s)
    mod = _modulation(c, mod_w, mod_b)
    for l in range(depth):
        shift, scale, gate = mod[l, :, 0:d], mod[l, :, d:2 * d], mod[l, :, 2 * d:3 * d]
        p = _inproj(x, norm_w[l], scale, shift, cos_t, sin_t, _pack_w_in(w_in[l]))
        k_cmp, v_cmp = _compress(p["kc"], p["vc"], cmp_pos_k[l], cmp_pos_v[l], cmp_w1_k[l], cmp_w2_k[l],
                                 cmp_w1_v[l], cmp_w2_v[l])
        o_nsa = _nsa(p, k_cmp, v_cmp)
        s5p = _s5_params(s5_a_re[l], s5_a_im[l], s5_b_re[l], s5_b_im[l], s5_c_re[l], s5_c_im[l],
                         s5_log_dt[l])
        o_s5 = _s5(p["su"], s5p, s5_d[l], s5_glu_w[l], s5_glu_b[l])
        o_rwkv = _rwkv(p["mix"], rwkv_mu[l], rwkv_w0[l], rwkv_w2[l], rwkv_a0[l], rwkv_a2[l],
                       rwkv_k_k[l], rwkv_k_a[l], rwkv_r_k[l], rwkv_ln_w[l], rwkv_ln_b[l])
        x = _merge(o_nsa, o_s5, o_rwkv, p, x, gate, w_up[l], w_out[l], final_norm_w,
                   final=(l == depth - 1))
    return x
```

```python
import functools
import math

import numpy as np
import jax
import jax.numpy as jnp
from jax import lax
from jax.experimental import pallas as pl
from jax.experimental.pallas import tpu as pltpu

F32 = jnp.float32
BF16 = jnp.bfloat16
HIGHEST = lax.Precision.HIGHEST

HEAD_DIM = 64
NSA_HEADS = 8
NSA_GROUPS = 2
NSA_HPG = NSA_HEADS // NSA_GROUPS
CMP_BLOCK = 32
CMP_STRIDE = 16
SLC_BLOCK = 64
SLC_TOP = 16
WINDOW = 512
Q_BLOCK = 128
S5_GROUPS = 32
S5_GROUP_CH = 16
S5_STATE = 64
RWKV_HEADS = 8
BRANCH_W = 512
LORA_W = 64
RWKV_MIX_W = 3 * BRANCH_W + 2 * LORA_W
ROPE_THETA = 10000.0
NORM_EPS = 1e-6
RWKV_LN_EPS = 64e-5
NEG_BIG = -1e30
MASK_BIG = 2.0 ** 100
M_INIT = -3.0e38
Q_SCALE = HEAD_DIM ** -0.5 * math.log2(math.e)

VMEM_LIMIT = 56 * 1024 * 1024
LANES = 128
SUBLANES = 8

C_Q = 0
C_KC, C_KS, C_KW, C_VC, C_VS, C_VW = 512, 640, 768, 896, 1024, 1152
C_G = 1280
C_NG = 1408
C_SU = 1920
C_SG = 2432
C_MIX = 2944
C_RG = C_MIX + RWKV_MIX_W
C_MG = C_RG + BRANCH_W
IN_PACKED = C_MG + 3 * 1024


def _sigmoid(z):
    return 0.5 * jnp.tanh(0.5 * z) + 0.5


def _silu(z):
    return z * _sigmoid(z)


def _bdot(a, b):
    return jnp.dot(a.astype(BF16), b.astype(BF16), preferred_element_type=F32)


def _dot_nt(a, b, precision=None):
    return lax.dot_general(a, b, (((1,), (1,)), ((), ())), precision=precision,
                           preferred_element_type=F32)


def _hdot(a, b):
    return jnp.dot(a, b, precision=HIGHEST, preferred_element_type=F32)


def _mod_kernel(c_ref, w_ref, b_ref, o_ref):
    cond = _silu(c_ref[...])
    o_ref[0] = _bdot(cond, w_ref[0]) + b_ref[0]


def _modulation(c, mod_w, mod_b):
    depth, d, d3 = mod_w.shape
    bn = c.shape[0]
    nj = d3 // d
    return pl.pallas_call(
        _mod_kernel,
        grid=(depth, nj),
        in_specs=[pl.BlockSpec((bn, d), lambda l, j: (0, 0)),
                  pl.BlockSpec((1, d, d), lambda l, j: (l, 0, j)),
                  pl.BlockSpec((1, 1, d), lambda l, j: (l, 0, j))],
        out_specs=pl.BlockSpec((1, bn, d), lambda l, j: (l, 0, j)),
        out_shape=jax.ShapeDtypeStruct((depth, bn, d3), F32),
        name="adaln_mod",
    )(c, mod_w, mod_b.reshape(depth, 1, d3))


_INPROJ_OUTS = (
    ("q", C_Q, 512, BF16), ("kc", C_KC, 128, F32), ("ks", C_KS, 128, BF16), ("kw", C_KW, 128, BF16),
    ("vc", C_VC, 128, F32), ("vs", C_VS, 128, BF16), ("vw", C_VW, 128, BF16), ("g", C_G, 128, F32),
    ("ng", C_NG, 512, BF16), ("su", C_SU, 512, BF16), ("sg", C_SG, 512, BF16),
    ("mix", C_MIX, RWKV_MIX_W, F32), ("rg", C_RG, 512, BF16), ("mg", C_MG, 3072, BF16))
_ROPED = ("q", "kc", "ks", "kw")


def _inproj_kernel(x_ref, nw_ref, sc_ref, sh_ref, cos_ref, sin_ref, w_ref, *out_refs):
    x = x_ref[0]
    tm = x.shape[0]
    ms = jnp.mean(x * x, axis=-1, keepdims=True)
    h = x * lax.rsqrt(ms + NORM_EPS) * nw_ref[...]
    h = h * (1.0 + sc_ref[0]) + sh_ref[0]
    hb = h.astype(BF16)
    cos = cos_ref[...]
    sin = sin_ref[...]
    lane = lax.broadcasted_iota(jnp.int32, (tm, LANES), 1)
    first_half = (lane % HEAD_DIM) < (HEAD_DIM // 2)

    def rope(t):
        partner = jnp.where(first_half, pltpu.roll(t, 96, 1), pltpu.roll(t, 32, 1))
        return t * cos + partner * sin

    for (name, c0, width, dt), o_ref in zip(_INPROJ_OUTS, out_refs):
        step = min(width, 512)
        for j0 in range(0, width, step):
            w = min(step, width - j0)
            y = jnp.dot(hb, w_ref[:, c0 + j0:c0 + j0 + w], preferred_element_type=F32)
            if name in _ROPED:
                y = jnp.concatenate([rope(y[:, k:k + LANES]) for k in range(0, w, LANES)], axis=1)
            if name == "q":
                y = y * Q_SCALE
            o_ref[0, :, j0:j0 + w] = y.astype(dt)


def _inproj(x, nw, scale, shift, cos_t, sin_t, w_packed, tm=512):
    bn, s, d = x.shape
    out_shape = [jax.ShapeDtypeStruct((bn, s, w), dt) for (_, _, w, dt) in _INPROJ_OUTS]
    out_specs = [pl.BlockSpec((1, tm, w), lambda b, i: (b, i, 0)) for (_, _, w, _) in _INPROJ_OUTS]
    outs = pl.pallas_call(
        _inproj_kernel,
        grid=(bn, s // tm),
        in_specs=[pl.BlockSpec((1, tm, d), lambda b, i: (b, i, 0)),
                  pl.BlockSpec((1, d), lambda b, i: (0, 0)),
                  pl.BlockSpec((1, 1, d), lambda b, i: (b, 0, 0)),
                  pl.BlockSpec((1, 1, d), lambda b, i: (b, 0, 0)),
                  pl.BlockSpec((tm, LANES), lambda b, i: (i, 0)),
                  pl.BlockSpec((tm, LANES), lambda b, i: (i, 0)),
                  pl.BlockSpec((d, IN_PACKED), lambda b, i: (0, 0), pipeline_mode=pl.Buffered(1))],
        out_specs=out_specs,
        out_shape=out_shape,
        compiler_params=pltpu.CompilerParams(dimension_semantics=("parallel", "parallel"),
                                             vmem_limit_bytes=VMEM_LIMIT),
        name="inproj",
    )(x, nw.reshape(1, d), scale.reshape(bn, 1, d), shift.reshape(bn, 1, d), cos_t, sin_t, w_packed)
    return dict(zip([o[0] for o in _INPROJ_OUTS], outs))


def _pack_w_in(w_in):
    d = w_in.shape[0]
    sizes = (512, 768, 24, 512, 512, 512, RWKV_MIX_W, 512, 3072)
    offs = np.concatenate([[0], np.cumsum(sizes)])
    q, kv, g, ng, su, sg, mix, rg, mg = [w_in[:, offs[i]:offs[i + 1]] for i in range(len(sizes))]
    kc, vc, ks, vs, kw, vw = [kv[:, i * 128:(i + 1) * 128] for i in range(6)]
    gpad = jnp.pad(g, ((0, 0), (0, 128 - 24)))
    return jnp.concatenate([q, kc, ks, kw, vc, vs, vw, gpad, ng, su, sg, mix, rg, mg], axis=1).astype(BF16)


def _rope_tables(s):
    half = HEAD_DIM // 2
    inv = jnp.exp(-math.log(ROPE_THETA) * jnp.arange(half, dtype=F32) / half)
    ang = jnp.arange(s, dtype=F32)[:, None] * inv[None, :]
    cos, sin = jnp.cos(ang), jnp.sin(ang)
    cos_t = jnp.tile(cos, (1, LANES // half))
    sin_t = jnp.tile(jnp.concatenate([-sin, sin], axis=1), (1, LANES // HEAD_DIM))
    return cos_t, sin_t


def _compress_kernel(kc_ref, vc_ref, pk_ref, pv_ref, wkt_ref, wkb_ref, wvt_ref, wvb_ref,
                     w2k_ref, w2v_ref, ko_ref, vo_ref):
    def one(x_ref, p_ref, wt_ref, wb_ref, w2_ref, o_ref):
        kvw = x_ref.shape[2]
        n = x_ref.shape[1] // CMP_STRIDE
        top = bot = None
        for l in range(CMP_STRIDE):
            x_l = x_ref[0, pl.ds(l, n, stride=CMP_STRIDE), :]
            cols = slice(l * kvw, (l + 1) * kvw)
            t = _bdot(x_l + p_ref[0:1, cols], wt_ref[cols, :])
            b = _bdot(x_l + p_ref[1:2, cols], wb_ref[cols, :])
            top, bot = (t, b) if top is None else (top + t, bot + b)
        hid = top + pltpu.roll(bot, n - 1, 0)
        act = _silu(hid)
        hw = act.shape[1] // NSA_GROUPS
        for g in range(NSA_GROUPS):
            o_ref[0, g] = _bdot(act[:, g * hw:(g + 1) * hw], w2_ref[...])

    one(kc_ref, pk_ref, wkt_ref, wkb_ref, w2k_ref, ko_ref)
    one(vc_ref, pv_ref, wvt_ref, wvb_ref, w2v_ref, vo_ref)


def _compress_weights(pos, w1, w2):
    hid = w1.shape[1]
    half = CMP_BLOCK // 2
    w1r = w1.reshape(2, half, HEAD_DIM, hid)
    eye = jnp.eye(NSA_GROUPS, dtype=w1.dtype)
    wd = jnp.einsum("tldj,gh->tlgdhj", w1r, eye).reshape(2, half * NSA_GROUPS * HEAD_DIM,
                                                         NSA_GROUPS * hid)
    pr = pos.reshape(2, half, 1, HEAD_DIM)
    pt = jnp.broadcast_to(pr, (2, half, NSA_GROUPS, HEAD_DIM)).reshape(2, -1)
    return pt, wd[0].astype(BF16), wd[1].astype(BF16), w2.astype(BF16)


def _compress(kc, vc, pos_k, pos_v, w1k, w2k, w1v, w2v):
    bn, s, kvw = kc.shape
    n16 = s // CMP_STRIDE
    row_w = CMP_STRIDE * kvw
    pk, wkt, wkb, w2kb = _compress_weights(pos_k, w1k, w2k)
    pv, wvt, wvb, w2vb = _compress_weights(pos_v, w1v, w2v)
    hid2 = wkt.shape[1]
    full = lambda shape: pl.BlockSpec(shape, lambda b: tuple(0 for _ in shape))
    out = jax.ShapeDtypeStruct((bn, NSA_GROUPS, n16, HEAD_DIM), F32)
    return pl.pallas_call(
        _compress_kernel,
        grid=(bn,),
        in_specs=[pl.BlockSpec((1, s, kvw), lambda b: (b, 0, 0)),
                  pl.BlockSpec((1, s, kvw), lambda b: (b, 0, 0)),
                  full((2, row_w)), full((2, row_w)),
                  full((row_w, hid2)), full((row_w, hid2)), full((row_w, hid2)), full((row_w, hid2)),
                  full(w2kb.shape), full(w2vb.shape)],
        out_specs=[pl.BlockSpec((1, NSA_GROUPS, n16, HEAD_DIM), lambda b: (b, 0, 0, 0))] * 2,
        out_shape=[out, out],
        compiler_params=pltpu.CompilerParams(dimension_semantics=("parallel",),
                                             vmem_limit_bytes=VMEM_LIMIT),
        name="nsa_compress",
    )(kc, vc, pk, pv, wkt, wkb, wvt, wvb, w2kb, w2vb)


SEL_TILE = 256
KEY_TILE = 128
WIN_TILES = WINDOW // KEY_TILE + 1
MASK_W = 64


def _nsa_kernel(q_ref, g_ref, kcmp_ref, vcmp_ref, ks_ref, vs_ref, kw_ref, vw_ref, c2s_ref,
                o_ref, kaug_ref, vst_ref, vwt_ref, vct_ref, sca_ref, scb_ref, imp_ref, *, top):
    qb = pl.program_id(1)
    s0 = qb * Q_BLOCK
    s = ks_ref.shape[1]
    nc = kcmp_ref.shape[2]
    d = HEAD_DIM
    nq = Q_BLOCK

    @pl.when(qb == 0)
    def _():
        blk = lax.broadcasted_iota(jnp.int32, (s, MASK_W), 0) // SLC_BLOCK
        col = lax.broadcasted_iota(jnp.int32, (s, MASK_W), 1)
        onehot = jnp.where(blk == col, 1.0, 0.0).astype(BF16)
        for g in range(NSA_GROUPS):
            kaug_ref[g, :, 0:d] = ks_ref[0, :, g * d:(g + 1) * d]
            kaug_ref[g, :, d:d + MASK_W] = onehot
            vct_ref[g] = vcmp_ref[0, g].T.astype(BF16)

        def transpose_tile(j, carry):
            r0 = pl.multiple_of(j * SEL_TILE, SEL_TILE)
            vst_ref[j] = vs_ref[0, pl.ds(r0, SEL_TILE), :].astype(F32).T.astype(BF16)
            for half in range(SEL_TILE // KEY_TILE):
                r1 = pl.multiple_of(r0 + half * KEY_TILE, KEY_TILE)
                vwt_ref[j * (SEL_TILE // KEY_TILE) + half] = (
                    vw_ref[0, pl.ds(r1, KEY_TILE), :].astype(F32).T.astype(BF16))
            return carry

        lax.fori_loop(0, s // SEL_TILE, transpose_tile, 0)

    q_all = q_ref[0]
    gates_t = _sigmoid(g_ref[0]).T
    lane_minus_sub = (lax.broadcasted_iota(jnp.int32, (KEY_TILE, nq), 1)
                      - lax.broadcasted_iota(jnp.int32, (KEY_TILE, nq), 0))
    w_start = jnp.maximum(s0 - WINDOW, 0)
    w_tile0 = w_start // KEY_TILE

    groups = range(NSA_GROUPS)
    heads = range(NSA_HEADS)
    grp = lambda h: h // NSA_HPG
    gd = [slice(g * d, (g + 1) * d) for g in groups]
    q_h = [q_all[:, h * d:(h + 1) * d] for h in heads]

    n_row = lax.broadcasted_iota(jnp.int32, (nc, NSA_HPG * nq), 0)
    t_col = s0 + lax.broadcasted_iota(jnp.int32, (nc, NSA_HPG * nq), 1) % nq
    valid_t = (CMP_STRIDE * n_row + CMP_BLOCK - 1) <= t_col
    j_idx = lax.broadcasted_iota(jnp.int32, (MASK_W, nq), 0)
    blk_t = (s0 + lax.broadcasted_iota(jnp.int32, (MASK_W, nq), 1)) // SLC_BLOCK
    causal = j_idx <= blk_t
    forced = (j_idx == 0) | (j_idx == blk_t)
    c2s = c2s_ref[...]
    o_ct, imp = [], []
    for g in groups:
        qg = jnp.concatenate(q_h[g * NSA_HPG:(g + 1) * NSA_HPG], axis=0)
        st = _dot_nt(kcmp_ref[0, g].astype(BF16), qg)
        stm = jnp.where(valid_t, st, NEG_BIG)
        et = jnp.exp2(stm - jnp.max(stm, axis=0, keepdims=True))
        pt = jnp.where(valid_t, et / jnp.sum(et, axis=0, keepdims=True), 0.0)
        o_ct.append(jnp.dot(vct_ref[g], pt.astype(BF16), preferred_element_type=F32))
        psum = pt[:, 0:nq]
        for r in range(1, NSA_HPG):
            psum = psum + pt[:, r * nq:(r + 1) * nq]
        p_hi = psum.astype(BF16)
        p_lo = (psum - p_hi.astype(F32)).astype(BF16)
        imp_g = (jnp.dot(c2s, p_hi, preferred_element_type=F32)
                 + jnp.dot(c2s, p_lo, preferred_element_type=F32))
        imp_g = jnp.where(forced, jnp.inf, jnp.where(causal, imp_g, -jnp.inf))
        imp_ref[g] = imp_g
        imp.append(imp_g)

    def rank_body(i2, cnts):
        cnts = list(cnts)
        for i in (2 * i2, 2 * i2 + 1):
            tie = jnp.where(j_idx > i, 1, 0)
            for g in groups:
                row = imp_ref[g, pl.ds(i, 1), :]
                cnts[g] = cnts[g] + jnp.where(row > imp[g], 1, jnp.where(row == imp[g], tie, 0))
        return tuple(cnts)

    n_seen = (s0 + nq - 1) // SLC_BLOCK + 1
    cnts = lax.fori_loop(0, n_seen // 2, rank_body,
                         tuple(jnp.zeros((MASK_W, nq), jnp.int32) for _ in groups))
    q_aug = []
    for g in groups:
        sel_t = jnp.where(causal, jnp.where(cnts[g] < top, 0.0, -MASK_BIG), -MASK_BIG)
        mterm = sel_t.T.astype(BF16)
        q_aug += [jnp.concatenate([q_h[g * NSA_HPG + r], mterm], axis=1) for r in range(NSA_HPG)]

    sub_tiles = SEL_TILE // KEY_TILE

    q_aug_g = [jnp.concatenate(q_aug[g * NSA_HPG:(g + 1) * NSA_HPG], axis=0) for g in groups]
    head_lanes = lambda h: slice((h % NSA_HPG) * nq, (h % NSA_HPG + 1) * nq)

    def sel_scores(i, dst_ref):
        for g in groups:
            kt = kaug_ref[g, pl.ds(pl.multiple_of(i * SEL_TILE, SEL_TILE), SEL_TILE), :]
            dst_ref[g] = _dot_nt(kt, q_aug_g[g])

    def sel_step(i, src_ref, carry, masked):
        k0 = i * SEL_TILE
        sc = [src_ref[grp(h), :, head_lanes(h)] for h in heads]
        if masked:
            sc = [jnp.concatenate(
                [jnp.where(lane_minus_sub + (s0 - k0 - j * KEY_TILE) >= 0,
                           x[j * KEY_TILE:(j + 1) * KEY_TILE], -MASK_BIG)
                 for j in range(sub_tiles)], axis=0) for x in sc]
        m_new = [jnp.maximum(carry[h][0], jnp.max(sc[h], axis=0, keepdims=True)) for h in heads]
        alpha = [jnp.exp2(carry[h][0] - m_new[h]) for h in heads]
        p = [jnp.exp2(sc[h] - m_new[h]) for h in heads]
        l_new = [alpha[h] * carry[h][1] + jnp.sum(p[h], axis=0, keepdims=True) for h in heads]
        pb = [jnp.concatenate([p[h].astype(BF16) for h in range(g * NSA_HPG, (g + 1) * NSA_HPG)], axis=1)
              for g in groups]
        pv = [jnp.dot(vst_ref[i, gd[g], :], pb[g], preferred_element_type=F32) for g in groups]
        return tuple((m_new[h], l_new[h], alpha[h] * carry[h][2] + pv[grp(h)][:, head_lanes(h)])
                     for h in heads)

    init = tuple((jnp.full((1, nq), M_INIT, F32), jnp.zeros((1, nq), F32), jnp.zeros((d, nq), F32))
                 for _ in heads)
    n_full = s0 // SEL_TILE

    sel_scores(0, sca_ref)

    zeros = jnp.zeros((nq, d), BF16)
    q_w = [jnp.concatenate([q_h[h], zeros] if grp(h) == 0 else [zeros, q_h[h]], axis=1)
           for h in heads]
    q_w_g = [jnp.concatenate(q_w[g * NSA_HPG:(g + 1) * NSA_HPG], axis=0) for g in groups]
    k_tiles = [kw_ref[0, pl.ds(pl.multiple_of(w_start + j * KEY_TILE, KEY_TILE), KEY_TILE), :]
               for j in range(WIN_TILES)]
    win_ok = [(lane_minus_sub + (s0 - w_start - j * KEY_TILE)).astype(jnp.uint32) < WINDOW
              for j in range(WIN_TILES)]

    def win_scores(g):
        out = []
        for j in range(WIN_TILES):
            sc = _dot_nt(k_tiles[j], q_w_g[g])
            out.append([jnp.where(win_ok[j], sc[:, r * nq:(r + 1) * nq], -MASK_BIG) for r in range(NSA_HPG)])
        return out

    def win_softmax(sc):
        p_all, l_all = [], []
        for r in range(NSA_HPG):
            m = sc[0][r]
            for j in range(1, WIN_TILES):
                m = jnp.maximum(m, sc[j][r])
            m = jnp.max(m, axis=0, keepdims=True)
            p = [jnp.exp2(sc[j][r] - m) for j in range(WIN_TILES)]
            tot = p[0]
            for x in p[1:]:
                tot = tot + x
            p_all.append([x.astype(BF16) for x in p])
            l_all.append(jnp.sum(tot, axis=0, keepdims=True))
        return p_all, l_all

    def win_out(g, pl_):
        p_all, l_all = pl_
        acc = None
        for j in range(WIN_TILES):
            pj = jnp.concatenate([p_all[r][j] for r in range(NSA_HPG)], axis=1)
            term = jnp.dot(vwt_ref[w_tile0 + j, gd[g], :], pj, preferred_element_type=F32)
            acc = term if acc is None else acc + term
        return [acc[:, r * nq:(r + 1) * nq] / l_all[r] for r in range(NSA_HPG)]

    sc_w = [win_scores(g) for g in groups]
    p_w = [win_softmax(sc_w[g]) for g in groups]
    o_wt = []
    for g in groups:
        o_wt += win_out(g, p_w[g])

    def sel_body(k, carry):
        sel_scores(2 * k + 1, scb_ref)
        carry = sel_step(2 * k, sca_ref, carry, masked=False)
        sel_scores(2 * k + 2, sca_ref)
        return sel_step(2 * k + 1, scb_ref, carry, masked=False)

    carry = lax.fori_loop(0, n_full // 2, sel_body, init)

    def odd_tail(carry):
        sel_scores(n_full, scb_ref)
        carry = sel_step(n_full - 1, sca_ref, carry, masked=False)
        return sel_step(n_full, scb_ref, carry, masked=True)

    carry = lax.cond(n_full % 2 == 1, odd_tail,
                     lambda c: sel_step(n_full, sca_ref, c, masked=True), carry)
    o_st = [acc / l for (_, l, acc) in carry]

    for pair in range(NSA_HEADS // 2):
        halves = []
        for h in (2 * pair, 2 * pair + 1):
            r = h % NSA_HPG
            halves.append(gates_t[3 * h:3 * h + 1, :] * o_ct[grp(h)][:, r * nq:(r + 1) * nq]
                          + gates_t[3 * h + 1:3 * h + 2, :] * o_st[h]
                          + gates_t[3 * h + 2:3 * h + 3, :] * o_wt[h])
        o_ref[0, :, 2 * pair * d:(2 * pair + 2) * d] = jnp.concatenate(halves, axis=0).T.astype(o_ref.dtype)


def _cmp_to_slc_t(s):
    n_cmp = s // CMP_STRIDE
    n_slc = s // SLC_BLOCK
    cs = CMP_STRIDE * np.arange(n_cmp)[:, None]
    ss = SLC_BLOCK * np.arange(n_slc)[None, :]
    overlap = np.clip(np.minimum(cs + CMP_BLOCK, ss + SLC_BLOCK) - np.maximum(cs, ss), 0, None)
    m = np.zeros((MASK_W, n_cmp), np.float32)
    m[:n_slc] = (overlap / CMP_BLOCK).T
    return jnp.asarray(m).astype(BF16)


def _nsa(p, k_cmp, v_cmp):
    q = p["q"]
    bn, s, _ = q.shape
    assert s // SLC_BLOCK <= MASK_W and s % SEL_TILE == 0
    top = min(SLC_TOP, s // SLC_BLOCK)
    nc = k_cmp.shape[2]
    kvw = NSA_GROUPS * HEAD_DIM
    per_b = lambda w: pl.BlockSpec((1, s, w), lambda b, i: (b, 0, 0))
    return pl.pallas_call(
        functools.partial(_nsa_kernel, top=top),
        grid=(bn, s // Q_BLOCK),
        in_specs=[pl.BlockSpec((1, Q_BLOCK, NSA_HEADS * HEAD_DIM), lambda b, i: (b, i, 0)),
                  pl.BlockSpec((1, Q_BLOCK, LANES), lambda b, i: (b, i, 0)),
                  pl.BlockSpec((1, NSA_GROUPS, nc, HEAD_DIM), lambda b, i: (b, 0, 0, 0)),
                  pl.BlockSpec((1, NSA_GROUPS, nc, HEAD_DIM), lambda b, i: (b, 0, 0, 0)),
                  per_b(kvw), per_b(kvw), per_b(kvw), per_b(kvw),
                  pl.BlockSpec((MASK_W, nc), lambda b, i: (0, 0))],
        out_specs=pl.BlockSpec((1, Q_BLOCK, NSA_HEADS * HEAD_DIM), lambda b, i: (b, i, 0)),
        out_shape=jax.ShapeDtypeStruct((bn, s, NSA_HEADS * HEAD_DIM), BF16),
        scratch_shapes=[pltpu.VMEM((NSA_GROUPS, s, HEAD_DIM + MASK_W), BF16),
                        pltpu.VMEM((s // SEL_TILE, kvw, SEL_TILE), BF16),
                        pltpu.VMEM((s // KEY_TILE, kvw, KEY_TILE), BF16),
                        pltpu.VMEM((NSA_GROUPS, HEAD_DIM, nc), BF16),
                        pltpu.VMEM((NSA_GROUPS, SEL_TILE, NSA_HPG * Q_BLOCK), F32),
                        pltpu.VMEM((NSA_GROUPS, SEL_TILE, NSA_HPG * Q_BLOCK), F32),
                        pltpu.VMEM((NSA_GROUPS, MASK_W, Q_BLOCK), F32)],
        compiler_params=pltpu.CompilerParams(dimension_semantics=("parallel", "arbitrary"),
                                             vmem_limit_bytes=VMEM_LIMIT),
        name="nsa_attention",
    )(q, p["g"], k_cmp, v_cmp, p["ks"], p["vs"], p["kw"], p["vw"], _cmp_to_slc_t(s))


S5_STRIP = 512


def _gelu_tanh(y):
    return 0.5 * y * (1.0 + jnp.tanh(math.sqrt(2.0 / math.pi) * (y + 0.044715 * (y * y * y))))


def _s5_kernel(u_ref, bre_ref, bim_ref, are_ref, aim_ref, cre_ref, cim_ref, d_ref, gw_ref, gb_ref,
               o_ref, xre_ref, xim_ref, sre_ref, sim_ref):
    bn, tc, width = u_ref.shape
    nstate = are_ref.shape[1]

    @pl.when(pl.program_id(0) == 0)
    def _():
        sre_ref[...] = jnp.zeros_like(sre_ref)
        sim_ref[...] = jnp.zeros_like(sim_ref)

    u = jnp.swapaxes(u_ref[...].astype(F32), 0, 1).reshape(tc * bn, width)
    ub = u.astype(BF16)
    n_strips = nstate // S5_STRIP
    cw = width // n_strips
    for k in range(n_strips):
        cols = pl.ds(k * S5_STRIP, S5_STRIP)
        xre_ref[:, cols] = jnp.dot(ub[:, k * cw:(k + 1) * cw], bre_ref[k], preferred_element_type=F32)
        xim_ref[:, cols] = jnp.dot(ub[:, k * cw:(k + 1) * cw], bim_ref[k], preferred_element_type=F32)

    for k in range(n_strips):
        c0 = k * S5_STRIP
        cols = pl.ds(c0, S5_STRIP)
        a_r = jnp.broadcast_to(are_ref[:, c0:c0 + S5_STRIP], (bn, S5_STRIP))
        a_i = jnp.broadcast_to(aim_ref[:, c0:c0 + S5_STRIP], (bn, S5_STRIP))

        def step(t, carry):
            x_r, x_i = carry
            r0 = pl.multiple_of(t * bn, bn)
            n_r = a_r * x_r - a_i * x_i + xre_ref[pl.ds(r0, bn), cols]
            n_i = a_r * x_i + a_i * x_r + xim_ref[pl.ds(r0, bn), cols]
            xre_ref[pl.ds(r0, bn), cols] = n_r
            xim_ref[pl.ds(r0, bn), cols] = n_i
            return n_r, n_i

        x_r, x_i = lax.fori_loop(0, tc, step, (sre_ref[:, cols], sim_ref[:, cols]), unroll=8)
        sre_ref[:, cols] = x_r
        sim_ref[:, cols] = x_i

    y = jnp.concatenate(
        [jnp.dot(xre_ref[:, k * S5_STRIP:(k + 1) * S5_STRIP].astype(BF16), cre_ref[k],
                 preferred_element_type=F32)
         + jnp.dot(xim_ref[:, k * S5_STRIP:(k + 1) * S5_STRIP].astype(BF16), cim_ref[k],
                   preferred_element_type=F32) for k in range(n_strips)], axis=1) + d_ref[...] * u
    y = _gelu_tanh(y)
    z = _bdot(y, gw_ref[...]) + gb_ref[...]
    out = (y * _sigmoid(z)).reshape(tc, bn, width)
    o_ref[...] = jnp.swapaxes(out, 0, 1).astype(o_ref.dtype)


def _s5_params(a_re, a_im, b_re, b_im, c_re, c_im, log_dt):
    g, p = a_re.shape
    c = b_re.shape[-1]
    a_re, a_im = a_re.astype(F32), a_im.astype(F32)
    b_re, b_im = b_re.astype(F32), b_im.astype(F32)
    dt = jnp.exp(log_dt.astype(F32))[:, None]
    mag = jnp.exp(a_re * dt)
    abar_re, abar_im = mag * jnp.cos(a_im * dt), mag * jnp.sin(a_im * dt)
    den = a_re * a_re + a_im * a_im
    f_re = ((abar_re - 1.0) * a_re + abar_im * a_im) / den
    f_im = (abar_im * a_re - (abar_re - 1.0) * a_im) / den
    bbar_re = f_re[..., None] * b_re - f_im[..., None] * b_im
    bbar_im = f_re[..., None] * b_im + f_im[..., None] * b_re
    gs = S5_STRIP // p
    nb = g // gs
    eye = jnp.eye(gs, dtype=F32)
    pack_b = lambda m: jnp.einsum("kgpc,gh->kgchp", m.reshape(nb, gs, p, c), eye).reshape(
        nb, gs * c, gs * p).astype(BF16)
    pack_c = lambda m: jnp.einsum("kgcp,gh->kgphc", m.reshape(nb, gs, c, p), eye).reshape(
        nb, gs * p, gs * c).astype(BF16)
    return (pack_b(bbar_re), pack_b(bbar_im), abar_re.reshape(1, g * p), abar_im.reshape(1, g * p),
            pack_c(c_re.astype(F32)), pack_c(-c_im.astype(F32)))


def _s5(u, params, d, glu_w, glu_b, tc=128):
    bn, s, width = u.shape
    assert bn == SUBLANES
    bre, bim, are, aim, cre, cim = params
    nstate = are.shape[1]
    full = lambda a: pl.BlockSpec(a.shape, lambda i: tuple(0 for _ in a.shape))
    d2, gb2, gwb = d.reshape(1, width), glu_b.reshape(1, width), glu_w.astype(BF16)
    return pl.pallas_call(
        _s5_kernel,
        grid=(s // tc,),
        in_specs=[pl.BlockSpec((bn, tc, width), lambda i: (0, i, 0)),
                  full(bre), full(bim), full(are), full(aim), full(cre), full(cim),
                  full(d2), full(gwb), full(gb2)],
        out_specs=pl.BlockSpec((bn, tc, width), lambda i: (0, i, 0)),
        out_shape=jax.ShapeDtypeStruct((bn, s, width), BF16),
        scratch_shapes=[pltpu.VMEM((tc * bn, nstate), F32), pltpu.VMEM((tc * bn, nstate), F32),
                        pltpu.VMEM((bn, nstate), F32), pltpu.VMEM((bn, nstate), F32)],
        compiler_params=pltpu.CompilerParams(dimension_semantics=("arbitrary",),
                                             vmem_limit_bytes=VMEM_LIMIT),
        name="s5_scan",
    )(u, bre, bim, are, aim, cre, cim, d2, gwb, gb2)


RWKV_CHUNK = 64
RWKV_BLOCK = 256


def _split_dot(x, seg):
    hi = x.astype(BF16)
    lo = (x - hi.astype(F32)).astype(BF16)
    return (jnp.dot(hi, seg, preferred_element_type=F32) + jnp.dot(lo, seg, preferred_element_type=F32))


def _rwkv_kernel(mix_ref, mu_ref, w0_ref, w2_ref, a0_ref, a2_ref, kk_ref, ka_ref, rk_ref,
                 lnw_ref, lnb_ref, seg_ref, o_ref, state_ref, carry_ref, y_ref):
    n = HEAD_DIM
    chunk = RWKV_CHUNK
    tb = mix_ref.shape[1]
    w = BRANCH_W

    @pl.when(pl.program_id(1) == 0)
    def _():
        state_ref[...] = jnp.zeros_like(state_ref)
        carry_ref[...] = jnp.zeros_like(carry_ref)

    mix = mix_ref[0]
    row_m = lax.broadcasted_iota(jnp.int32, mix.shape, 0)
    prev = jnp.where(row_m == 0, carry_ref[0:1, :], pltpu.roll(mix, 1, 0))
    carry_ref[0:1, :] = mix[tb - 1:tb, :]
    xs = mix + (prev - mix) * mu_ref[...]
    r, k, v = xs[:, 0:w], xs[:, w:2 * w], xs[:, 2 * w:3 * w]
    wl, al = xs[:, 3 * w:3 * w + LORA_W], xs[:, 3 * w + LORA_W:3 * w + 2 * LORA_W]

    lw = w0_ref[...] + _bdot(jnp.tanh(wl), w2_ref[...])
    z = -lw
    softplus = jnp.maximum(z, 0.0) + jnp.log(1.0 + jnp.exp(-jnp.abs(z)))
    ld = -jnp.exp(-softplus - 0.5)
    a = _sigmoid(a0_ref[...] + _bdot(al, a2_ref[...]))
    seg = seg_ref[...]
    kk = k * kk_ref[...]
    kk = kk / jnp.maximum(jnp.sqrt(_split_dot(kk * kk, seg)), 1e-12)
    k2 = k * (1.0 + (a - 1.0) * ka_ref[...])
    aa = -kk
    bb = kk * a

    row = lax.broadcasted_iota(jnp.int32, (tb, w), 0) % chunk
    cum = ld
    sh = 1
    while sh < chunk:
        cum = cum + jnp.where(row >= sh, pltpu.roll(cum, sh, 0), 0.0)
        sh *= 2
    at = (aa * jnp.exp(cum - ld)).astype(BF16)
    rt = (r * jnp.exp(cum)).astype(BF16)
    e_neg = jnp.exp(-cum)
    bt, kt = (bb * e_neg).astype(BF16), (k2 * e_neg).astype(BF16)
    vb = v.astype(BF16)

    ti = lax.broadcasted_iota(jnp.int32, (2 * chunk, chunk), 0)
    si = lax.broadcasted_iota(jnp.int32, (2 * chunk, chunk), 1)
    strict = si < ti
    lower2 = si < jnp.where(ti < chunk, ti, ti - chunk + 1)
    n_double = int(math.log2(chunk)) - 1

    n_chunks = tb // chunk
    pairs = [(ci, h) for ci in range(n_chunks) for h in range(RWKV_HEADS)]
    rows_of = lambda ci: slice(ci * chunk, (ci + 1) * chunk)
    lanes_of = lambda h: slice(h * n, (h + 1) * n)
    bdot32 = lambda p_, q_: jnp.dot(p_, q_, preferred_element_type=F32)

    at_l = [at[rows_of(ci), lanes_of(h)] for ci, h in pairs]
    rt_l = [rt[rows_of(ci), lanes_of(h)] for ci, h in pairs]
    v_l = [vb[rows_of(ci), lanes_of(h)] for ci, h in pairs]
    amat_l = [_dot_nt(jnp.concatenate([at_l[i], rt_l[i]], axis=0),
                      jnp.concatenate([bt[rows_of(ci), lanes_of(h)], kt[rows_of(ci), lanes_of(h)]], axis=0))
              for i, (ci, h) in enumerate(pairs)]
    pw_l = [jnp.where(strict[:chunk], m[:chunk, :chunk], 0.0).astype(BF16) for m in amat_l]
    arb_l = [jnp.where(lower2[chunk:], m[chunk:, :chunk], 0.0).astype(BF16) for m in amat_l]
    axk_l = [jnp.where(lower2, m[:, chunk:], 0.0).astype(BF16) for m in amat_l]
    xv_l = [bdot32(axk_l[i], v_l[i]) for i in range(len(pairs))]
    x_l = [jnp.concatenate([at_l[i].astype(F32), xv_l[i][:chunk]], axis=1) for i in range(len(pairs))]
    x_l = [x + bdot32(pw, x.astype(BF16)) for x, pw in zip(x_l, pw_l)]
    for _ in range(n_double):
        pw_l = [bdot32(pw, pw).astype(BF16) for pw in pw_l]
        x_l = [x + bdot32(pw, x.astype(BF16)) for x, pw in zip(x_l, pw_l)]
    ro_l = [jnp.concatenate([rt_l[i].astype(F32), xv_l[i][chunk:]], axis=1)
            + bdot32(arb_l[i], x_l[i].astype(BF16)) for i in range(len(pairs))]
    wr_l = [jnp.concatenate([x[:, :n], ro[:, :n]], axis=0).astype(BF16) for x, ro in zip(x_l, ro_l)]

    state = [state_ref[h] for h in range(RWKV_HEADS)]
    for ci in range(n_chunks):
        rows = rows_of(ci)
        cum_c = cum[rows]
        tot = cum_c[chunk - 1:chunk, :]
        e_rem = jnp.exp(tot - cum_c)
        bh, kh = (bb[rows] * e_rem).astype(BF16), (k2[rows] * e_rem).astype(BF16)
        p_tot = jnp.exp(tot)
        base = ci * RWKV_HEADS
        g_l = [_dot_nt(wr_l[base + h], state[h].astype(BF16)) for h in range(RWKV_HEADS)]
        for h in range(RWKV_HEADS):
            y_ref[rows, lanes_of(h)] = g_l[h][chunk:] + ro_l[base + h][:, n:]
        uv_l = [jnp.concatenate([g_l[h][:chunk] + x_l[base + h][:, n:], v_l[base + h].astype(F32)], axis=0)
                for h in range(RWKV_HEADS)]
        state = [state[h] * p_tot[:, lanes_of(h)]
                 + bdot32(uv_l[h].T.astype(BF16),
                          jnp.concatenate([bh[:, lanes_of(h)], kh[:, lanes_of(h)]], axis=0))
                 for h in range(RWKV_HEADS)]
    for h in range(RWKV_HEADS):
        state_ref[h] = state[h]

    y = y_ref[...]
    inv_n = 1.0 / n
    mean = _split_dot(y, seg) * inv_n
    dev = y - mean
    var = _split_dot(dev * dev, seg) * inv_n
    yn = dev * lax.rsqrt(var + RWKV_LN_EPS) * lnw_ref[...] + lnb_ref[...]
    bonus = _split_dot(r * k2 * rk_ref[...], seg) * v
    o_ref[0] = (yn + bonus).astype(o_ref.dtype)


def _rwkv(mix, mu, w0, w2, a0, a2, k_k, k_a, r_k, ln_w, ln_b):
    bn, s, mw = mix.shape
    w = BRANCH_W
    tb = min(RWKV_BLOCK, s)
    head = np.arange(w) // HEAD_DIM
    seg = jnp.asarray((head[:, None] == head[None, :]).astype(np.float32)).astype(BF16)
    row = lambda t: t.reshape(1, -1).astype(F32)
    args = (row(mu), row(w0), w2.astype(BF16), row(a0), a2.astype(BF16), row(k_k), row(k_a), row(r_k),
            row(ln_w), row(ln_b), seg)
    full = lambda a: pl.BlockSpec(a.shape, lambda b, i: tuple(0 for _ in a.shape))
    return pl.pallas_call(
        _rwkv_kernel,
        grid=(bn, s // tb),
        in_specs=[pl.BlockSpec((1, tb, mw), lambda b, i: (b, i, 0))] + [full(t) for t in args],
        out_specs=pl.BlockSpec((1, tb, w), lambda b, i: (b, i, 0)),
        out_shape=jax.ShapeDtypeStruct((bn, s, w), BF16),
        scratch_shapes=[pltpu.VMEM((RWKV_HEADS, HEAD_DIM, HEAD_DIM), F32),
                        pltpu.VMEM((SUBLANES, mw), F32),
                        pltpu.VMEM((tb, w), F32)],
        compiler_params=pltpu.CompilerParams(dimension_semantics=("parallel", "arbitrary"),
                                             vmem_limit_bytes=VMEM_LIMIT),
        name="rwkv7",
    )(mix, *args)


def _merge_kernel(on_ref, os_ref, or_ref, gn_ref, gs_ref, gr_ref, mg_ref, x_ref, gate_ref,
                  wup_ref, wout_ref, fnw_ref, o_ref, *, final):
    d = x_ref.shape[2]
    merged = None
    for i, (b_ref, g_ref) in enumerate(((on_ref, gn_ref), (os_ref, gs_ref), (or_ref, gr_ref))):
        branch = b_ref[0].astype(F32) * _silu(g_ref[0].astype(F32))
        up = _bdot(branch, wup_ref[i])
        term = _sigmoid(mg_ref[0, :, i * d:(i + 1) * d].astype(F32)) * up
        merged = term if merged is None else merged + term
    out = x_ref[0] + gate_ref[0] * _bdot(merged, wout_ref[...])
    if final:
        ms = jnp.mean(out * out, axis=-1, keepdims=True)
        out = out * lax.rsqrt(ms + NORM_EPS) * fnw_ref[...]
    o_ref[0] = out


def _merge(o_nsa, o_s5, o_rwkv, p, x, gate, w_up, w_out, fnw, final, tm=512):
    bn, s, d = x.shape
    w = BRANCH_W
    rows = lambda width: pl.BlockSpec((1, tm, width), lambda b, i: (b, i, 0))
    wupb, woutb = w_up.astype(BF16), w_out.astype(BF16)
    return pl.pallas_call(
        functools.partial(_merge_kernel, final=final),
        grid=(bn, s // tm),
        in_specs=[rows(w)] * 6 + [rows(3 * d), rows(d),
                                  pl.BlockSpec((1, 1, d), lambda b, i: (b, 0, 0)),
                                  pl.BlockSpec(wupb.shape, lambda b, i: (0, 0, 0)),
                                  pl.BlockSpec(woutb.shape, lambda b, i: (0, 0)),
                                  pl.BlockSpec((1, d), lambda b, i: (0, 0))],
        out_specs=rows(d),
        out_shape=jax.ShapeDtypeStruct((bn, s, d), F32),
        compiler_params=pltpu.CompilerParams(dimension_semantics=("parallel", "parallel"),
                                             vmem_limit_bytes=VMEM_LIMIT),
        name="merge_out",
    )(o_nsa, o_s5, o_rwkv, p["ng"], p["sg"], p["rg"], p["mg"], x, gate.reshape(bn, 1, d),
      wupb, woutb, fnw.reshape(1, d))


def kernel(x, c, norm_w, mod_w, mod_b, w_in, cmp_pos_k, cmp_pos_v, cmp_w1_k, cmp_w2_k, cmp_w1_v, cmp_w2_v, s5_a_re, s5_a_im, s5_b_re, s5_b_im, s5_c_re, s5_c_im, s5_d, s5_log_dt, s5_glu_w, s5_glu_b, rwkv_mu, rwkv_w0, rwkv_w2, rwkv_a0, rwkv_a2, rwkv_k_k, rwkv_k_a, rwkv_r_k, rwkv_ln_w, rwkv_ln_b, w_up, w_out, final_norm_w):
    bn, s, d = x.shape
    depth = norm_w.shape[0]
    cos_t, sin_t = _rope_tables(s)
    mod = _modulation(c, mod_w, mod_b)
    for l in range(depth):
        shift, scale, gate = mod[l, :, 0:d], mod[l, :, d:2 * d], mod[l, :, 2 * d:3 * d]
        p = _inproj(x, norm_w[l], scale, shift, cos_t, sin_t, _pack_w_in(w_in[l]))
        k_cmp, v_cmp = _compress(p["kc"], p["vc"], cmp_pos_k[l], cmp_pos_v[l], cmp_w1_k[l], cmp_w2_k[l],
                                 cmp_w1_v[l], cmp_w2_v[l])
        o_nsa = _nsa(p, k_cmp, v_cmp)
        s5p = _s5_params(s5_a_re[l], s5_a_im[l], s5_b_re[l], s5_b_im[l], s5_c_re[l], s5_c_im[l],
                         s5_log_dt[l])
        o_s5 = _s5(p["su"], s5p, s5_d[l], s5_glu_w[l], s5_glu_b[l])
        o_rwkv = _rwkv(p["mix"], rwkv_mu[l], rwkv_w0[l], rwkv_w2[l], rwkv_a0[l], rwkv_a2[l],
                       rwkv_k_k[l], rwkv_k_a[l], rwkv_r_k[l], rwkv_ln_w[l], rwkv_ln_b[l])
        x = _merge(o_nsa, o_s5, o_rwkv, p, x, gate, w_up[l], w_out[l], final_norm_w,
                   final=(l == depth - 1))
    return x
```

```python
import functools
import math

import numpy as np
import jax
import jax.numpy as jnp
from jax import lax
from jax.experimental import pallas as pl
from jax.experimental.pallas import tpu as pltpu

F32 = jnp.float32
BF16 = jnp.bfloat16

HEAD_DIM = 64
NSA_HEADS = 8
NSA_GROUPS = 2
NSA_HPG = NSA_HEADS // NSA_GROUPS
CMP_BLOCK = 32
CMP_STRIDE = 16
SLC_BLOCK = 64
SLC_TOP = 16
WINDOW = 512
Q_BLOCK = 128
RWKV_HEADS = 8
BRANCH_W = 512
LORA_W = 64
RWKV_MIX_W = 3 * BRANCH_W + 2 * LORA_W
ROPE_THETA = 10000.0
NORM_EPS = 1e-6
RWKV_LN_EPS = 64e-5
NEG_BIG = -1e30
MASK_BIG = 2.0 ** 100
M_INIT = -3.0e38
Q_SCALE = HEAD_DIM ** -0.5 * math.log2(math.e)

VMEM_LIMIT = 56 * 1024 * 1024
LANES = 128
SUBLANES = 8

C_Q = 0
C_KC, C_KS, C_KW, C_VC, C_VS, C_VW = 512, 640, 768, 896, 1024, 1152
C_G = 1280
C_NG = 1408
C_SU = 1920
C_SG = 2432
C_MIX = 2944
C_RG = C_MIX + RWKV_MIX_W
C_MG = C_RG + BRANCH_W
IN_PACKED = C_MG + 3 * 1024


def _sigmoid(z):
    return 0.5 * jnp.tanh(0.5 * z) + 0.5


def _silu(z):
    return z * _sigmoid(z)


def _bdot(a, b):
    return jnp.dot(a.astype(BF16), b.astype(BF16), preferred_element_type=F32)


def _dot_nt(a, b):
    return lax.dot_general(a, b, (((1,), (1,)), ((), ())), preferred_element_type=F32)


def _mod_kernel(c_ref, w_ref, b_ref, o_ref):
    cond = _silu(c_ref[...])
    o_ref[0] = _bdot(cond, w_ref[0]) + b_ref[0]


def _modulation(c, mod_w, mod_b):
    depth, d, d3 = mod_w.shape
    bn = c.shape[0]
    nj = d3 // d
    return pl.pallas_call(
        _mod_kernel,
        grid=(depth, nj),
        in_specs=[pl.BlockSpec((bn, d), lambda l, j: (0, 0)),
                  pl.BlockSpec((1, d, d), lambda l, j: (l, 0, j)),
                  pl.BlockSpec((1, 1, d), lambda l, j: (l, 0, j))],
        out_specs=pl.BlockSpec((1, bn, d), lambda l, j: (l, 0, j)),
        out_shape=jax.ShapeDtypeStruct((depth, bn, d3), F32),
        name="adaln_mod",
    )(c, mod_w, mod_b.reshape(depth, 1, d3))


_INPROJ_OUTS = (
    ("q", C_Q, 512, BF16), ("kc", C_KC, 128, F32), ("ks", C_KS, 128, BF16), ("kw", C_KW, 128, BF16),
    ("vc", C_VC, 128, F32), ("vs", C_VS, 128, BF16), ("vw", C_VW, 128, BF16), ("g", C_G, 128, F32),
    ("ng", C_NG, 512, BF16), ("su", C_SU, 512, BF16), ("sg", C_SG, 512, BF16),
    ("mix", C_MIX, RWKV_MIX_W, F32), ("rg", C_RG, 512, BF16), ("mg", C_MG, 3072, BF16))
_ROPED = ("q", "kc", "ks", "kw")


def _inproj_kernel(x_ref, nw_ref, sc_ref, sh_ref, cos_ref, sin_ref, w_ref, *out_refs):
    x = x_ref[0]
    tm = x.shape[0]
    ms = jnp.mean(x * x, axis=-1, keepdims=True)
    h = x * lax.rsqrt(ms + NORM_EPS) * nw_ref[...]
    h = h * (1.0 + sc_ref[0]) + sh_ref[0]
    hb = h.astype(BF16)
    cos = cos_ref[...]
    sin = sin_ref[...]
    lane = lax.broadcasted_iota(jnp.int32, (tm, LANES), 1)
    first_half = (lane % HEAD_DIM) < (HEAD_DIM // 2)

    def rope(t):
        partner = jnp.where(first_half, pltpu.roll(t, 96, 1), pltpu.roll(t, 32, 1))
        return t * cos + partner * sin

    for (name, c0, width, dt), o_ref in zip(_INPROJ_OUTS, out_refs):
        step = min(width, 512)
        for j0 in range(0, width, step):
            w = min(step, width - j0)
            y = jnp.dot(hb, w_ref[:, c0 + j0:c0 + j0 + w], preferred_element_type=F32)
            if name in _ROPED:
                y = jnp.concatenate([rope(y[:, k:k + LANES]) for k in range(0, w, LANES)], axis=1)
            if name == "q":
                y = y * Q_SCALE
            o_ref[0, :, j0:j0 + w] = y.astype(dt)


def _inproj(x, nw, scale, shift, cos_t, sin_t, w_packed, tm=512):
    bn, s, d = x.shape
    out_shape = [jax.ShapeDtypeStruct((bn, s, w), dt) for (_, _, w, dt) in _INPROJ_OUTS]
    out_specs = [pl.BlockSpec((1, tm, w), lambda b, i: (b, i, 0)) for (_, _, w, _) in _INPROJ_OUTS]
    outs = pl.pallas_call(
        _inproj_kernel,
        grid=(bn, s // tm),
        in_specs=[pl.BlockSpec((1, tm, d), lambda b, i: (b, i, 0)),
                  pl.BlockSpec((1, d), lambda b, i: (0, 0)),
                  pl.BlockSpec((1, 1, d), lambda b, i: (b, 0, 0)),
                  pl.BlockSpec((1, 1, d), lambda b, i: (b, 0, 0)),
                  pl.BlockSpec((tm, LANES), lambda b, i: (i, 0)),
                  pl.BlockSpec((tm, LANES), lambda b, i: (i, 0)),
                  pl.BlockSpec((d, IN_PACKED), lambda b, i: (0, 0), pipeline_mode=pl.Buffered(1))],
        out_specs=out_specs,
        out_shape=out_shape,
        compiler_params=pltpu.CompilerParams(dimension_semantics=("parallel", "parallel"),
                                             vmem_limit_bytes=VMEM_LIMIT),
        name="inproj",
    )(x, nw.reshape(1, d), scale.reshape(bn, 1, d), shift.reshape(bn, 1, d), cos_t, sin_t, w_packed)
    return dict(zip([o[0] for o in _INPROJ_OUTS], outs))


def _pack_w_in(w_in):
    d = w_in.shape[0]
    sizes = (512, 768, 24, 512, 512, 512, RWKV_MIX_W, 512, 3072)
    offs = np.concatenate([[0], np.cumsum(sizes)])
    q, kv, g, ng, su, sg, mix, rg, mg = [w_in[:, offs[i]:offs[i + 1]] for i in range(len(sizes))]
    kc, vc, ks, vs, kw, vw = [kv[:, i * 128:(i + 1) * 128] for i in range(6)]
    gpad = jnp.pad(g, ((0, 0), (0, 128 - 24)))
    return jnp.concatenate([q, kc, ks, kw, vc, vs, vw, gpad, ng, su, sg, mix, rg, mg], axis=1).astype(BF16)


def _rope_tables(s):
    half = HEAD_DIM // 2
    inv = jnp.exp(-math.log(ROPE_THETA) * jnp.arange(half, dtype=F32) / half)
    ang = jnp.arange(s, dtype=F32)[:, None] * inv[None, :]
    cos, sin = jnp.cos(ang), jnp.sin(ang)
    cos_t = jnp.tile(cos, (1, LANES // half))
    sin_t = jnp.tile(jnp.concatenate([-sin, sin], axis=1), (1, LANES // HEAD_DIM))
    return cos_t, sin_t


def _compress_kernel(kc_ref, vc_ref, pk_ref, pv_ref, wkt_ref, wkb_ref, wvt_ref, wvb_ref,
                     w2k_ref, w2v_ref, ko_ref, vo_ref):
    def one(x_ref, p_ref, wt_ref, wb_ref, w2_ref, o_ref):
        kvw = x_ref.shape[2]
        n = x_ref.shape[1] // CMP_STRIDE
        top = bot = None
        for l in range(CMP_STRIDE):
            x_l = x_ref[0, pl.ds(l, n, stride=CMP_STRIDE), :]
            cols = slice(l * kvw, (l + 1) * kvw)
            t = _bdot(x_l + p_ref[0:1, cols], wt_ref[cols, :])
            b = _bdot(x_l + p_ref[1:2, cols], wb_ref[cols, :])
            top, bot = (t, b) if top is None else (top + t, bot + b)
        hid = top + pltpu.roll(bot, n - 1, 0)
        act = _silu(hid)
        hw = act.shape[1] // NSA_GROUPS
        for g in range(NSA_GROUPS):
            o_ref[0, g] = _bdot(act[:, g * hw:(g + 1) * hw], w2_ref[...])

    one(kc_ref, pk_ref, wkt_ref, wkb_ref, w2k_ref, ko_ref)
    one(vc_ref, pv_ref, wvt_ref, wvb_ref, w2v_ref, vo_ref)


def _compress_weights(pos, w1, w2):
    hid = w1.shape[1]
    half = CMP_BLOCK // 2
    w1r = w1.reshape(2, half, HEAD_DIM, hid)
    eye = jnp.eye(NSA_GROUPS, dtype=w1.dtype)
    wd = jnp.einsum("tldj,gh->tlgdhj", w1r, eye).reshape(2, half * NSA_GROUPS * HEAD_DIM,
                                                         NSA_GROUPS * hid)
    pr = pos.reshape(2, half, 1, HEAD_DIM)
    pt = jnp.broadcast_to(pr, (2, half, NSA_GROUPS, HEAD_DIM)).reshape(2, -1)
    return pt, wd[0].astype(BF16), wd[1].astype(BF16), w2.astype(BF16)


def _compress(kc, vc, pos_k, pos_v, w1k, w2k, w1v, w2v):
    bn, s, kvw = kc.shape
    n16 = s // CMP_STRIDE
    row_w = CMP_STRIDE * kvw
    pk, wkt, wkb, w2kb = _compress_weights(pos_k, w1k, w2k)
    pv, wvt, wvb, w2vb = _compress_weights(pos_v, w1v, w2v)
    hid2 = wkt.shape[1]
    full = lambda shape: pl.BlockSpec(shape, lambda b: tuple(0 for _ in shape))
    out = jax.ShapeDtypeStruct((bn, NSA_GROUPS, n16, HEAD_DIM), F32)
    return pl.pallas_call(
        _compress_kernel,
        grid=(bn,),
        in_specs=[pl.BlockSpec((1, s, kvw), lambda b: (b, 0, 0)),
                  pl.BlockSpec((1, s, kvw), lambda b: (b, 0, 0)),
                  full((2, row_w)), full((2, row_w)),
                  full((row_w, hid2)), full((row_w, hid2)), full((row_w, hid2)), full((row_w, hid2)),
                  full(w2kb.shape), full(w2vb.shape)],
        out_specs=[pl.BlockSpec((1, NSA_GROUPS, n16, HEAD_DIM), lambda b: (b, 0, 0, 0))] * 2,
        out_shape=[out, out],
        compiler_params=pltpu.CompilerParams(dimension_semantics=("parallel",),
                                             vmem_limit_bytes=VMEM_LIMIT),
        name="nsa_compress",
    )(kc, vc, pk, pv, wkt, wkb, wvt, wvb, w2kb, w2vb)


SEL_TILE = 256
KEY_TILE = 128
WIN_TILES = WINDOW // KEY_TILE + 1
MASK_W = 64


def _nsa_kernel(q_ref, g_ref, kcmp_ref, vcmp_ref, ks_ref, vs_ref, kw_ref, vw_ref, c2s_ref,
                o_ref, kaug_ref, vst_ref, vwt_ref, vct_ref, sca_ref, scb_ref, imp_ref, *, top):
    qb = pl.program_id(1)
    s0 = qb * Q_BLOCK
    s = ks_ref.shape[1]
    nc = kcmp_ref.shape[2]
    d = HEAD_DIM
    nq = Q_BLOCK

    @pl.when(qb == 0)
    def _():
        blk = lax.broadcasted_iota(jnp.int32, (s, MASK_W), 0) // SLC_BLOCK
        col = lax.broadcasted_iota(jnp.int32, (s, MASK_W), 1)
        onehot = jnp.where(blk == col, 1.0, 0.0).astype(BF16)
        for g in range(NSA_GROUPS):
            kaug_ref[g, :, 0:d] = ks_ref[0, :, g * d:(g + 1) * d]
            kaug_ref[g, :, d:d + MASK_W] = onehot
            vct_ref[g] = vcmp_ref[0, g].T.astype(BF16)

        def transpose_tile(j, carry):
            r0 = pl.multiple_of(j * SEL_TILE, SEL_TILE)
            vst_ref[j] = vs_ref[0, pl.ds(r0, SEL_TILE), :].astype(F32).T.astype(BF16)
            for half in range(SEL_TILE // KEY_TILE):
                r1 = pl.multiple_of(r0 + half * KEY_TILE, KEY_TILE)
                vwt_ref[j * (SEL_TILE // KEY_TILE) + half] = (
                    vw_ref[0, pl.ds(r1, KEY_TILE), :].astype(F32).T.astype(BF16))
            return carry

        lax.fori_loop(0, s // SEL_TILE, transpose_tile, 0)

    q_all = q_ref[0]
    gates_t = _sigmoid(g_ref[0]).T
    lane_minus_sub = (lax.broadcasted_iota(jnp.int32, (KEY_TILE, nq), 1)
                      - lax.broadcasted_iota(jnp.int32, (KEY_TILE, nq), 0))
    w_start = jnp.maximum(s0 - WINDOW, 0)
    w_tile0 = w_start // KEY_TILE

    groups = range(NSA_GROUPS)
    heads = range(NSA_HEADS)
    grp = lambda h: h // NSA_HPG
    gd = [slice(g * d, (g + 1) * d) for g in groups]
    q_h = [q_all[:, h * d:(h + 1) * d] for h in heads]

    n_row = lax.broadcasted_iota(jnp.int32, (nc, NSA_HPG * nq), 0)
    t_col = s0 + lax.broadcasted_iota(jnp.int32, (nc, NSA_HPG * nq), 1) % nq
    valid_t = (CMP_STRIDE * n_row + CMP_BLOCK - 1) <= t_col
    j_idx = lax.broadcasted_iota(jnp.int32, (MASK_W, nq), 0)
    blk_t = (s0 + lax.broadcasted_iota(jnp.int32, (MASK_W, nq), 1)) // SLC_BLOCK
    causal = j_idx <= blk_t
    forced = (j_idx == 0) | (j_idx == blk_t)
    c2s = c2s_ref[...]
    o_ct, imp = [], []
    for g in groups:
        qg = jnp.concatenate(q_h[g * NSA_HPG:(g + 1) * NSA_HPG], axis=0)
        st = _dot_nt(kcmp_ref[0, g].astype(BF16), qg)
        stm = jnp.where(valid_t, st, NEG_BIG)
        et = jnp.exp2(stm - jnp.max(stm, axis=0, keepdims=True))
        pt = jnp.where(valid_t, et / jnp.sum(et, axis=0, keepdims=True), 0.0)
        o_ct.append(jnp.dot(vct_ref[g], pt.astype(BF16), preferred_element_type=F32))
        psum = pt[:, 0:nq]
        for r in range(1, NSA_HPG):
            psum = psum + pt[:, r * nq:(r + 1) * nq]
        p_hi = psum.astype(BF16)
        p_lo = (psum - p_hi.astype(F32)).astype(BF16)
        imp_g = (jnp.dot(c2s, p_hi, preferred_element_type=F32)
                 + jnp.dot(c2s, p_lo, preferred_element_type=F32))
        imp_g = jnp.where(forced, jnp.inf, jnp.where(causal, imp_g, -jnp.inf))
        imp_ref[g] = imp_g
        imp.append(imp_g)

    rank_rows = 4

    def rank_body(i4, cnts):
        cnts = list(cnts)
        for i in [rank_rows * i4 + k for k in range(rank_rows)]:
            tie = jnp.where(j_idx > i, 1, 0)
            for g in groups:
                row = imp_ref[g, pl.ds(i, 1), :]
                cnts[g] = cnts[g] + jnp.where(row > imp[g], 1, jnp.where(row == imp[g], tie, 0))
        return tuple(cnts)

    n_seen = (s0 + nq - 1) // SLC_BLOCK + 1
    cnts = lax.fori_loop(0, (n_seen + rank_rows - 1) // rank_rows, rank_body,
                         tuple(jnp.zeros((MASK_W, nq), jnp.int32) for _ in groups))
    q_aug = []
    for g in groups:
        sel_t = jnp.where(causal, jnp.where(cnts[g] < top, 0.0, -MASK_BIG), -MASK_BIG)
        mterm = sel_t.T.astype(BF16)
        q_aug += [jnp.concatenate([q_h[g * NSA_HPG + r], mterm], axis=1) for r in range(NSA_HPG)]

    sub_tiles = SEL_TILE // KEY_TILE

    q_aug_g = [jnp.concatenate(q_aug[g * NSA_HPG:(g + 1) * NSA_HPG], axis=0) for g in groups]
    head_lanes = lambda h: slice((h % NSA_HPG) * nq, (h % NSA_HPG + 1) * nq)

    def sel_scores(i, dst_ref):
        for g in groups:
            kt = kaug_ref[g, pl.ds(pl.multiple_of(i * SEL_TILE, SEL_TILE), SEL_TILE), :]
            dst_ref[g] = _dot_nt(kt, q_aug_g[g])

    def sel_step(i, src_ref, carry, masked):
        k0 = i * SEL_TILE
        sc = [src_ref[grp(h), :, head_lanes(h)] for h in heads]
        if masked:
            sc = [jnp.concatenate(
                [jnp.where(lane_minus_sub + (s0 - k0 - j * KEY_TILE) >= 0,
                           x[j * KEY_TILE:(j + 1) * KEY_TILE], -MASK_BIG)
                 for j in range(sub_tiles)], axis=0) for x in sc]
        m_new = [jnp.maximum(carry[h][0], jnp.max(sc[h], axis=0, keepdims=True)) for h in heads]
        alpha = [jnp.exp2(carry[h][0] - m_new[h]) for h in heads]
        p = [jnp.exp2(sc[h] - m_new[h]) for h in heads]
        l_new = [alpha[h] * carry[h][1] + jnp.sum(p[h], axis=0, keepdims=True) for h in heads]
        pb = [jnp.concatenate([p[h].astype(BF16) for h in range(g * NSA_HPG, (g + 1) * NSA_HPG)], axis=1)
              for g in groups]
        pv = [jnp.dot(vst_ref[i, gd[g], :], pb[g], preferred_element_type=F32) for g in groups]
        return tuple((m_new[h], l_new[h], alpha[h] * carry[h][2] + pv[grp(h)][:, head_lanes(h)])
                     for h in heads)

    init = tuple((jnp.full((1, nq), M_INIT, F32), jnp.zeros((1, nq), F32), jnp.zeros((d, nq), F32))
                 for _ in heads)
    n_full = s0 // SEL_TILE

    sel_scores(0, sca_ref)

    zeros = jnp.zeros((nq, d), BF16)
    q_w = [jnp.concatenate([q_h[h], zeros] if grp(h) == 0 else [zeros, q_h[h]], axis=1)
           for h in heads]
    q_w_g = [jnp.concatenate(q_w[g * NSA_HPG:(g + 1) * NSA_HPG], axis=0) for g in groups]
    k_tiles = [kw_ref[0, pl.ds(pl.multiple_of(w_start + j * KEY_TILE, KEY_TILE), KEY_TILE), :]
               for j in range(WIN_TILES)]
    win_ok = [(lane_minus_sub + (s0 - w_start - j * KEY_TILE)).astype(jnp.uint32) < WINDOW
              for j in range(WIN_TILES)]

    def win_scores(g):
        out = []
        for j in range(WIN_TILES):
            sc = _dot_nt(k_tiles[j], q_w_g[g])
            out.append([jnp.where(win_ok[j], sc[:, r * nq:(r + 1) * nq], -MASK_BIG) for r in range(NSA_HPG)])
        return out

    def win_softmax(sc):
        p_all, l_all = [], []
        for r in range(NSA_HPG):
            m = sc[0][r]
            for j in range(1, WIN_TILES):
                m = jnp.maximum(m, sc[j][r])
            m = jnp.max(m, axis=0, keepdims=True)
            p = [jnp.exp2(sc[j][r] - m) for j in range(WIN_TILES)]
            tot = p[0]
            for x in p[1:]:
                tot = tot + x
            p_all.append([x.astype(BF16) for x in p])
            l_all.append(jnp.sum(tot, axis=0, keepdims=True))
        return p_all, l_all

    def win_out(g, pl_):
        p_all, l_all = pl_
        acc = None
        for j in range(WIN_TILES):
            pj = jnp.concatenate([p_all[r][j] for r in range(NSA_HPG)], axis=1)
            term = jnp.dot(vwt_ref[w_tile0 + j, gd[g], :], pj, preferred_element_type=F32)
            acc = term if acc is None else acc + term
        return [acc[:, r * nq:(r + 1) * nq] / l_all[r] for r in range(NSA_HPG)]

    sc_w = [win_scores(g) for g in groups]
    p_w = [win_softmax(sc_w[g]) for g in groups]
    o_wt = []
    for g in groups:
        o_wt += win_out(g, p_w[g])

    def sel_body(k, carry):
        sel_scores(2 * k + 1, scb_ref)
        carry = sel_step(2 * k, sca_ref, carry, masked=False)
        sel_scores(2 * k + 2, sca_ref)
        return sel_step(2 * k + 1, scb_ref, carry, masked=False)

    carry = lax.fori_loop(0, n_full // 2, sel_body, init)

    def odd_tail(carry):
        sel_scores(n_full, scb_ref)
        carry = sel_step(n_full - 1, sca_ref, carry, masked=False)
        return sel_step(n_full, scb_ref, carry, masked=True)

    carry = lax.cond(n_full % 2 == 1, odd_tail,
                     lambda c: sel_step(n_full, sca_ref, c, masked=True), carry)
    o_st = [acc / l for (_, l, acc) in carry]

    for pair in range(NSA_HEADS // 2):
        halves = []
        for h in (2 * pair, 2 * pair + 1):
            r = h % NSA_HPG
            halves.append(gates_t[3 * h:3 * h + 1, :] * o_ct[grp(h)][:, r * nq:(r + 1) * nq]
                          + gates_t[3 * h + 1:3 * h + 2, :] * o_st[h]
                          + gates_t[3 * h + 2:3 * h + 3, :] * o_wt[h])
        o_ref[0, :, 2 * pair * d:(2 * pair + 2) * d] = jnp.concatenate(halves, axis=0).T.astype(o_ref.dtype)


def _cmp_to_slc_t(s):
    n_cmp = s // CMP_STRIDE
    n_slc = s // SLC_BLOCK
    cs = CMP_STRIDE * np.arange(n_cmp)[:, None]
    ss = SLC_BLOCK * np.arange(n_slc)[None, :]
    overlap = np.clip(np.minimum(cs + CMP_BLOCK, ss + SLC_BLOCK) - np.maximum(cs, ss), 0, None)
    m = np.zeros((MASK_W, n_cmp), np.float32)
    m[:n_slc] = (overlap / CMP_BLOCK).T
    return jnp.asarray(m).astype(BF16)


def _nsa(p, k_cmp, v_cmp):
    q = p["q"]
    bn, s, _ = q.shape
    assert s // SLC_BLOCK <= MASK_W and s % SEL_TILE == 0
    top = min(SLC_TOP, s // SLC_BLOCK)
    nc = k_cmp.shape[2]
    kvw = NSA_GROUPS * HEAD_DIM
    per_b = lambda w: pl.BlockSpec((1, s, w), lambda b, i: (b, 0, 0))
    return pl.pallas_call(
        functools.partial(_nsa_kernel, top=top),
        grid=(bn, s // Q_BLOCK),
        in_specs=[pl.BlockSpec((1, Q_BLOCK, NSA_HEADS * HEAD_DIM), lambda b, i: (b, i, 0)),
                  pl.BlockSpec((1, Q_BLOCK, LANES), lambda b, i: (b, i, 0)),
                  pl.BlockSpec((1, NSA_GROUPS, nc, HEAD_DIM), lambda b, i: (b, 0, 0, 0)),
                  pl.BlockSpec((1, NSA_GROUPS, nc, HEAD_DIM), lambda b, i: (b, 0, 0, 0)),
                  per_b(kvw), per_b(kvw), per_b(kvw), per_b(kvw),
                  pl.BlockSpec((MASK_W, nc), lambda b, i: (0, 0))],
        out_specs=pl.BlockSpec((1, Q_BLOCK, NSA_HEADS * HEAD_DIM), lambda b, i: (b, i, 0)),
        out_shape=jax.ShapeDtypeStruct((bn, s, NSA_HEADS * HEAD_DIM), BF16),
        scratch_shapes=[pltpu.VMEM((NSA_GROUPS, s, HEAD_DIM + MASK_W), BF16),
                        pltpu.VMEM((s // SEL_TILE, kvw, SEL_TILE), BF16),
                        pltpu.VMEM((s // KEY_TILE, kvw, KEY_TILE), BF16),
                        pltpu.VMEM((NSA_GROUPS, HEAD_DIM, nc), BF16),
                        pltpu.VMEM((NSA_GROUPS, SEL_TILE, NSA_HPG * Q_BLOCK), F32),
                        pltpu.VMEM((NSA_GROUPS, SEL_TILE, NSA_HPG * Q_BLOCK), F32),
                        pltpu.VMEM((NSA_GROUPS, MASK_W, Q_BLOCK), F32)],
        compiler_params=pltpu.CompilerParams(dimension_semantics=("parallel", "arbitrary"),
                                             vmem_limit_bytes=VMEM_LIMIT),
        name="nsa_attention",
    )(q, p["g"], k_cmp, v_cmp, p["ks"], p["vs"], p["kw"], p["vw"], _cmp_to_slc_t(s))


S5_STRIP = 512


def _gelu_tanh(y):
    return 0.5 * y * (1.0 + jnp.tanh(math.sqrt(2.0 / math.pi) * (y + 0.044715 * (y * y * y))))


def _s5_kernel(u_ref, bre_ref, bim_ref, are_ref, aim_ref, cre_ref, cim_ref, d_ref, gw_ref, gb_ref,
               o_ref, xre_ref, xim_ref, sre_ref, sim_ref):
    bn, tc, width = u_ref.shape
    nstate = are_ref.shape[1]

    @pl.when(pl.program_id(0) == 0)
    def _():
        sre_ref[...] = jnp.zeros_like(sre_ref)
        sim_ref[...] = jnp.zeros_like(sim_ref)

    u = jnp.swapaxes(u_ref[...].astype(F32), 0, 1).reshape(tc * bn, width)
    ub = u.astype(BF16)
    n_strips = nstate // S5_STRIP
    cw = width // n_strips
    for k in range(n_strips):
        cols = pl.ds(k * S5_STRIP, S5_STRIP)
        xre_ref[:, cols] = jnp.dot(ub[:, k * cw:(k + 1) * cw], bre_ref[k], preferred_element_type=F32)
        xim_ref[:, cols] = jnp.dot(ub[:, k * cw:(k + 1) * cw], bim_ref[k], preferred_element_type=F32)

    for k in range(n_strips):
        c0 = k * S5_STRIP
        cols = pl.ds(c0, S5_STRIP)
        a_r = jnp.broadcast_to(are_ref[:, c0:c0 + S5_STRIP], (bn, S5_STRIP))
        a_i = jnp.broadcast_to(aim_ref[:, c0:c0 + S5_STRIP], (bn, S5_STRIP))

        def step(t, carry):
            x_r, x_i = carry
            r0 = pl.multiple_of(t * bn, bn)
            n_r = a_r * x_r - a_i * x_i + xre_ref[pl.ds(r0, bn), cols]
            n_i = a_r * x_i + a_i * x_r + xim_ref[pl.ds(r0, bn), cols]
            xre_ref[pl.ds(r0, bn), cols] = n_r
            xim_ref[pl.ds(r0, bn), cols] = n_i
            return n_r, n_i

        x_r, x_i = lax.fori_loop(0, tc, step, (sre_ref[:, cols], sim_ref[:, cols]), unroll=8)
        sre_ref[:, cols] = x_r
        sim_ref[:, cols] = x_i

    y = jnp.concatenate(
        [jnp.dot(xre_ref[:, k * S5_STRIP:(k + 1) * S5_STRIP].astype(BF16), cre_ref[k],
                 preferred_element_type=F32)
         + jnp.dot(xim_ref[:, k * S5_STRIP:(k + 1) * S5_STRIP].astype(BF16), cim_ref[k],
                   preferred_element_type=F32) for k in range(n_strips)], axis=1) + d_ref[...] * u
    y = _gelu_tanh(y)
    z = _bdot(y, gw_ref[...]) + gb_ref[...]
    out = (y * _sigmoid(z)).reshape(tc, bn, width)
    o_ref[...] = jnp.swapaxes(out, 0, 1).astype(o_ref.dtype)


def _s5_params(a_re, a_im, b_re, b_im, c_re, c_im, log_dt):
    g, p = a_re.shape
    c = b_re.shape[-1]
    a_re, a_im = a_re.astype(F32), a_im.astype(F32)
    b_re, b_im = b_re.astype(F32), b_im.astype(F32)
    dt = jnp.exp(log_dt.astype(F32))[:, None]
    mag = jnp.exp(a_re * dt)
    abar_re, abar_im = mag * jnp.cos(a_im * dt), mag * jnp.sin(a_im * dt)
    den = a_re * a_re + a_im * a_im
    f_re = ((abar_re - 1.0) * a_re + abar_im * a_im) / den
    f_im = (abar_im * a_re - (abar_re - 1.0) * a_im) / den
    bbar_re = f_re[..., None] * b_re - f_im[..., None] * b_im
    bbar_im = f_re[..., None] * b_im + f_im[..., None] * b_re
    gs = S5_STRIP // p
    nb = g // gs
    eye = jnp.eye(gs, dtype=F32)
    pack_b = lambda m: jnp.einsum("kgpc,gh->kgchp", m.reshape(nb, gs, p, c), eye).reshape(
        nb, gs * c, gs * p).astype(BF16)
    pack_c = lambda m: jnp.einsum("kgcp,gh->kgphc", m.reshape(nb, gs, c, p), eye).reshape(
        nb, gs * p, gs * c).astype(BF16)
    return (pack_b(bbar_re), pack_b(bbar_im), abar_re.reshape(1, g * p), abar_im.reshape(1, g * p),
            pack_c(c_re.astype(F32)), pack_c(-c_im.astype(F32)))


def _s5(u, params, d, glu_w, glu_b, tc=128):
    bn, s, width = u.shape
    assert bn == SUBLANES
    bre, bim, are, aim, cre, cim = params
    nstate = are.shape[1]
    full = lambda a: pl.BlockSpec(a.shape, lambda i: tuple(0 for _ in a.shape))
    d2, gb2, gwb = d.reshape(1, width), glu_b.reshape(1, width), glu_w.astype(BF16)
    return pl.pallas_call(
        _s5_kernel,
        grid=(s // tc,),
        in_specs=[pl.BlockSpec((bn, tc, width), lambda i: (0, i, 0)),
                  full(bre), full(bim), full(are), full(aim), full(cre), full(cim),
                  full(d2), full(gwb), full(gb2)],
        out_specs=pl.BlockSpec((bn, tc, width), lambda i: (0, i, 0)),
        out_shape=jax.ShapeDtypeStruct((bn, s, width), BF16),
        scratch_shapes=[pltpu.VMEM((tc * bn, nstate), F32), pltpu.VMEM((tc * bn, nstate), F32),
                        pltpu.VMEM((bn, nstate), F32), pltpu.VMEM((bn, nstate), F32)],
        compiler_params=pltpu.CompilerParams(dimension_semantics=("arbitrary",),
                                             vmem_limit_bytes=VMEM_LIMIT),
        name="s5_scan",
    )(u, bre, bim, are, aim, cre, cim, d2, gwb, gb2)


RWKV_CHUNK = 64
RWKV_BLOCK = 256


def _split_dot(x, seg):
    hi = x.astype(BF16)
    lo = (x - hi.astype(F32)).astype(BF16)
    return (jnp.dot(hi, seg, preferred_element_type=F32) + jnp.dot(lo, seg, preferred_element_type=F32))


def _rwkv_kernel(mix_ref, mu_ref, w0_ref, w2_ref, a0_ref, a2_ref, kk_ref, ka_ref, rk_ref,
                 lnw_ref, lnb_ref, seg_ref, o_ref, state_ref, carry_ref, y_ref):
    n = HEAD_DIM
    chunk = RWKV_CHUNK
    tb = mix_ref.shape[1]
    w = BRANCH_W

    @pl.when(pl.program_id(1) == 0)
    def _():
        state_ref[...] = jnp.zeros_like(state_ref)
        carry_ref[...] = jnp.zeros_like(carry_ref)

    mix = mix_ref[0]
    row_m = lax.broadcasted_iota(jnp.int32, mix.shape, 0)
    prev = jnp.where(row_m == 0, carry_ref[0:1, :], pltpu.roll(mix, 1, 0))
    carry_ref[0:1, :] = mix[tb - 1:tb, :]
    xs = mix + (prev - mix) * mu_ref[...]
    r, k, v = xs[:, 0:w], xs[:, w:2 * w], xs[:, 2 * w:3 * w]
    wl, al = xs[:, 3 * w:3 * w + LORA_W], xs[:, 3 * w + LORA_W:3 * w + 2 * LORA_W]

    lw = w0_ref[...] + _bdot(jnp.tanh(wl), w2_ref[...])
    z = -lw
    softplus = jnp.maximum(z, 0.0) + jnp.log(1.0 + jnp.exp(-jnp.abs(z)))
    ld = -jnp.exp(-softplus - 0.5)
    a = _sigmoid(a0_ref[...] + _bdot(al, a2_ref[...]))
    seg = seg_ref[...]
    kk = k * kk_ref[...]
    kk = kk / jnp.maximum(jnp.sqrt(_split_dot(kk * kk, seg)), 1e-12)
    k2 = k * (1.0 + (a - 1.0) * ka_ref[...])
    aa = -kk
    bb = kk * a

    row = lax.broadcasted_iota(jnp.int32, (tb, w), 0) % chunk
    cum = ld
    sh = 1
    while sh < chunk:
        cum = cum + jnp.where(row >= sh, pltpu.roll(cum, sh, 0), 0.0)
        sh *= 2
    at = (aa * jnp.exp(cum - ld)).astype(BF16)
    rt = (r * jnp.exp(cum)).astype(BF16)
    e_neg = jnp.exp(-cum)
    bt, kt = (bb * e_neg).astype(BF16), (k2 * e_neg).astype(BF16)
    vb = v.astype(BF16)

    ti = lax.broadcasted_iota(jnp.int32, (2 * chunk, chunk), 0)
    si = lax.broadcasted_iota(jnp.int32, (2 * chunk, chunk), 1)
    strict = si < ti
    lower2 = si < jnp.where(ti < chunk, ti, ti - chunk + 1)
    n_double = int(math.log2(chunk)) - 1

    n_chunks = tb // chunk
    pairs = [(ci, h) for ci in range(n_chunks) for h in range(RWKV_HEADS)]
    rows_of = lambda ci: slice(ci * chunk, (ci + 1) * chunk)
    lanes_of = lambda h: slice(h * n, (h + 1) * n)
    bdot32 = lambda p_, q_: jnp.dot(p_, q_, preferred_element_type=F32)

    at_l = [at[rows_of(ci), lanes_of(h)] for ci, h in pairs]
    rt_l = [rt[rows_of(ci), lanes_of(h)] for ci, h in pairs]
    v_l = [vb[rows_of(ci), lanes_of(h)] for ci, h in pairs]
    amat_l = [_dot_nt(jnp.concatenate([at_l[i], rt_l[i]], axis=0),
                      jnp.concatenate([bt[rows_of(ci), lanes_of(h)], kt[rows_of(ci), lanes_of(h)]], axis=0))
              for i, (ci, h) in enumerate(pairs)]
    pw_l = [jnp.where(strict[:chunk], m[:chunk, :chunk], 0.0).astype(BF16) for m in amat_l]
    arb_l = [jnp.where(lower2[chunk:], m[chunk:, :chunk], 0.0).astype(BF16) for m in amat_l]
    axk_l = [jnp.where(lower2, m[:, chunk:], 0.0).astype(BF16) for m in amat_l]
    xv_l = [bdot32(axk_l[i], v_l[i]) for i in range(len(pairs))]
    x_l = [jnp.concatenate([at_l[i].astype(F32), xv_l[i][:chunk]], axis=1) for i in range(len(pairs))]
    x_l = [x + bdot32(pw, x.astype(BF16)) for x, pw in zip(x_l, pw_l)]
    for _ in range(n_double):
        pw_l = [bdot32(pw, pw).astype(BF16) for pw in pw_l]
        x_l = [x + bdot32(pw, x.astype(BF16)) for x, pw in zip(x_l, pw_l)]
    ro_l = [jnp.concatenate([rt_l[i].astype(F32), xv_l[i][chunk:]], axis=1)
            + bdot32(arb_l[i], x_l[i].astype(BF16)) for i in range(len(pairs))]
    wr_l = [jnp.concatenate([x[:, :n], ro[:, :n]], axis=0).astype(BF16) for x, ro in zip(x_l, ro_l)]

    state = [state_ref[h] for h in range(RWKV_HEADS)]
    for ci in range(n_chunks):
        rows = rows_of(ci)
        cum_c = cum[rows]
        tot = cum_c[chunk - 1:chunk, :]
        e_rem = jnp.exp(tot - cum_c)
        bh, kh = (bb[rows] * e_rem).astype(BF16), (k2[rows] * e_rem).astype(BF16)
        p_tot = jnp.exp(tot)
        base = ci * RWKV_HEADS
        g_l = [_dot_nt(wr_l[base + h], state[h].astype(BF16)) for h in range(RWKV_HEADS)]
        for h in range(RWKV_HEADS):
            y_ref[rows, lanes_of(h)] = g_l[h][chunk:] + ro_l[base + h][:, n:]
        uv_l = [jnp.concatenate([g_l[h][:chunk] + x_l[base + h][:, n:], v_l[base + h].astype(F32)], axis=0)
                for h in range(RWKV_HEADS)]
        state = [state[h] * p_tot[:, lanes_of(h)]
                 + bdot32(uv_l[h].T.astype(BF16),
                          jnp.concatenate([bh[:, lanes_of(h)], kh[:, lanes_of(h)]], axis=0))
                 for h in range(RWKV_HEADS)]
    for h in range(RWKV_HEADS):
        state_ref[h] = state[h]

    y = y_ref[...]
    inv_n = 1.0 / n
    mean = _split_dot(y, seg) * inv_n
    dev = y - mean
    var = _split_dot(dev * dev, seg) * inv_n
    yn = dev * lax.rsqrt(var + RWKV_LN_EPS) * lnw_ref[...] + lnb_ref[...]
    bonus = _split_dot(r * k2 * rk_ref[...], seg) * v
    o_ref[0] = (yn + bonus).astype(o_ref.dtype)


def _rwkv(mix, mu, w0, w2, a0, a2, k_k, k_a, r_k, ln_w, ln_b):
    bn, s, mw = mix.shape
    w = BRANCH_W
    tb = min(RWKV_BLOCK, s)
    head = np.arange(w) // HEAD_DIM
    seg = jnp.asarray((head[:, None] == head[None, :]).astype(np.float32)).astype(BF16)
    row = lambda t: t.reshape(1, -1).astype(F32)
    args = (row(mu), row(w0), w2.astype(BF16), row(a0), a2.astype(BF16), row(k_k), row(k_a), row(r_k),
            row(ln_w), row(ln_b), seg)
    full = lambda a: pl.BlockSpec(a.shape, lambda b, i: tuple(0 for _ in a.shape))
    return pl.pallas_call(
        _rwkv_kernel,
        grid=(bn, s // tb),
        in_specs=[pl.BlockSpec((1, tb, mw), lambda b, i: (b, i, 0))] + [full(t) for t in args],
        out_specs=pl.BlockSpec((1, tb, w), lambda b, i: (b, i, 0)),
        out_shape=jax.ShapeDtypeStruct((bn, s, w), BF16),
        scratch_shapes=[pltpu.VMEM((RWKV_HEADS, HEAD_DIM, HEAD_DIM), F32),
                        pltpu.VMEM((SUBLANES, mw), F32),
                        pltpu.VMEM((tb, w), F32)],
        compiler_params=pltpu.CompilerParams(dimension_semantics=("parallel", "arbitrary"),
                                             vmem_limit_bytes=VMEM_LIMIT),
        name="rwkv7",
    )(mix, *args)


def _merge_kernel(on_ref, os_ref, or_ref, gn_ref, gs_ref, gr_ref, mg_ref, x_ref, gate_ref,
                  wup_ref, wout_ref, fnw_ref, o_ref, *, final):
    d = x_ref.shape[2]
    merged = None
    for i, (b_ref, g_ref) in enumerate(((on_ref, gn_ref), (os_ref, gs_ref), (or_ref, gr_ref))):
        branch = b_ref[0].astype(F32) * _silu(g_ref[0].astype(F32))
        up = _bdot(branch, wup_ref[i])
        term = _sigmoid(mg_ref[0, :, i * d:(i + 1) * d].astype(F32)) * up
        merged = term if merged is None else merged + term
    out = x_ref[0] + gate_ref[0] * _bdot(merged, wout_ref[...])
    if final:
        ms = jnp.mean(out * out, axis=-1, keepdims=True)
        out = out * lax.rsqrt(ms + NORM_EPS) * fnw_ref[...]
    o_ref[0] = out


def _merge(o_nsa, o_s5, o_rwkv, p, x, gate, w_up, w_out, fnw, final, tm=512):
    bn, s, d = x.shape
    w = BRANCH_W
    rows = lambda width: pl.BlockSpec((1, tm, width), lambda b, i: (b, i, 0))
    wupb, woutb = w_up.astype(BF16), w_out.astype(BF16)
    return pl.pallas_call(
        functools.partial(_merge_kernel, final=final),
        grid=(bn, s // tm),
        in_specs=[rows(w)] * 6 + [rows(3 * d), rows(d),
                                  pl.BlockSpec((1, 1, d), lambda b, i: (b, 0, 0)),
                                  pl.BlockSpec(wupb.shape, lambda b, i: (0, 0, 0)),
                                  pl.BlockSpec(woutb.shape, lambda b, i: (0, 0)),
                                  pl.BlockSpec((1, d), lambda b, i: (0, 0))],
        out_specs=rows(d),
        out_shape=jax.ShapeDtypeStruct((bn, s, d), F32),
        compiler_params=pltpu.CompilerParams(dimension_semantics=("parallel", "parallel"),
                                             vmem_limit_bytes=VMEM_LIMIT),
        name="merge_out",
    )(o_nsa, o_s5, o_rwkv, p["ng"], p["sg"], p["rg"], p["mg"], x, gate.reshape(bn, 1, d),
      wupb, woutb, fnw.reshape(1, d))


def kernel(x, c, norm_w, mod_w, mod_b, w_in, cmp_pos_k, cmp_pos_v, cmp_w1_k, cmp_w2_k, cmp_w1_v, cmp_w2_v, s5_a_re, s5_a_im, s5_b_re, s5_b_im, s5_c_re, s5_c_im, s5_d, s5_log_dt, s5_glu_w, s5_glu_b, rwkv_mu, rwkv_w0, rwkv_w2, rwkv_a0, rwkv_a2, rwkv_k_k, rwkv_k_a, rwkv_r_k, rwkv_ln_w, rwkv_ln_b, w_up, w_out, final_norm_w):
    bn, s, d = x.shape
    depth = norm_w.shape[0]
    cos_t, sin_t = _rope_tables(s)
    mod = _modulation(c, mod_w, mod_b)
    for l in range(depth):
        shift, scale, gate = mod[l, :, 0:d], mod[l, :, d:2 * d], mod[l, :, 2 * d:3 * d]
        p = _inproj(x, norm_w[l], scale, shift, cos_t, sin_t, _pack_w_in(w_in[l]))
        k_cmp, v_cmp = _compress(p["kc"], p["vc"], cmp_pos_k[l], cmp_pos_v[l], cmp_w1_k[l], cmp_w2_k[l],
                                 cmp_w1_v[l], cmp_w2_v[l])
        o_nsa = _nsa(p, k_cmp, v_cmp)
        s5p = _s5_params(s5_a_re[l], s5_a_im[l], s5_b_re[l], s5_b_im[l], s5_c_re[l], s5_c_im[l],
                         s5_log_dt[l])
        o_s5 = _s5(p["su"], s5p, s5_d[l], s5_glu_w[l], s5_glu_b[l])
        o_rwkv = _rwkv(p["mix"], rwkv_mu[l], rwkv_w0[l], rwkv_w2[l], rwkv_a0[l], rwkv_a2[l],
                       rwkv_k_k[l], rwkv_k_a[l], rwkv_r_k[l], rwkv_ln_w[l], rwkv_ln_b[l])
        x = _merge(o_nsa, o_s5, o_rwkv, p, x, gate, w_up[l], w_out[l], final_norm_w,
                   final=(l == depth - 1))
    return x
```

```python
import functools
import math

import numpy as np
import jax
import jax.numpy as jnp
from jax import lax
from jax.experimental import pallas as pl
from jax.experimental.pallas import tpu as pltpu

F32 = jnp.float32
BF16 = jnp.bfloat16

HEAD_DIM = 64
NSA_HEADS = 8
NSA_GROUPS = 2
NSA_HPG = NSA_HEADS // NSA_GROUPS
CMP_BLOCK = 32
CMP_STRIDE = 16
SLC_BLOCK = 64
SLC_TOP = 16
WINDOW = 512
Q_BLOCK = 128
RWKV_HEADS = 8
BRANCH_W = 512
LORA_W = 64
RWKV_MIX_W = 3 * BRANCH_W + 2 * LORA_W
ROPE_THETA = 10000.0
NORM_EPS = 1e-6
RWKV_LN_EPS = 64e-5
NEG_BIG = -1e30
MASK_BIG = 2.0 ** 100
M_INIT = -3.0e38
Q_SCALE = HEAD_DIM ** -0.5 * math.log2(math.e)

VMEM_LIMIT = 56 * 1024 * 1024
LANES = 128
SUBLANES = 8

C_Q = 0
C_KC, C_KS, C_KW, C_VC, C_VS, C_VW = 512, 640, 768, 896, 1024, 1152
C_G = 1280
C_NG = 1408
C_SU = 1920
C_SG = 2432
C_MIX = 2944
C_RG = C_MIX + RWKV_MIX_W
C_MG = C_RG + BRANCH_W
IN_PACKED = C_MG + 3 * 1024


def _sigmoid(z):
    return 0.5 * jnp.tanh(0.5 * z) + 0.5


def _silu(z):
    return z * _sigmoid(z)


def _bdot(a, b):
    return jnp.dot(a.astype(BF16), b.astype(BF16), preferred_element_type=F32)


def _dot_nt(a, b):
    return lax.dot_general(a, b, (((1,), (1,)), ((), ())), preferred_element_type=F32)


def _mod_kernel(c_ref, w_ref, b_ref, o_ref):
    cond = _silu(c_ref[...])
    o_ref[0] = _bdot(cond, w_ref[0]) + b_ref[0]


def _modulation(c, mod_w, mod_b):
    depth, d, d3 = mod_w.shape
    bn = c.shape[0]
    nj = d3 // d
    return pl.pallas_call(
        _mod_kernel,
        grid=(depth, nj),
        in_specs=[pl.BlockSpec((bn, d), lambda l, j: (0, 0)),
                  pl.BlockSpec((1, d, d), lambda l, j: (l, 0, j)),
                  pl.BlockSpec((1, 1, d), lambda l, j: (l, 0, j))],
        out_specs=pl.BlockSpec((1, bn, d), lambda l, j: (l, 0, j)),
        out_shape=jax.ShapeDtypeStruct((depth, bn, d3), F32),
        name="adaln_mod",
    )(c, mod_w, mod_b.reshape(depth, 1, d3))


_INPROJ_OUTS = (
    ("q", C_Q, 512, BF16), ("kc", C_KC, 128, F32), ("ks", C_KS, 128, BF16), ("kw", C_KW, 128, BF16),
    ("vc", C_VC, 128, F32), ("vs", C_VS, 128, BF16), ("vw", C_VW, 128, BF16), ("g", C_G, 128, F32),
    ("ng", C_NG, 512, BF16), ("su", C_SU, 512, BF16), ("sg", C_SG, 512, BF16),
    ("mix", C_MIX, RWKV_MIX_W, F32), ("rg", C_RG, 512, BF16), ("mg", C_MG, 3072, BF16))
_ROPED = ("q", "kc", "ks", "kw")


def _inproj_kernel(x_ref, nw_ref, sc_ref, sh_ref, cos_ref, sin_ref, w_ref, *out_refs):
    x = x_ref[0]
    tm = x.shape[0]
    ms = jnp.mean(x * x, axis=-1, keepdims=True)
    h = x * lax.rsqrt(ms + NORM_EPS) * nw_ref[...]
    h = h * (1.0 + sc_ref[0]) + sh_ref[0]
    hb = h.astype(BF16)
    cos = cos_ref[...]
    sin = sin_ref[...]
    lane = lax.broadcasted_iota(jnp.int32, (tm, LANES), 1)
    first_half = (lane % HEAD_DIM) < (HEAD_DIM // 2)

    def rope(t):
        half = HEAD_DIM // 2
        partner = jnp.where(first_half, pltpu.roll(t, LANES - half, 1), pltpu.roll(t, half, 1))
        return t * cos + partner * sin

    for (name, c0, width, dt), o_ref in zip(_INPROJ_OUTS, out_refs):
        step = min(width, 512)
        for j0 in range(0, width, step):
            w = min(step, width - j0)
            y = jnp.dot(hb, w_ref[:, c0 + j0:c0 + j0 + w], preferred_element_type=F32)
            if name in _ROPED:
                y = jnp.concatenate([rope(y[:, k:k + LANES]) for k in range(0, w, LANES)], axis=1)
            if name == "q":
                y = y * Q_SCALE
            o_ref[0, :, j0:j0 + w] = y.astype(dt)


def _inproj(x, nw, scale, shift, cos_t, sin_t, w_packed, tm=512):
    bn, s, d = x.shape
    out_shape = [jax.ShapeDtypeStruct((bn, s, w), dt) for (_, _, w, dt) in _INPROJ_OUTS]
    out_specs = [pl.BlockSpec((1, tm, w), lambda b, i: (b, i, 0)) for (_, _, w, _) in _INPROJ_OUTS]
    outs = pl.pallas_call(
        _inproj_kernel,
        grid=(bn, s // tm),
        in_specs=[pl.BlockSpec((1, tm, d), lambda b, i: (b, i, 0)),
                  pl.BlockSpec((1, d), lambda b, i: (0, 0)),
                  pl.BlockSpec((1, 1, d), lambda b, i: (b, 0, 0)),
                  pl.BlockSpec((1, 1, d), lambda b, i: (b, 0, 0)),
                  pl.BlockSpec((tm, LANES), lambda b, i: (i, 0)),
                  pl.BlockSpec((tm, LANES), lambda b, i: (i, 0)),
                  pl.BlockSpec((d, IN_PACKED), lambda b, i: (0, 0), pipeline_mode=pl.Buffered(1))],
        out_specs=out_specs,
        out_shape=out_shape,
        compiler_params=pltpu.CompilerParams(dimension_semantics=("parallel", "parallel"),
                                             vmem_limit_bytes=VMEM_LIMIT),
        name="inproj",
    )(x, nw.reshape(1, d), scale.reshape(bn, 1, d), shift.reshape(bn, 1, d), cos_t, sin_t, w_packed)
    return dict(zip([o[0] for o in _INPROJ_OUTS], outs))


def _pack_w_in(w_in):
    d = w_in.shape[0]
    sizes = (512, 768, 24, 512, 512, 512, RWKV_MIX_W, 512, 3072)
    offs = np.concatenate([[0], np.cumsum(sizes)])
    q, kv, g, ng, su, sg, mix, rg, mg = [w_in[:, offs[i]:offs[i + 1]] for i in range(len(sizes))]
    kc, vc, ks, vs, kw, vw = [kv[:, i * 128:(i + 1) * 128] for i in range(6)]
    gpad = jnp.pad(g, ((0, 0), (0, 128 - 24)))
    return jnp.concatenate([q, kc, ks, kw, vc, vs, vw, gpad, ng, su, sg, mix, rg, mg], axis=1).astype(BF16)


def _rope_tables(s):
    half = HEAD_DIM // 2
    inv = jnp.exp(-math.log(ROPE_THETA) * jnp.arange(half, dtype=F32) / half)
    ang = jnp.arange(s, dtype=F32)[:, None] * inv[None, :]
    cos, sin = jnp.cos(ang), jnp.sin(ang)
    cos_t = jnp.tile(cos, (1, LANES // half))
    sin_t = jnp.tile(jnp.concatenate([-sin, sin], axis=1), (1, LANES // HEAD_DIM))
    return cos_t, sin_t


def _compress_kernel(kc_ref, vc_ref, pk_ref, pv_ref, wkt_ref, wkb_ref, wvt_ref, wvb_ref,
                     w2k_ref, w2v_ref, ko_ref, vo_ref):
    def one(x_ref, p_ref, wt_ref, wb_ref, w2_ref, o_ref):
        kvw = x_ref.shape[2]
        n = x_ref.shape[1] // CMP_STRIDE
        top = bot = None
        for l in range(CMP_STRIDE):
            x_l = x_ref[0, pl.ds(l, n, stride=CMP_STRIDE), :]
            cols = slice(l * kvw, (l + 1) * kvw)
            t = _bdot(x_l + p_ref[0:1, cols], wt_ref[cols, :])
            b = _bdot(x_l + p_ref[1:2, cols], wb_ref[cols, :])
            top, bot = (t, b) if top is None else (top + t, bot + b)
        hid = top + pltpu.roll(bot, n - 1, 0)
        act = _silu(hid)
        hw = act.shape[1] // NSA_GROUPS
        for g in range(NSA_GROUPS):
            o_ref[0, g] = _bdot(act[:, g * hw:(g + 1) * hw], w2_ref[...])

    one(kc_ref, pk_ref, wkt_ref, wkb_ref, w2k_ref, ko_ref)
    one(vc_ref, pv_ref, wvt_ref, wvb_ref, w2v_ref, vo_ref)


def _compress_weights(pos, w1, w2):
    hid = w1.shape[1]
    half = CMP_BLOCK // 2
    w1r = w1.reshape(2, half, HEAD_DIM, hid)
    eye = jnp.eye(NSA_GROUPS, dtype=w1.dtype)
    wd = jnp.einsum("tldj,gh->tlgdhj", w1r, eye).reshape(2, half * NSA_GROUPS * HEAD_DIM,
                                                         NSA_GROUPS * hid)
    pr = pos.reshape(2, half, 1, HEAD_DIM)
    pt = jnp.broadcast_to(pr, (2, half, NSA_GROUPS, HEAD_DIM)).reshape(2, -1)
    return pt, wd[0].astype(BF16), wd[1].astype(BF16), w2.astype(BF16)


def _compress(kc, vc, pos_k, pos_v, w1k, w2k, w1v, w2v):
    bn, s, kvw = kc.shape
    n16 = s // CMP_STRIDE
    row_w = CMP_STRIDE * kvw
    pk, wkt, wkb, w2kb = _compress_weights(pos_k, w1k, w2k)
    pv, wvt, wvb, w2vb = _compress_weights(pos_v, w1v, w2v)
    hid2 = wkt.shape[1]
    full = lambda shape: pl.BlockSpec(shape, lambda b: tuple(0 for _ in shape))
    out = jax.ShapeDtypeStruct((bn, NSA_GROUPS, n16, HEAD_DIM), F32)
    return pl.pallas_call(
        _compress_kernel,
        grid=(bn,),
        in_specs=[pl.BlockSpec((1, s, kvw), lambda b: (b, 0, 0)),
                  pl.BlockSpec((1, s, kvw), lambda b: (b, 0, 0)),
                  full((2, row_w)), full((2, row_w)),
                  full((row_w, hid2)), full((row_w, hid2)), full((row_w, hid2)), full((row_w, hid2)),
                  full(w2kb.shape), full(w2vb.shape)],
        out_specs=[pl.BlockSpec((1, NSA_GROUPS, n16, HEAD_DIM), lambda b: (b, 0, 0, 0))] * 2,
        out_shape=[out, out],
        compiler_params=pltpu.CompilerParams(dimension_semantics=("parallel",),
                                             vmem_limit_bytes=VMEM_LIMIT),
        name="nsa_compress",
    )(kc, vc, pk, pv, wkt, wkb, wvt, wvb, w2kb, w2vb)


SEL_TILE = 256
KEY_TILE = 128
WIN_TILES = WINDOW // KEY_TILE + 1
MASK_W = 64


def _nsa_kernel(q_ref, g_ref, kcmp_ref, vcmp_ref, ks_ref, vs_ref, kw_ref, vw_ref, c2s_ref,
                o_ref, kaug_ref, vst_ref, vwt_ref, vct_ref, sca_ref, scb_ref, imp_ref, *, top):
    qb = pl.program_id(1)
    s0 = qb * Q_BLOCK
    s = ks_ref.shape[1]
    nc = kcmp_ref.shape[2]
    d = HEAD_DIM
    nq = Q_BLOCK

    @pl.when(qb == 0)
    def _():
        blk = lax.broadcasted_iota(jnp.int32, (s, MASK_W), 0) // SLC_BLOCK
        col = lax.broadcasted_iota(jnp.int32, (s, MASK_W), 1)
        onehot = jnp.where(blk == col, 1.0, 0.0).astype(BF16)
        for g in range(NSA_GROUPS):
            kaug_ref[g, :, 0:d] = ks_ref[0, :, g * d:(g + 1) * d]
            kaug_ref[g, :, d:d + MASK_W] = onehot
            vct_ref[g] = vcmp_ref[0, g].T.astype(BF16)

        def transpose_tile(j, carry):
            r0 = pl.multiple_of(j * SEL_TILE, SEL_TILE)
            vst_ref[j] = vs_ref[0, pl.ds(r0, SEL_TILE), :].astype(F32).T.astype(BF16)
            for half in range(SEL_TILE // KEY_TILE):
                r1 = pl.multiple_of(r0 + half * KEY_TILE, KEY_TILE)
                vwt_ref[j * (SEL_TILE // KEY_TILE) + half] = (
                    vw_ref[0, pl.ds(r1, KEY_TILE), :].astype(F32).T.astype(BF16))
            return carry

        lax.fori_loop(0, s // SEL_TILE, transpose_tile, 0)

    q_all = q_ref[0]
    gates_t = _sigmoid(g_ref[0]).T
    lane_minus_sub = (lax.broadcasted_iota(jnp.int32, (KEY_TILE, nq), 1)
                      - lax.broadcasted_iota(jnp.int32, (KEY_TILE, nq), 0))
    w_start = jnp.maximum(s0 - WINDOW, 0)
    w_tile0 = w_start // KEY_TILE

    groups = range(NSA_GROUPS)
    heads = range(NSA_HEADS)
    grp = lambda h: h // NSA_HPG
    gd = [slice(g * d, (g + 1) * d) for g in groups]
    q_h = [q_all[:, h * d:(h + 1) * d] for h in heads]

    n_row = lax.broadcasted_iota(jnp.int32, (nc, NSA_HPG * nq), 0)
    t_col = s0 + lax.broadcasted_iota(jnp.int32, (nc, NSA_HPG * nq), 1) % nq
    valid_t = (CMP_STRIDE * n_row + CMP_BLOCK - 1) <= t_col
    t_query = s0 + lax.broadcasted_iota(jnp.int32, (1, NSA_HPG * nq), 1) % nq
    j_idx = lax.broadcasted_iota(jnp.int32, (MASK_W, nq), 0)
    blk_t = (s0 + lax.broadcasted_iota(jnp.int32, (MASK_W, nq), 1)) // SLC_BLOCK
    causal = j_idx <= blk_t
    forced = (j_idx == 0) | (j_idx == blk_t)
    c2s = c2s_ref[...]
    o_ct, imp = [], []
    for g in groups:
        qg = jnp.concatenate(q_h[g * NSA_HPG:(g + 1) * NSA_HPG], axis=0)
        st = _dot_nt(kcmp_ref[0, g].astype(BF16), qg)
        stm = jnp.where(valid_t, st, NEG_BIG)
        et = jnp.exp2(stm - jnp.max(stm, axis=0, keepdims=True))
        inv = jnp.where(t_query >= CMP_BLOCK - 1, 1.0 / jnp.sum(et, axis=0, keepdims=True), 0.0)
        pt = et * inv
        o_ct.append(jnp.dot(vct_ref[g], pt.astype(BF16), preferred_element_type=F32))
        psum = pt[:, 0:nq]
        for r in range(1, NSA_HPG):
            psum = psum + pt[:, r * nq:(r + 1) * nq]
        p_hi = psum.astype(BF16)
        p_lo = (psum - p_hi.astype(F32)).astype(BF16)
        imp_g = (jnp.dot(c2s, p_hi, preferred_element_type=F32)
                 + jnp.dot(c2s, p_lo, preferred_element_type=F32))
        imp_g = jnp.where(forced, jnp.inf, jnp.where(causal, imp_g, -jnp.inf))
        imp_ref[g] = imp_g
        imp.append(imp_g)

    rank_rows = 4

    def rank_body(i4, cnts):
        cnts = list(cnts)
        for i in [rank_rows * i4 + k for k in range(rank_rows)]:
            tie = jnp.where(j_idx > i, 1, 0)
            for g in groups:
                row = imp_ref[g, pl.ds(i, 1), :]
                cnts[g] = cnts[g] + jnp.where(row > imp[g], 1, jnp.where(row == imp[g], tie, 0))
        return tuple(cnts)

    n_seen = (s0 + nq - 1) // SLC_BLOCK + 1
    cnts = lax.fori_loop(0, (n_seen + rank_rows - 1) // rank_rows, rank_body,
                         tuple(jnp.zeros((MASK_W, nq), jnp.int32) for _ in groups))
    q_aug = []
    for g in groups:
        sel_t = jnp.where(causal, jnp.where(cnts[g] < top, 0.0, -MASK_BIG), -MASK_BIG)
        mterm = sel_t.T.astype(BF16)
        q_aug += [jnp.concatenate([q_h[g * NSA_HPG + r], mterm], axis=1) for r in range(NSA_HPG)]

    sub_tiles = SEL_TILE // KEY_TILE

    q_aug_g = [jnp.concatenate(q_aug[g * NSA_HPG:(g + 1) * NSA_HPG], axis=0) for g in groups]
    head_lanes = lambda h: slice((h % NSA_HPG) * nq, (h % NSA_HPG + 1) * nq)

    def sel_scores(i, dst_ref):
        for g in groups:
            kt = kaug_ref[g, pl.ds(pl.multiple_of(i * SEL_TILE, SEL_TILE), SEL_TILE), :]
            dst_ref[g] = _dot_nt(kt, q_aug_g[g])

    def sel_step(i, src_ref, carry, masked):
        k0 = i * SEL_TILE
        sc = [src_ref[grp(h), :, head_lanes(h)] for h in heads]
        if masked:
            sc = [jnp.concatenate(
                [jnp.where(lane_minus_sub + (s0 - k0 - j * KEY_TILE) >= 0,
                           x[j * KEY_TILE:(j + 1) * KEY_TILE], -MASK_BIG)
                 for j in range(sub_tiles)], axis=0) for x in sc]
        m_new = [jnp.maximum(carry[h][0], jnp.max(sc[h], axis=0, keepdims=True)) for h in heads]
        alpha = [jnp.exp2(carry[h][0] - m_new[h]) for h in heads]
        p = [jnp.exp2(sc[h] - m_new[h]) for h in heads]
        l_new = [alpha[h] * carry[h][1] + jnp.sum(p[h], axis=0, keepdims=True) for h in heads]
        pb = [jnp.concatenate([p[h].astype(BF16) for h in range(g * NSA_HPG, (g + 1) * NSA_HPG)], axis=1)
              for g in groups]
        pv = [jnp.dot(vst_ref[i, gd[g], :], pb[g], preferred_element_type=F32) for g in groups]
        return tuple((m_new[h], l_new[h], alpha[h] * carry[h][2] + pv[grp(h)][:, head_lanes(h)])
                     for h in heads)

    init = tuple((jnp.full((1, nq), M_INIT, F32), jnp.zeros((1, nq), F32), jnp.zeros((d, nq), F32))
                 for _ in heads)
    n_full = s0 // SEL_TILE

    sel_scores(0, sca_ref)

    zeros = jnp.zeros((nq, d), BF16)
    q_w = [jnp.concatenate([q_h[h], zeros] if grp(h) == 0 else [zeros, q_h[h]], axis=1)
           for h in heads]
    q_w_g = [jnp.concatenate(q_w[g * NSA_HPG:(g + 1) * NSA_HPG], axis=0) for g in groups]
    k_tiles = [kw_ref[0, pl.ds(pl.multiple_of(w_start + j * KEY_TILE, KEY_TILE), KEY_TILE), :]
               for j in range(WIN_TILES)]
    win_ok = [(lane_minus_sub + (s0 - w_start - j * KEY_TILE)).astype(jnp.uint32) < WINDOW
              for j in range(WIN_TILES)]

    def win_scores(g):
        out = []
        for j in range(WIN_TILES):
            sc = _dot_nt(k_tiles[j], q_w_g[g])
            out.append([jnp.where(win_ok[j], sc[:, r * nq:(r + 1) * nq], -MASK_BIG) for r in range(NSA_HPG)])
        return out

    def win_softmax(sc):
        p_all, l_all = [], []
        for r in range(NSA_HPG):
            m = sc[0][r]
            for j in range(1, WIN_TILES):
                m = jnp.maximum(m, sc[j][r])
            m = jnp.max(m, axis=0, keepdims=True)
            p = [jnp.exp2(sc[j][r] - m) for j in range(WIN_TILES)]
            tot = p[0]
            for x in p[1:]:
                tot = tot + x
            p_all.append([x.astype(BF16) for x in p])
            l_all.append(jnp.sum(tot, axis=0, keepdims=True))
        return p_all, l_all

    def win_out(g, pl_):
        p_all, l_all = pl_
        acc = None
        for j in range(WIN_TILES):
            pj = jnp.concatenate([p_all[r][j] for r in range(NSA_HPG)], axis=1)
            term = jnp.dot(vwt_ref[w_tile0 + j, gd[g], :], pj, preferred_element_type=F32)
            acc = term if acc is None else acc + term
        return [acc[:, r * nq:(r + 1) * nq] / l_all[r] for r in range(NSA_HPG)]

    sc_w = [win_scores(g) for g in groups]
    p_w = [win_softmax(sc_w[g]) for g in groups]
    o_wt = []
    for g in groups:
        o_wt += win_out(g, p_w[g])

    def sel_body(k, carry):
        sel_scores(2 * k + 1, scb_ref)
        carry = sel_step(2 * k, sca_ref, carry, masked=False)
        sel_scores(2 * k + 2, sca_ref)
        return sel_step(2 * k + 1, scb_ref, carry, masked=False)

    carry = lax.fori_loop(0, n_full // 2, sel_body, init)

    def odd_tail(carry):
        sel_scores(n_full, scb_ref)
        carry = sel_step(n_full - 1, sca_ref, carry, masked=False)
        return sel_step(n_full, scb_ref, carry, masked=True)

    carry = lax.cond(n_full % 2 == 1, odd_tail,
                     lambda c: sel_step(n_full, sca_ref, c, masked=True), carry)
    o_st = [acc / l for (_, l, acc) in carry]

    for pair in range(NSA_HEADS // 2):
        halves = []
        for h in (2 * pair, 2 * pair + 1):
            r = h % NSA_HPG
            halves.append(gates_t[3 * h:3 * h + 1, :] * o_ct[grp(h)][:, r * nq:(r + 1) * nq]
                          + gates_t[3 * h + 1:3 * h + 2, :] * o_st[h]
                          + gates_t[3 * h + 2:3 * h + 3, :] * o_wt[h])
        o_ref[0, :, 2 * pair * d:(2 * pair + 2) * d] = jnp.concatenate(halves, axis=0).T.astype(o_ref.dtype)


def _cmp_to_slc_t(s):
    n_cmp = s // CMP_STRIDE
    n_slc = s // SLC_BLOCK
    cs = CMP_STRIDE * np.arange(n_cmp)[:, None]
    ss = SLC_BLOCK * np.arange(n_slc)[None, :]
    overlap = np.clip(np.minimum(cs + CMP_BLOCK, ss + SLC_BLOCK) - np.maximum(cs, ss), 0, None)
    m = np.zeros((MASK_W, n_cmp), np.float32)
    m[:n_slc] = (overlap / CMP_BLOCK).T
    return jnp.asarray(m).astype(BF16)


def _nsa(p, k_cmp, v_cmp):
    q = p["q"]
    bn, s, _ = q.shape
    assert s // SLC_BLOCK <= MASK_W and s % SEL_TILE == 0
    top = min(SLC_TOP, s // SLC_BLOCK)
    nc = k_cmp.shape[2]
    kvw = NSA_GROUPS * HEAD_DIM
    per_b = lambda w: pl.BlockSpec((1, s, w), lambda b, i: (b, 0, 0))
    return pl.pallas_call(
        functools.partial(_nsa_kernel, top=top),
        grid=(bn, s // Q_BLOCK),
        in_specs=[pl.BlockSpec((1, Q_BLOCK, NSA_HEADS * HEAD_DIM), lambda b, i: (b, i, 0)),
                  pl.BlockSpec((1, Q_BLOCK, LANES), lambda b, i: (b, i, 0)),
                  pl.BlockSpec((1, NSA_GROUPS, nc, HEAD_DIM), lambda b, i: (b, 0, 0, 0)),
                  pl.BlockSpec((1, NSA_GROUPS, nc, HEAD_DIM), lambda b, i: (b, 0, 0, 0)),
                  per_b(kvw), per_b(kvw), per_b(kvw), per_b(kvw),
                  pl.BlockSpec((MASK_W, nc), lambda b, i: (0, 0))],
        out_specs=pl.BlockSpec((1, Q_BLOCK, NSA_HEADS * HEAD_DIM), lambda b, i: (b, i, 0)),
        out_shape=jax.ShapeDtypeStruct((bn, s, NSA_HEADS * HEAD_DIM), BF16),
        scratch_shapes=[pltpu.VMEM((NSA_GROUPS, s, HEAD_DIM + MASK_W), BF16),
                        pltpu.VMEM((s // SEL_TILE, kvw, SEL_TILE), BF16),
                        pltpu.VMEM((s // KEY_TILE, kvw, KEY_TILE), BF16),
                        pltpu.VMEM((NSA_GROUPS, HEAD_DIM, nc), BF16),
                        pltpu.VMEM((NSA_GROUPS, SEL_TILE, NSA_HPG * Q_BLOCK), F32),
                        pltpu.VMEM((NSA_GROUPS, SEL_TILE, NSA_HPG * Q_BLOCK), F32),
                        pltpu.VMEM((NSA_GROUPS, MASK_W, Q_BLOCK), F32)],
        compiler_params=pltpu.CompilerParams(dimension_semantics=("parallel", "arbitrary"),
                                             vmem_limit_bytes=VMEM_LIMIT),
        name="nsa_attention",
    )(q, p["g"], k_cmp, v_cmp, p["ks"], p["vs"], p["kw"], p["vw"], _cmp_to_slc_t(s))


S5_STRIP = 512


def _gelu_tanh(y):
    return 0.5 * y * (1.0 + jnp.tanh(math.sqrt(2.0 / math.pi) * (y + 0.044715 * (y * y * y))))


def _s5_kernel(u_ref, bre_ref, bim_ref, are_ref, aim_ref, cre_ref, cim_ref, d_ref, gw_ref, gb_ref,
               o_ref, xre_ref, xim_ref, sre_ref, sim_ref):
    bn, tc, width = u_ref.shape
    nstate = are_ref.shape[1]

    @pl.when(pl.program_id(0) == 0)
    def _():
        sre_ref[...] = jnp.zeros_like(sre_ref)
        sim_ref[...] = jnp.zeros_like(sim_ref)

    u = jnp.swapaxes(u_ref[...].astype(F32), 0, 1).reshape(tc * bn, width)
    ub = u.astype(BF16)
    n_strips = nstate // S5_STRIP
    cw = width // n_strips
    for k in range(n_strips):
        cols = pl.ds(k * S5_STRIP, S5_STRIP)
        xre_ref[:, cols] = jnp.dot(ub[:, k * cw:(k + 1) * cw], bre_ref[k], preferred_element_type=F32)
        xim_ref[:, cols] = jnp.dot(ub[:, k * cw:(k + 1) * cw], bim_ref[k], preferred_element_type=F32)

    for k in range(n_strips):
        c0 = k * S5_STRIP
        cols = pl.ds(c0, S5_STRIP)
        a_r = jnp.broadcast_to(are_ref[:, c0:c0 + S5_STRIP], (bn, S5_STRIP))
        a_i = jnp.broadcast_to(aim_ref[:, c0:c0 + S5_STRIP], (bn, S5_STRIP))

        def step(t, carry):
            x_r, x_i = carry
            r0 = pl.multiple_of(t * bn, bn)
            n_r = a_r * x_r - a_i * x_i + xre_ref[pl.ds(r0, bn), cols]
            n_i = a_r * x_i + a_i * x_r + xim_ref[pl.ds(r0, bn), cols]
            xre_ref[pl.ds(r0, bn), cols] = n_r
            xim_ref[pl.ds(r0, bn), cols] = n_i
            return n_r, n_i

        x_r, x_i = lax.fori_loop(0, tc, step, (sre_ref[:, cols], sim_ref[:, cols]), unroll=8)
        sre_ref[:, cols] = x_r
        sim_ref[:, cols] = x_i

    y = jnp.concatenate(
        [jnp.dot(xre_ref[:, k * S5_STRIP:(k + 1) * S5_STRIP].astype(BF16), cre_ref[k],
                 preferred_element_type=F32)
         + jnp.dot(xim_ref[:, k * S5_STRIP:(k + 1) * S5_STRIP].astype(BF16), cim_ref[k],
                   preferred_element_type=F32) for k in range(n_strips)], axis=1) + d_ref[...] * u
    y = _gelu_tanh(y)
    z = _bdot(y, gw_ref[...]) + gb_ref[...]
    out = (y * _sigmoid(z)).reshape(tc, bn, width)
    o_ref[...] = jnp.swapaxes(out, 0, 1).astype(o_ref.dtype)


def _s5_params(a_re, a_im, b_re, b_im, c_re, c_im, log_dt):
    g, p = a_re.shape
    c = b_re.shape[-1]
    a_re, a_im = a_re.astype(F32), a_im.astype(F32)
    b_re, b_im = b_re.astype(F32), b_im.astype(F32)
    dt = jnp.exp(log_dt.astype(F32))[:, None]
    mag = jnp.exp(a_re * dt)
    abar_re, abar_im = mag * jnp.cos(a_im * dt), mag * jnp.sin(a_im * dt)
    den = a_re * a_re + a_im * a_im
    f_re = ((abar_re - 1.0) * a_re + abar_im * a_im) / den
    f_im = (abar_im * a_re - (abar_re - 1.0) * a_im) / den
    bbar_re = f_re[..., None] * b_re - f_im[..., None] * b_im
    bbar_im = f_re[..., None] * b_im + f_im[..., None] * b_re
    gs = S5_STRIP // p
    nb = g // gs
    eye = jnp.eye(gs, dtype=F32)
    pack_b = lambda m: jnp.einsum("kgpc,gh->kgchp", m.reshape(nb, gs, p, c), eye).reshape(
        nb, gs * c, gs * p).astype(BF16)
    pack_c = lambda m: jnp.einsum("kgcp,gh->kgphc", m.reshape(nb, gs, c, p), eye).reshape(
        nb, gs * p, gs * c).astype(BF16)
    return (pack_b(bbar_re), pack_b(bbar_im), abar_re.reshape(1, g * p), abar_im.reshape(1, g * p),
            pack_c(c_re.astype(F32)), pack_c(-c_im.astype(F32)))


def _s5(u, params, d, glu_w, glu_b, tc=128):
    bn, s, width = u.shape
    assert bn == SUBLANES
    bre, bim, are, aim, cre, cim = params
    nstate = are.shape[1]
    full = lambda a: pl.BlockSpec(a.shape, lambda i: tuple(0 for _ in a.shape))
    d2, gb2, gwb = d.reshape(1, width), glu_b.reshape(1, width), glu_w.astype(BF16)
    return pl.pallas_call(
        _s5_kernel,
        grid=(s // tc,),
        in_specs=[pl.BlockSpec((bn, tc, width), lambda i: (0, i, 0)),
                  full(bre), full(bim), full(are), full(aim), full(cre), full(cim),
                  full(d2), full(gwb), full(gb2)],
        out_specs=pl.BlockSpec((bn, tc, width), lambda i: (0, i, 0)),
        out_shape=jax.ShapeDtypeStruct((bn, s, width), BF16),
        scratch_shapes=[pltpu.VMEM((tc * bn, nstate), F32), pltpu.VMEM((tc * bn, nstate), F32),
                        pltpu.VMEM((bn, nstate), F32), pltpu.VMEM((bn, nstate), F32)],
        compiler_params=pltpu.CompilerParams(dimension_semantics=("arbitrary",),
                                             vmem_limit_bytes=VMEM_LIMIT),
        name="s5_scan",
    )(u, bre, bim, are, aim, cre, cim, d2, gwb, gb2)


RWKV_CHUNK = 64
RWKV_BLOCK = 256


def _split_dot(x, seg):
    hi = x.astype(BF16)
    lo = (x - hi.astype(F32)).astype(BF16)
    return (jnp.dot(hi, seg, preferred_element_type=F32) + jnp.dot(lo, seg, preferred_element_type=F32))


def _rwkv_kernel(mix_ref, mu_ref, w0_ref, w2_ref, a0_ref, a2_ref, kk_ref, ka_ref, rk_ref,
                 lnw_ref, lnb_ref, seg_ref, o_ref, state_ref, carry_ref, y_ref):
    n = HEAD_DIM
    chunk = RWKV_CHUNK
    tb = mix_ref.shape[1]
    w = BRANCH_W

    @pl.when(pl.program_id(1) == 0)
    def _():
        state_ref[...] = jnp.zeros_like(state_ref)
        carry_ref[...] = jnp.zeros_like(carry_ref)

    mix = mix_ref[0]
    row_m = lax.broadcasted_iota(jnp.int32, mix.shape, 0)
    prev = jnp.where(row_m == 0, carry_ref[0:1, :], pltpu.roll(mix, 1, 0))
    carry_ref[0:1, :] = mix[tb - 1:tb, :]
    xs = mix + (prev - mix) * mu_ref[...]
    r, k, v = xs[:, 0:w], xs[:, w:2 * w], xs[:, 2 * w:3 * w]
    wl, al = xs[:, 3 * w:3 * w + LORA_W], xs[:, 3 * w + LORA_W:3 * w + 2 * LORA_W]

    lw = w0_ref[...] + _bdot(jnp.tanh(wl), w2_ref[...])
    z = -lw
    softplus = jnp.maximum(z, 0.0) + jnp.log(1.0 + jnp.exp(-jnp.abs(z)))
    ld = -jnp.exp(-softplus - 0.5)
    a = _sigmoid(a0_ref[...] + _bdot(al, a2_ref[...]))
    seg = seg_ref[...]
    kk = k * kk_ref[...]
    kk = kk / jnp.maximum(jnp.sqrt(_split_dot(kk * kk, seg)), 1e-12)
    k2 = k * (1.0 + (a - 1.0) * ka_ref[...])
    aa = -kk
    bb = kk * a

    row = lax.broadcasted_iota(jnp.int32, (tb, w), 0) % chunk
    cum = ld
    sh = 1
    while sh < chunk:
        cum = cum + jnp.where(row >= sh, pltpu.roll(cum, sh, 0), 0.0)
        sh *= 2
    at = (aa * jnp.exp(cum - ld)).astype(BF16)
    rt = (r * jnp.exp(cum)).astype(BF16)
    e_neg = jnp.exp(-cum)
    bt, kt = (bb * e_neg).astype(BF16), (k2 * e_neg).astype(BF16)
    vb = v.astype(BF16)

    ti = lax.broadcasted_iota(jnp.int32, (2 * chunk, chunk), 0)
    si = lax.broadcasted_iota(jnp.int32, (2 * chunk, chunk), 1)
    strict = si < ti
    lower2 = si < jnp.where(ti < chunk, ti, ti - chunk + 1)
    n_double = int(math.log2(chunk)) - 1

    n_chunks = tb // chunk
    pairs = [(ci, h) for ci in range(n_chunks) for h in range(RWKV_HEADS)]
    rows_of = lambda ci: slice(ci * chunk, (ci + 1) * chunk)
    lanes_of = lambda h: slice(h * n, (h + 1) * n)
    bdot32 = lambda p_, q_: jnp.dot(p_, q_, preferred_element_type=F32)

    at_l = [at[rows_of(ci), lanes_of(h)] for ci, h in pairs]
    rt_l = [rt[rows_of(ci), lanes_of(h)] for ci, h in pairs]
    v_l = [vb[rows_of(ci), lanes_of(h)] for ci, h in pairs]
    amat_l = [_dot_nt(jnp.concatenate([at_l[i], rt_l[i]], axis=0),
                      jnp.concatenate([bt[rows_of(ci), lanes_of(h)], kt[rows_of(ci), lanes_of(h)]], axis=0))
              for i, (ci, h) in enumerate(pairs)]
    pw_l = [jnp.where(strict[:chunk], m[:chunk, :chunk], 0.0).astype(BF16) for m in amat_l]
    arb_l = [jnp.where(lower2[chunk:], m[chunk:, :chunk], 0.0).astype(BF16) for m in amat_l]
    axk_l = [jnp.where(lower2, m[:, chunk:], 0.0).astype(BF16) for m in amat_l]
    xv_l = [bdot32(axk_l[i], v_l[i]) for i in range(len(pairs))]
    x_l = [jnp.concatenate([at_l[i].astype(F32), xv_l[i][:chunk]], axis=1) for i in range(len(pairs))]
    x_l = [x + bdot32(pw, x.astype(BF16)) for x, pw in zip(x_l, pw_l)]
    for _ in range(n_double):
        pw_l = [bdot32(pw, pw).astype(BF16) for pw in pw_l]
        x_l = [x + bdot32(pw, x.astype(BF16)) for x, pw in zip(x_l, pw_l)]
    ro_l = [jnp.concatenate([rt_l[i].astype(F32), xv_l[i][chunk:]], axis=1)
            + bdot32(arb_l[i], x_l[i].astype(BF16)) for i in range(len(pairs))]
    wr_l = [jnp.concatenate([x[:, :n], ro[:, :n]], axis=0).astype(BF16) for x, ro in zip(x_l, ro_l)]

    state = [state_ref[h] for h in range(RWKV_HEADS)]
    for ci in range(n_chunks):
        rows = rows_of(ci)
        cum_c = cum[rows]
        tot = cum_c[chunk - 1:chunk, :]
        e_rem = jnp.exp(tot - cum_c)
        bh, kh = (bb[rows] * e_rem).astype(BF16), (k2[rows] * e_rem).astype(BF16)
        p_tot = jnp.exp(tot)
        base = ci * RWKV_HEADS
        g_l = [_dot_nt(wr_l[base + h], state[h].astype(BF16)) for h in range(RWKV_HEADS)]
        for h in range(RWKV_HEADS):
            y_ref[rows, lanes_of(h)] = g_l[h][chunk:] + ro_l[base + h][:, n:]
        uv_l = [jnp.concatenate([g_l[h][:chunk] + x_l[base + h][:, n:], v_l[base + h].astype(F32)], axis=0)
                for h in range(RWKV_HEADS)]
        state = [state[h] * p_tot[:, lanes_of(h)]
                 + bdot32(uv_l[h].T.astype(BF16),
                          jnp.concatenate([bh[:, lanes_of(h)], kh[:, lanes_of(h)]], axis=0))
                 for h in range(RWKV_HEADS)]
    for h in range(RWKV_HEADS):
        state_ref[h] = state[h]

    y = y_ref[...]
    inv_n = 1.0 / n
    mean = _split_dot(y, seg) * inv_n
    dev = y - mean
    var = _split_dot(dev * dev, seg) * inv_n
    yn = dev * lax.rsqrt(var + RWKV_LN_EPS) * lnw_ref[...] + lnb_ref[...]
    bonus = _split_dot(r * k2 * rk_ref[...], seg) * v
    o_ref[0] = (yn + bonus).astype(o_ref.dtype)


def _rwkv(mix, mu, w0, w2, a0, a2, k_k, k_a, r_k, ln_w, ln_b):
    bn, s, mw = mix.shape
    w = BRANCH_W
    tb = min(RWKV_BLOCK, s)
    head = np.arange(w) // HEAD_DIM
    seg = jnp.asarray((head[:, None] == head[None, :]).astype(np.float32)).astype(BF16)
    row = lambda t: t.reshape(1, -1).astype(F32)
    args = (row(mu), row(w0), w2.astype(BF16), row(a0), a2.astype(BF16), row(k_k), row(k_a), row(r_k),
            row(ln_w), row(ln_b), seg)
    full = lambda a: pl.BlockSpec(a.shape, lambda b, i: tuple(0 for _ in a.shape))
    return pl.pallas_call(
        _rwkv_kernel,
        grid=(bn, s // tb),
        in_specs=[pl.BlockSpec((1, tb, mw), lambda b, i: (b, i, 0))] + [full(t) for t in args],
        out_specs=pl.BlockSpec((1, tb, w), lambda b, i: (b, i, 0)),
        out_shape=jax.ShapeDtypeStruct((bn, s, w), BF16),
        scratch_shapes=[pltpu.VMEM((RWKV_HEADS, HEAD_DIM, HEAD_DIM), F32),
                        pltpu.VMEM((SUBLANES, mw), F32),
                        pltpu.VMEM((tb, w), F32)],
        compiler_params=pltpu.CompilerParams(dimension_semantics=("parallel", "arbitrary"),
                                             vmem_limit_bytes=VMEM_LIMIT),
        name="rwkv7",
    )(mix, *args)


def _merge_kernel(on_ref, os_ref, or_ref, gn_ref, gs_ref, gr_ref, mg_ref, x_ref, gate_ref,
                  wup_ref, wout_ref, fnw_ref, o_ref, *, final):
    d = x_ref.shape[2]
    merged = None
    for i, (b_ref, g_ref) in enumerate(((on_ref, gn_ref), (os_ref, gs_ref), (or_ref, gr_ref))):
        branch = b_ref[0].astype(F32) * _silu(g_ref[0].astype(F32))
        up = _bdot(branch, wup_ref[i])
        term = _sigmoid(mg_ref[0, :, i * d:(i + 1) * d].astype(F32)) * up
        merged = term if merged is None else merged + term
    out = x_ref[0] + gate_ref[0] * _bdot(merged, wout_ref[...])
    if final:
        ms = jnp.mean(out * out, axis=-1, keepdims=True)
        out = out * lax.rsqrt(ms + NORM_EPS) * fnw_ref[...]
    o_ref[0] = out


def _merge(o_nsa, o_s5, o_rwkv, p, x, gate, w_up, w_out, fnw, final, tm=512):
    bn, s, d = x.shape
    w = BRANCH_W
    rows = lambda width: pl.BlockSpec((1, tm, width), lambda b, i: (b, i, 0))
    wupb, woutb = w_up.astype(BF16), w_out.astype(BF16)
    return pl.pallas_call(
        functools.partial(_merge_kernel, final=final),
        grid=(bn, s // tm),
        in_specs=[rows(w)] * 6 + [rows(3 * d), rows(d),
                                  pl.BlockSpec((1, 1, d), lambda b, i: (b, 0, 0)),
                                  pl.BlockSpec(wupb.shape, lambda b, i: (0, 0, 0)),
                                  pl.BlockSpec(woutb.shape, lambda b, i: (0, 0)),
                                  pl.BlockSpec((1, d), lambda b, i: (0, 0))],
        out_specs=rows(d),
        out_shape=jax.ShapeDtypeStruct((bn, s, d), F32),
        compiler_params=pltpu.CompilerParams(dimension_semantics=("parallel", "parallel"),
                                             vmem_limit_bytes=VMEM_LIMIT),
        name="merge_out",
    )(o_nsa, o_s5, o_rwkv, p["ng"], p["sg"], p["rg"], p["mg"], x, gate.reshape(bn, 1, d),
      wupb, woutb, fnw.reshape(1, d))


def kernel(x, c, norm_w, mod_w, mod_b, w_in, cmp_pos_k, cmp_pos_v, cmp_w1_k, cmp_w2_k, cmp_w1_v, cmp_w2_v, s5_a_re, s5_a_im, s5_b_re, s5_b_im, s5_c_re, s5_c_im, s5_d, s5_log_dt, s5_glu_w, s5_glu_b, rwkv_mu, rwkv_w0, rwkv_w2, rwkv_a0, rwkv_a2, rwkv_k_k, rwkv_k_a, rwkv_r_k, rwkv_ln_w, rwkv_ln_b, w_up, w_out, final_norm_w):
    bn, s, d = x.shape
    depth = norm_w.shape[0]
    cos_t, sin_t = _rope_tables(s)
    mod = _modulation(c, mod_w, mod_b)
    for l in range(depth):
        shift, scale, gate = mod[l, :, 0:d], mod[l, :, d:2 * d], mod[l, :, 2 * d:3 * d]
        p = _inproj(x, norm_w[l], scale, shift, cos_t, sin_t, _pack_w_in(w_in[l]))
        k_cmp, v_cmp = _compress(p["kc"], p["vc"], cmp_pos_k[l], cmp_pos_v[l], cmp_w1_k[l], cmp_w2_k[l],
                                 cmp_w1_v[l], cmp_w2_v[l])
        o_nsa = _nsa(p, k_cmp, v_cmp)
        s5p = _s5_params(s5_a_re[l], s5_a_im[l], s5_b_re[l], s5_b_im[l], s5_c_re[l], s5_c_im[l],
                         s5_log_dt[l])
        o_s5 = _s5(p["su"], s5p, s5_d[l], s5_glu_w[l], s5_glu_b[l])
        o_rwkv = _rwkv(p["mix"], rwkv_mu[l], rwkv_w0[l], rwkv_w2[l], rwkv_a0[l], rwkv_a2[l],
                       rwkv_k_k[l], rwkv_k_a[l], rwkv_r_k[l], rwkv_ln_w[l], rwkv_ln_b[l])
        x = _merge(o_nsa, o_s5, o_rwkv, p, x, gate, w_up[l], w_out[l], final_norm_w,
                   final=(l == depth - 1))
    return x
```
